```python
import jax, jax.numpy as jnp
from jax import lax
import numpy as np

D_MODEL = 2048
BATCH = 2
SEQ = 8192
DEPTH = 4
DEC_BATCH = 16
DEC_SEQ = 2048
PAST_LEN = 128

N_TOKEN_MIXERS = 2
ATTN_WINDOWS = ((128, 1), (512, 4), (2048, 16))
N_ATTN_GROUPS = 3
HEADS_PER_GROUP = 6
HEAD_DIM = 128
N_ATTN_HEADS = N_ATTN_GROUPS * HEADS_PER_GROUP
ATTN_WIDTH = N_ATTN_HEADS * HEAD_DIM
QKV_WIDTH = 3 * ATTN_WIDTH
N_FOURIER_GROUPS = 8
FOURIER_GROUP_DIM = D_MODEL // N_FOURIER_GROUPS
D_FF = 4 * D_MODEL
N_ATTN_LAYERS = (DEPTH + 1) // 2
N_FOURIER_LAYERS = DEPTH // 2
N_MOD = 6
RMS_EPS = 1e-6
MASK_VALUE = -1e30

kernel_name = "dilated_attn_fnet_adaln_hybrid_encoder"


def rmsnorm(x, gain):
    xf = x.astype(jnp.float32)
    xf = xf * lax.rsqrt(jnp.mean(xf * xf, axis=-1, keepdims=True) + RMS_EPS)
    return xf.astype(x.dtype) * gain


def ada_norm(x, gain, shift, scale):
    return rmsnorm(x, gain) * (1 + scale[:, None, :]) + shift[:, None, :]


def alibi_slopes():
    h = jnp.arange(1, N_ATTN_HEADS + 1, dtype=jnp.float32)
    return jnp.exp2(-8.0 * h / N_ATTN_HEADS)


def dilated_window_attention(q, k, v, dilation, half, slopes):
    B, S, H, E = q.shape
    blk = dilation * half
    s_pad = -(-S // blk) * blk
    pad = s_pad - S
    n_u = s_pad // dilation
    nb = n_u // half

    def to_sub(t):
        t = jnp.pad(t, ((0, 0), (0, pad), (0, 0), (0, 0)))
        t = t.reshape(B, n_u, dilation, H, E).transpose(0, 2, 1, 3, 4)
        return t.reshape(B, dilation, nb, half, H, E)

    def neighbours(t):
        z = jnp.zeros_like(t[:, :, :1])
        prev = jnp.concatenate([z, t[:, :, :-1]], axis=2)
        nxt = jnp.concatenate([t[:, :, 1:], z], axis=2)
        return jnp.concatenate([prev, t, nxt], axis=3)

    qb = to_sub(q)
    kn = neighbours(to_sub(k))
    vn = neighbours(to_sub(v))

    scores = jnp.einsum('brnqhe,brnkhe->brnhqk', qb, kn,
                        preferred_element_type=jnp.float32) * (E ** -0.5)
    qi = jnp.arange(half)
    kj = jnp.arange(3 * half) - half
    du = kj[None, :] - qi[:, None]
    u_key = jnp.arange(nb)[:, None] * half + kj[None, :]
    pos_key = u_key[None] * dilation + jnp.arange(dilation)[:, None, None]
    key_ok = (u_key[None] >= 0) & (pos_key < S)
    valid = key_ok[:, :, None, :] & (jnp.abs(du) <= half)[None, None]
    dist = (jnp.abs(du) * dilation).astype(jnp.float32)
    scores = scores - slopes[:, None, None] * dist[None]
    scores = jnp.where(valid[None, :, :, None], scores, MASK_VALUE)
    lse = jax.nn.logsumexp(scores, axis=-1)
    p = jnp.exp(scores - lse[..., None])
    out = jnp.einsum('brnhqk,brnkhe->brnqhe', p.astype(vn.dtype), vn)
    out = out.reshape(B, dilation, n_u, H, E).transpose(0, 2, 1, 3, 4).reshape(B, s_pad, H, E)[:, :S]
    lse = lse.transpose(0, 1, 2, 4, 3).reshape(B, dilation, n_u, H)
    lse = lse.transpose(0, 2, 1, 3).reshape(B, s_pad, H)[:, :S]
    return out, lse


def dilated_attention_mixer(h, w_qkv, w_o):
    B, S, _ = h.shape
    qkv = (h @ w_qkv).reshape(B, S, N_ATTN_GROUPS, 3, HEADS_PER_GROUP, HEAD_DIM)
    slopes = alibi_slopes()
    outs, lses = [], []
    for g, (window, dilation) in enumerate(ATTN_WINDOWS):
        half = window // (2 * dilation)
        o, l = dilated_window_attention(qkv[:, :, g, 0], qkv[:, :, g, 1], qkv[:, :, g, 2], dilation, half,
                                        slopes[g * HEADS_PER_GROUP:(g + 1) * HEADS_PER_GROUP])
        outs.append(o)
        lses.append(l)
    alpha = jax.nn.softmax(jnp.stack(lses, axis=0), axis=0)
    mixed = jnp.concatenate([(alpha[g][..., None] * outs[g]).astype(h.dtype)
                             for g in range(N_ATTN_GROUPS)], axis=2)
    return mixed.reshape(B, S, ATTN_WIDTH) @ w_o


def fourier_mixer(h, w_f, b_f):
    B, S, D = h.shape
    hg = h.astype(jnp.float32).reshape(B, S, N_FOURIER_GROUPS, FOURIER_GROUP_DIM)
    mixed = jnp.fft.fft2(hg, axes=(1, 3), norm="ortho").real
    return mixed.reshape(B, S, D).astype(h.dtype) @ w_f + b_f


def sqrelu_mlp(h, w1, b1, w2, b2):
    u = jax.nn.relu(h @ w1 + b1)
    return jnp.square(u) @ w2 + b2


def trunk(x, c, w_ada, b_ada, norm1_g, norm2_g, w_qkv, w_o, w_f, b_f, w1, b1, w2, b2, final_g):
    c_act = jax.nn.silu(c)
    for i in range(DEPTH):
        mod = c_act @ w_ada[i] + b_ada[i]
        sh1, sc1, g1, sh2, sc2, g2 = jnp.split(mod, N_MOD, axis=-1)
        h = ada_norm(x, norm1_g[i], sh1, sc1)
        if i % N_TOKEN_MIXERS == 0:
            y = dilated_attention_mixer(h, w_qkv[i // N_TOKEN_MIXERS], w_o[i // N_TOKEN_MIXERS])
        else:
            y = fourier_mixer(h, w_f[i // N_TOKEN_MIXERS], b_f[i // N_TOKEN_MIXERS])
        x = x + g1[:, None, :] * y
        h = ada_norm(x, norm2_g[i], sh2, sc2)
        x = x + g2[:, None, :] * sqrelu_mlp(h, w1[i], b1[i], w2[i], b2[i])
    return rmsnorm(x, final_g)


def setup_inputs(seed: int = 0) -> dict:
    key = jax.random.key(seed)
    ks = jax.random.split(key, 20)
    f32 = jnp.float32
    n = lambda k, shape, s: jax.random.normal(k, shape, f32) * s
    D = D_MODEL
    return {
        "x_prompt": n(ks[0], (BATCH, SEQ, D), 1.0),
        "x_sample": n(ks[1], (DEC_BATCH, DEC_SEQ, D), 1.0),
        "c_prompt": n(ks[2], (BATCH, D), 1.0),
        "c_sample": n(ks[3], (DEC_BATCH, D), 1.0),
        "w_ada": n(ks[4], (DEPTH, D, N_MOD * D), D ** -0.5),
        "b_ada": n(ks[5], (DEPTH, N_MOD * D), 0.02),
        "norm1_g": 1.0 + n(ks[6], (DEPTH, D), 0.02),
        "norm2_g": 1.0 + n(ks[7], (DEPTH, D), 0.02),
        "w_qkv": n(ks[8], (N_ATTN_LAYERS, D, QKV_WIDTH), D ** -0.5),
        "w_o": n(ks[9], (N_ATTN_LAYERS, ATTN_WIDTH, D), ATTN_WIDTH ** -0.5),
        "w_f": n(ks[10], (N_FOURIER_LAYERS, D, D), D ** -0.5),
        "b_f": n(ks[11], (N_FOURIER_LAYERS, D), 0.02),
        "w1": n(ks[12], (DEPTH, D, D_FF), D ** -0.5),
        "b1": n(ks[13], (DEPTH, D_FF), 0.02),
        "w2": n(ks[14], (DEPTH, D_FF, D), D_FF ** -0.5),
        "b2": n(ks[15], (DEPTH, D), 0.02),
        "final_g": 1.0 + n(ks[16], (D,), 0.02),
    }


def reference(x_prompt, x_sample, c_prompt, c_sample, w_ada, b_ada, norm1_g, norm2_g, w_qkv, w_o,
              w_f, b_f, w1, b1, w2, b2, final_g):
    y_prompt = trunk(x_prompt, c_prompt, w_ada, b_ada, norm1_g, norm2_g, w_qkv, w_o, w_f, b_f,
                     w1, b1, w2, b2, final_g)
    y_sample = trunk(x_sample, c_sample, w_ada, b_ada, norm1_g, norm2_g, w_qkv, w_o, w_f, b_f,
                     w1, b1, w2, b2, final_g)
    return (y_prompt, y_sample)
```

```python
import functools

import numpy as np
import jax
import jax.numpy as jnp
from jax import lax
from jax.experimental import pallas as pl
from jax.experimental.pallas import tpu as pltpu

F32 = jnp.float32
BF16 = jnp.bfloat16

N_MOD = 6
RMS_EPS = 1e-6
MASK_VALUE = -1e30
ATTN_WINDOWS = ((128, 1), (512, 4), (2048, 16))
HEADS_PER_GROUP = 6
HEAD_DIM = 128
N_ATTN_HEADS = len(ATTN_WINDOWS) * HEADS_PER_GROUP
GROUP_WIDTH = HEADS_PER_GROUP * HEAD_DIM
FOURIER_GROUP_DIM = 256
HALF = 64
FFT_N2 = 128
LANES = 128
VMEM_LIMIT = 56 * 1024 * 1024


def _params(n_axes, n_parallel):
    sem = ("parallel",) * n_parallel + ("arbitrary",) * (n_axes - n_parallel)
    return pltpu.CompilerParams(dimension_semantics=sem, vmem_limit_bytes=VMEM_LIMIT)


def _ada_norm_rows(x_ref, gain, shift, scale, h_ref, rows, chunk):
    def body(c, carry):
        r0 = pl.multiple_of(c * chunk, chunk)
        x = x_ref[pl.ds(r0, chunk), :]
        ms = jnp.mean(x * x, axis=-1, keepdims=True)
        xn = x * lax.rsqrt(ms + RMS_EPS)
        h = xn * gain * (1.0 + scale) + shift
        h_ref[pl.ds(r0, chunk), :] = h.astype(h_ref.dtype)
        return carry

    lax.fori_loop(0, rows // chunk, body, 0)


def _mod_kernel(c_ref, w_ref, b_ref, o_ref):
    c = c_ref[...]
    act = (c * jax.nn.sigmoid(c)).astype(BF16)
    w = w_ref[...].astype(BF16)
    o_ref[...] = jnp.dot(act, w, preferred_element_type=F32) + b_ref[...]


def _modulation(c_all, w_ada, b_ada):
    depth, d, n = w_ada.shape
    rows = c_all.shape[0]
    tn = 1024
    return pl.pallas_call(
        _mod_kernel,
        out_shape=jax.ShapeDtypeStruct((depth, rows, n), F32),
        grid=(depth, n // tn),
        in_specs=[
            pl.BlockSpec((rows, d), lambda l, j: (0, 0)),
            pl.BlockSpec((None, d, tn), lambda l, j: (l, 0, j)),
            pl.BlockSpec((None, 1, tn), lambda l, j: (l, 0, j)),
        ],
        out_specs=pl.BlockSpec((None, rows, tn), lambda l, j: (l, 0, j)),
        compiler_params=_params(2, 2),
        name="adaln_modulation",
    )(c_all, w_ada, b_ada.reshape(depth, 1, n))


def _mod_in(layer, which, d, n_grid):
    if n_grid == 2:
        imap = lambda b, i: (layer, b, which, 0, 0)
    else:
        imap = lambda b, i, j: (layer, b, which, 0, 0)
    return pl.BlockSpec((None, None, None, 1, d), imap)


def _row_in(layer, d, n_grid):
    if n_grid == 2:
        imap = lambda b, i: (layer, 0, 0)
    else:
        imap = lambda b, i, j: (layer, 0, 0)
    return pl.BlockSpec((None, 1, d), imap)


def _norm_proj_kernel(x_ref, g_ref, sh_ref, sc_ref, w_ref, o_ref, h_ref, *, tm, chunk):
    @pl.when(pl.program_id(2) == 0)
    def _():
        _ada_norm_rows(x_ref, g_ref[...], sh_ref[...], sc_ref[...], h_ref, tm, chunk)

    o_ref[...] = jnp.dot(h_ref[...], w_ref[...], preferred_element_type=F32).astype(o_ref.dtype)


def _norm_proj(x, mod5, gains, w, layer, w_layer, tm, tn):
    b, s, d = x.shape
    n = w.shape[-1]
    kern = functools.partial(_norm_proj_kernel, tm=tm, chunk=256)
    return pl.pallas_call(
        kern,
        out_shape=jax.ShapeDtypeStruct((b, s, n), BF16),
        grid=(b, s // tm, n // tn),
        in_specs=[
            pl.BlockSpec((None, tm, d), lambda bb, i, j: (bb, i, 0)),
            _row_in(layer, d, 3),
            _mod_in(layer, 0, d, 3),
            _mod_in(layer, 1, d, 3),
            pl.BlockSpec((None, d, tn), lambda bb, i, j: (w_layer, 0, j)),
        ],
        out_specs=pl.BlockSpec((None, tm, tn), lambda bb, i, j: (bb, i, j)),
        scratch_shapes=[pltpu.VMEM((tm, d), BF16)],
        compiler_params=_params(3, 2),
        name="norm_qkv_proj",
    )(x, gains, mod5, mod5, w)


def _attn_kernel(q_ref, kp_ref, km_ref, kn_ref, vp_ref, vm_ref, vn_ref, o_ref, lse_ref,
                 kc_ref, vc_ref, *, tq, n_u, dilation, slopes):
    i = pl.program_id(2)
    kc_ref[0:HALF, :] = kp_ref[...]
    kc_ref[HALF:HALF + tq, :] = km_ref[...]
    kc_ref[HALF + tq:, :] = kn_ref[...]
    vc_ref[0:HALF, :] = vp_ref[...]
    vc_ref[HALF:HALF + tq, :] = vm_ref[...]
    vc_ref[HALF + tq:, :] = vn_ref[...]

    sub = 2 * HALF
    span = sub + 2 * HALF
    row = lax.broadcasted_iota(jnp.int32, (sub, span), 0)
    col = lax.broadcasted_iota(jnp.int32, (sub, span), 1)
    du = col - HALF - row
    adu = jnp.abs(du)
    band = adu <= HALF
    dist = (adu * dilation).astype(F32)
    lane = lax.broadcasted_iota(jnp.int32, (sub, LANES), 1)
    scale = HEAD_DIM ** -0.5

    for sb in range(tq // sub):
        u_key = i * tq + (sb * sub - HALF) + col
        valid = band & (u_key >= 0) & (u_key < n_u)
        lse_tile = jnp.zeros((sub, LANES), F32)
        for h in range(HEADS_PER_GROUP):
            cs = slice(h * HEAD_DIM, (h + 1) * HEAD_DIM)
            q = q_ref[sb * sub:(sb + 1) * sub, cs]
            k = kc_ref[sb * sub:sb * sub + span, cs]
            v = vc_ref[sb * sub:sb * sub + span, cs]
            s = lax.dot_general(q, k, (((1,), (1,)), ((), ())), preferred_element_type=F32) * scale
            s = s - slopes[h] * dist
            s = jnp.where(valid, s, MASK_VALUE)
            m = jnp.max(s, axis=-1, keepdims=True)
            p = jnp.exp(s - m)
            l = jnp.sum(p, axis=-1, keepdims=True)
            o = jnp.dot(p.astype(BF16), v, preferred_element_type=F32) * (1.0 / l)
            o_ref[sb * sub:(sb + 1) * sub, cs] = o.astype(o_ref.dtype)
            lse_tile = jnp.where(lane == h, m + jnp.log(l), lse_tile)
        lse_ref[sb * sub:(sb + 1) * sub, :] = lse_tile


def _group_attention(qkv, group):
    b, s, width = qkv.shape
    _, dilation = ATTN_WINDOWS[group]
    n_u = s // dilation
    tq = min(256, n_u)
    n_slabs = width // GROUP_WIDTH
    view = qkv.reshape(b, n_u, dilation * width)
    halo_per_blk = tq // HALF
    n_halo = n_u // HALF
    slopes = tuple(float(np.exp2(np.float32(-8.0 * (group * HEADS_PER_GROUP + h + 1) / N_ATTN_HEADS)))
                   for h in range(HEADS_PER_GROUP))

    def col(t):
        return lambda bb, r, i: (bb, i, r * n_slabs + group * 3 + t)

    def prev(t):
        return lambda bb, r, i: (bb, jnp.maximum(i * halo_per_blk - 1, 0), r * n_slabs + group * 3 + t)

    def nxt(t):
        return lambda bb, r, i: (bb, jnp.minimum((i + 1) * halo_per_blk, n_halo - 1),
                                 r * n_slabs + group * 3 + t)

    main = lambda t: pl.BlockSpec((None, tq, GROUP_WIDTH), col(t))
    halo_p = lambda t: pl.BlockSpec((None, HALF, GROUP_WIDTH), prev(t))
    halo_n = lambda t: pl.BlockSpec((None, HALF, GROUP_WIDTH), nxt(t))
    kern = functools.partial(_attn_kernel, tq=tq, n_u=n_u, dilation=dilation, slopes=slopes)
    out, lse = pl.pallas_call(
        kern,
        out_shape=(jax.ShapeDtypeStruct((b, n_u, dilation * GROUP_WIDTH), BF16),
                   jax.ShapeDtypeStruct((b, n_u, dilation * LANES), F32)),
        grid=(b, dilation, n_u // tq),
        in_specs=[main(0), halo_p(1), main(1), halo_n(1), halo_p(2), main(2), halo_n(2)],
        out_specs=(pl.BlockSpec((None, tq, GROUP_WIDTH), lambda bb, r, i: (bb, i, r)),
                   pl.BlockSpec((None, tq, LANES), lambda bb, r, i: (bb, i, r))),
        scratch_shapes=[pltpu.VMEM((tq + 2 * HALF, GROUP_WIDTH), BF16),
                        pltpu.VMEM((tq + 2 * HALF, GROUP_WIDTH), BF16)],
        compiler_params=_params(3, 3),
        name=f"dilated_attention_g{group}",
    )(view, view, view, view, view, view, view)
    return out.reshape(b, s, GROUP_WIDTH), lse.reshape(b, s, LANES)


def _merge_proj_kernel(o0_ref, o1_ref, o2_ref, l0_ref, l1_ref, l2_ref, x_ref, gate_ref, w_ref,
                       out_ref, mix_ref):
    l0, l1, l2 = l0_ref[...], l1_ref[...], l2_ref[...]
    m = jnp.maximum(jnp.maximum(l0, l1), l2)
    e0, e1, e2 = jnp.exp(l0 - m), jnp.exp(l1 - m), jnp.exp(l2 - m)
    inv = 1.0 / (e0 + e1 + e2)
    for g, (o_ref, e) in enumerate(((o0_ref, e0), (o1_ref, e1), (o2_ref, e2))):
        alpha = e * inv
        for h in range(HEADS_PER_GROUP):
            src = slice(h * HEAD_DIM, (h + 1) * HEAD_DIM)
            dst = slice((g * HEADS_PER_GROUP + h) * HEAD_DIM, (g * HEADS_PER_GROUP + h + 1) * HEAD_DIM)
            mix_ref[:, dst] = (alpha[:, h:h + 1] * o_ref[:, src].astype(F32)).astype(mix_ref.dtype)
    y = jnp.dot(mix_ref[...], w_ref[...], preferred_element_type=F32)
    out_ref[...] = x_ref[...] + gate_ref[...] * y


def _merge_proj(outs, lses, x, mod5, w_o, layer, w_layer, tm):
    b, s, d = x.shape
    width = w_o.shape[1]
    tok = lambda c: pl.BlockSpec((None, tm, c), lambda bb, i: (bb, i, 0))
    return pl.pallas_call(
        _merge_proj_kernel,
        out_shape=jax.ShapeDtypeStruct((b, s, d), F32),
        grid=(b, s // tm),
        in_specs=[tok(GROUP_WIDTH)] * 3 + [tok(LANES)] * 3 + [
            tok(d),
            _mod_in(layer, 2, d, 2),
            pl.BlockSpec((None, width, d), lambda bb, i: (w_layer, 0, 0)),
        ],
        out_specs=tok(d),
        scratch_shapes=[pltpu.VMEM((tm, width), BF16)],
        compiler_params=_params(2, 2),
        name="attn_merge_out_proj",
    )(*outs, *lses, x, mod5, w_o)


def _dft_tables(s):
    n2 = FFT_N2
    n1 = s // n2
    c = FOURIER_GROUP_DIM
    ang_c = 2.0 * np.pi * np.outer(np.arange(c), np.arange(c)) / c
    cs_chan = np.concatenate([np.cos(ang_c), np.sin(ang_c)], axis=1) / np.sqrt(c)
    a1 = 2.0 * np.pi * np.outer(np.arange(n1), np.arange(n1)) / n1
    c1, s1 = np.cos(a1), np.sin(a1)
    w1 = np.block([[c1, s1], [-s1, c1]])
    th = 2.0 * np.pi * np.outer(np.arange(n2), np.arange(n1)) / s
    a2 = 2.0 * np.pi * np.outer(np.arange(n2), np.arange(n2)) / n2
    cs2 = np.concatenate([np.cos(a2), np.sin(a2)], axis=1)
    return (jnp.asarray(cs_chan, BF16), jnp.asarray(w1, BF16),
            jnp.asarray(np.cos(th)[:, :, None], F32), jnp.asarray(np.sin(th)[:, :, None], F32),
            jnp.asarray(cs2, BF16))


def _chan_dft_kernel(x_ref, g_ref, sh_ref, sc_ref, cs_ref, zr_ref, zi_ref, h_ref, *, tm, chunk):
    _ada_norm_rows(x_ref, g_ref[...], sh_ref[...], sc_ref[...], h_ref, tm, chunk)
    c = FOURIER_GROUP_DIM
    for g in range(x_ref.shape[-1] // c):
        cols = slice(g * c, (g + 1) * c)
        r = jnp.dot(h_ref[:, cols], cs_ref[...], preferred_element_type=F32)
        zr_ref[:, cols] = r[:, :c].astype(zr_ref.dtype)
        zi_ref[:, cols] = (-r[:, c:]).astype(zi_ref.dtype)


def _chan_dft(x, mod5, gains, cs_chan, layer, tm):
    b, s, d = x.shape
    tok = pl.BlockSpec((None, tm, d), lambda bb, i: (bb, i, 0))
    kern = functools.partial(_chan_dft_kernel, tm=tm, chunk=256)
    return pl.pallas_call(
        kern,
        out_shape=(jax.ShapeDtypeStruct((b, s, d), BF16),) * 2,
        grid=(b, s // tm),
        in_specs=[tok, _row_in(layer, d, 2), _mod_in(layer, 0, d, 2), _mod_in(layer, 1, d, 2),
                  pl.BlockSpec(cs_chan.shape, lambda bb, i: (0, 0))],
        out_specs=(tok, tok),
        scratch_shapes=[pltpu.VMEM((tm, d), BF16)],
        compiler_params=_params(2, 2),
        name="fourier_channel_dft",
    )(x, gains, mod5, mod5, cs_chan)


def _pos_dft1_kernel(zr_ref, zi_ref, w_ref, tc_ref, ts_ref, ur_ref, ui_ref, *, n1, ts2, d):
    for jj in range(ts2):
        cols = slice(jj * d, (jj + 1) * d)
        z = jnp.concatenate([zr_ref[:, cols], zi_ref[:, cols]], axis=0)
        t = jnp.dot(w_ref[...], z, preferred_element_type=F32)
        tr, ti = t[:n1], t[n1:]
        c, sn = tc_ref[jj], ts_ref[jj]
        ur_ref[:, cols] = (tr * c + ti * sn).astype(ur_ref.dtype)
        ui_ref[:, cols] = (ti * c - tr * sn).astype(ui_ref.dtype)


def _pos_dft1(zr, zi, w1, tw_c, tw_s):
    b, s, d = zr.shape
    n2 = FFT_N2
    n1 = s // n2
    ts2 = max(1, 128 // n1)
    view = lambda a: a.reshape(b, n1, n2 * d)
    blk = pl.BlockSpec((None, n1, ts2 * d), lambda bb, j: (bb, 0, j))
    tw = pl.BlockSpec((ts2, n1, 1), lambda bb, j: (j, 0, 0))
    kern = functools.partial(_pos_dft1_kernel, n1=n1, ts2=ts2, d=d)
    ur, ui = pl.pallas_call(
        kern,
        out_shape=(jax.ShapeDtypeStruct((b, n1, n2 * d), BF16),) * 2,
        grid=(b, n2 // ts2),
        in_specs=[blk, blk, pl.BlockSpec(w1.shape, lambda bb, j: (0, 0)), tw, tw],
        out_specs=(blk, blk),
        compiler_params=_params(2, 2),
        name="fourier_pos_dft_stage1",
    )(view(zr), view(zi), w1, tw_c, tw_s)
    return ur, ui


def _pos_dft2_proj_kernel(ur_ref, ui_ref, cs2_ref, w_ref, b_ref, x_ref, gate_ref, out_ref, *, inv_norm):
    u = jnp.concatenate([ur_ref[...], ui_ref[...]], axis=0)
    y = jnp.dot(cs2_ref[...], u, preferred_element_type=F32) * inv_norm
    proj = jnp.dot(y.astype(BF16), w_ref[...], preferred_element_type=F32) + b_ref[...]
    out_ref[...] = x_ref[...] + gate_ref[...] * proj


def _pos_dft2_proj(ur, ui, cs2, x, mod5, w_f, b_f, layer, w_layer):
    b, s, d = x.shape
    n2 = FFT_N2
    n1 = s // n2
    u_view = lambda a: a.reshape(b, n1 * n2, d)
    u_blk = pl.BlockSpec((None, n2, d), lambda bb, k1: (bb, k1, 0))
    x_blk = pl.BlockSpec((None, n2, d), lambda bb, k1: (bb, 0, k1))
    kern = functools.partial(_pos_dft2_proj_kernel, inv_norm=float(1.0 / np.sqrt(s)))
    out = pl.pallas_call(
        kern,
        out_shape=jax.ShapeDtypeStruct((b, n2, n1 * d), F32),
        grid=(b, n1),
        in_specs=[u_blk, u_blk, pl.BlockSpec(cs2.shape, lambda bb, k1: (0, 0)),
                  pl.BlockSpec((None, d, d), lambda bb, k1: (w_layer, 0, 0)),
                  _row_in(w_layer, d, 2), x_blk, _mod_in(layer, 2, d, 2)],
        out_specs=x_blk,
        compiler_params=_params(2, 2),
        name="fourier_pos_dft_stage2_proj",
    )(u_view(ur), u_view(ui), cs2, w_f, b_f, x.reshape(b, n2, n1 * d), mod5)
    return out.reshape(b, s, d)


def _mlp_kernel(x_ref, g_ref, sh_ref, sc_ref, gate_ref, w1_ref, b1_ref, w2_ref, b2_ref, fg_ref,
                out_ref, h_ref, acc_ref, *, tm, chunk, final_norm):
    j = pl.program_id(2)

    @pl.when(j == 0)
    def _():
        _ada_norm_rows(x_ref, g_ref[...], sh_ref[...], sc_ref[...], h_ref, tm, chunk)

    u = jnp.dot(h_ref[...], w1_ref[...], preferred_element_type=F32) + b1_ref[...]
    u = jnp.maximum(u, 0.0)
    p = jnp.dot((u * u).astype(BF16), w2_ref[...], preferred_element_type=F32)

    @pl.when(j == 0)
    def _():
        acc_ref[...] = p

    @pl.when(j > 0)
    def _():
        acc_ref[...] += p

    @pl.when(j == pl.num_programs(2) - 1)
    def _():
        y = x_ref[...] + gate_ref[...] * (acc_ref[...] + b2_ref[...])
        if final_norm:
            ms = jnp.mean(y * y, axis=-1, keepdims=True)
            y = y * lax.rsqrt(ms + RMS_EPS) * fg_ref[...]
        out_ref[...] = y


def _mlp(x, mod5, gains, w1, b1, w2, b2, final_g, layer, tm, tf, final_norm):
    b, s, d = x.shape
    dff = w1.shape[-1]
    kern = functools.partial(_mlp_kernel, tm=tm, chunk=256, final_norm=final_norm)
    tok = pl.BlockSpec((None, tm, d), lambda bb, i, j: (bb, i, 0))
    return pl.pallas_call(
        kern,
        out_shape=jax.ShapeDtypeStruct((b, s, d), F32),
        grid=(b, s // tm, dff // tf),
        in_specs=[
            tok,
            _row_in(layer, d, 3),
            _mod_in(layer, 3, d, 3),
            _mod_in(layer, 4, d, 3),
            _mod_in(layer, 5, d, 3),
            pl.BlockSpec((None, d, tf), lambda bb, i, j: (layer, 0, j)),
            pl.BlockSpec((None, 1, tf), lambda bb, i, j: (layer, 0, j)),
            pl.BlockSpec((None, tf, d), lambda bb, i, j: (layer, j, 0)),
            _row_in(layer, d, 3),
            pl.BlockSpec((1, d), lambda bb, i, j: (0, 0)),
        ],
        out_specs=tok,
        scratch_shapes=[pltpu.VMEM((tm, d), BF16), pltpu.VMEM((tm, d), F32)],
        compiler_params=_params(3, 2),
        name="sqrelu_mlp",
    )(x, gains, mod5, mod5, mod5, w1, b1, w2, b2, final_g)


def _trunk(x, mod5, p):
    depth = p["w1"].shape[0]
    s = x.shape[1]
    cs_chan, w1c, tw_c, tw_s, cs2 = _dft_tables(s)
    for i in range(depth):
        sub = i // 2
        if i % 2 == 0:
            qkv = _norm_proj(x, mod5, p["norm1_g"], p["w_qkv"], i, sub, tm=1024, tn=1152)
            outs, lses = zip(*[_group_attention(qkv, g) for g in range(len(ATTN_WINDOWS))])
            x = _merge_proj(outs, lses, x, mod5, p["w_o"], i, sub, tm=512)
        else:
            zr, zi = _chan_dft(x, mod5, p["norm1_g"], cs_chan, i, tm=512)
            ur, ui = _pos_dft1(zr, zi, w1c, tw_c, tw_s)
            x = _pos_dft2_proj(ur, ui, cs2, x, mod5, p["w_f"], p["b_f"], i, sub)
        x = _mlp(x, mod5, p["norm2_g"], p["w1"], p["b1"], p["w2"], p["b2"], p["final_g"], i,
                 tm=512, tf=1024, final_norm=(i == depth - 1))
    return x


def kernel(x_prompt, x_sample, c_prompt, c_sample, w_ada, b_ada, norm1_g, norm2_g, w_qkv, w_o, w_f, b_f,
           w1, b1, w2, b2, final_g):
    depth, d, _ = w_ada.shape
    n_p, n_s = c_prompt.shape[0], c_sample.shape[0]
    rows = -(-(n_p + n_s) // 16) * 16
    c_all = jnp.concatenate([c_prompt, c_sample, jnp.zeros((rows - n_p - n_s, d), F32)], axis=0)
    mod = _modulation(c_all, w_ada, b_ada)
    mod_p = mod[:, :n_p].reshape(depth, n_p, N_MOD, 1, d)
    mod_s = mod[:, n_p:n_p + n_s].reshape(depth, n_s, N_MOD, 1, d)

    row3 = lambda a: a.reshape(a.shape[0], 1, a.shape[-1])
    params = {
        "norm1_g": row3(norm1_g), "norm2_g": row3(norm2_g),
        "w_qkv": w_qkv.astype(BF16), "w_o": w_o.astype(BF16),
        "w_f": w_f.astype(BF16), "b_f": row3(b_f),
        "w1": w1.astype(BF16), "b1": row3(b1), "w2": w2.astype(BF16), "b2": row3(b2),
        "final_g": final_g.reshape(1, d),
    }
    return (_trunk(x_prompt, mod_p, params), _trunk(x_sample, mod_s, params))
```

```python
import functools

import numpy as np
import jax
import jax.numpy as jnp
from jax import lax
from jax.experimental import pallas as pl
from jax.experimental.pallas import tpu as pltpu

F32 = jnp.float32
BF16 = jnp.bfloat16

N_MOD = 6
RMS_EPS = 1e-6
MASK_VALUE = -1e30
ATTN_WINDOWS = ((128, 1), (512, 4), (2048, 16))
HEADS_PER_GROUP = 6
HEAD_DIM = 128
N_ATTN_HEADS = len(ATTN_WINDOWS) * HEADS_PER_GROUP
GROUP_WIDTH = HEADS_PER_GROUP * HEAD_DIM
FOURIER_GROUP_DIM = 256
HALF = 64
FFT_N2 = 128
LANES = 128
ROW_TILE = 16
PERM_BLOCK = 256
VMEM_LIMIT = 56 * 1024 * 1024


def _params(n_axes, n_parallel):
    sem = ("parallel",) * n_parallel + ("arbitrary",) * (n_axes - n_parallel)
    return pltpu.CompilerParams(dimension_semantics=sem, vmem_limit_bytes=VMEM_LIMIT)


def _const_spec(shape, n_grid):
    zeros = (0,) * len(shape)
    imap = (lambda a, b: zeros) if n_grid == 2 else (lambda a, b, c: zeros)
    return pl.BlockSpec(shape, imap, pipeline_mode=pl.Buffered(1))


def _ada_norm_rows(x_ref, gain, shift, scale, h_ref, rows, chunk):
    def body(c, carry):
        r0 = pl.multiple_of(c * chunk, chunk)
        x = x_ref[pl.ds(r0, chunk), :]
        ms = jnp.mean(x * x, axis=-1, keepdims=True)
        xn = x * lax.rsqrt(ms + RMS_EPS)
        h = xn * gain * (1.0 + scale) + shift
        h_ref[pl.ds(r0, chunk), :] = h.astype(h_ref.dtype)
        return carry

    lax.fori_loop(0, rows // chunk, body, 0)


def _mod_kernel(c_ref, w_ref, b_ref, o_ref):
    c = c_ref[...]
    act = (c * jax.nn.sigmoid(c)).astype(BF16)
    w = w_ref[...].astype(BF16)
    o_ref[...] = jnp.dot(act, w, preferred_element_type=F32) + b_ref[...]


def _modulation(c_all, w_ada, b_ada):
    depth, d, n = w_ada.shape
    rows = c_all.shape[0]
    tn = 1024
    return pl.pallas_call(
        _mod_kernel,
        out_shape=jax.ShapeDtypeStruct((depth, rows, n), F32),
        grid=(depth, n // tn),
        in_specs=[
            pl.BlockSpec((rows, d), lambda l, j: (0, 0)),
            pl.BlockSpec((None, d, tn), lambda l, j: (l, 0, j)),
            pl.BlockSpec((None, 1, tn), lambda l, j: (l, 0, j)),
        ],
        out_specs=pl.BlockSpec((None, rows, tn), lambda l, j: (l, 0, j)),
        compiler_params=_params(2, 2),
        name="adaln_modulation",
    )(c_all, w_ada, b_ada.reshape(depth, 1, n))


def _mod_in(layer, which, d, n_grid):
    if n_grid == 2:
        imap = lambda b, i: (layer, b, which, 0, 0)
    else:
        imap = lambda b, i, j: (layer, b, which, 0, 0)
    return pl.BlockSpec((None, None, None, 1, d), imap)


def _row_in(layer, d, n_grid):
    if n_grid == 2:
        imap = lambda b, i: (layer, 0, 0)
    else:
        imap = lambda b, i, j: (layer, 0, 0)
    return pl.BlockSpec((None, 1, d), imap)


def _residue_perms():
    mats = []
    for _, dil in ATTN_WINDOWS[1:]:
        ub = PERM_BLOCK // dil
        p = np.zeros((PERM_BLOCK, PERM_BLOCK), np.float32)
        nat = np.arange(PERM_BLOCK)
        p[(nat % dil) * ub + nat // dil, nat] = 1.0
        mats.append(p)
    return np.stack(mats)


def _norm_proj_kernel(x_ref, g_ref, sh_ref, sc_ref, w_ref, p_ref, o_ref, h_ref, *, tm, chunk, steps_per_group):
    j = pl.program_id(2)

    @pl.when(j == 0)
    def _():
        _ada_norm_rows(x_ref, g_ref[...], sh_ref[...], sc_ref[...], h_ref, tm, chunk)

    y = jnp.dot(h_ref[...], w_ref[...], preferred_element_type=F32).astype(o_ref.dtype)

    @pl.when(j < steps_per_group)
    def _():
        o_ref[...] = y

    @pl.when(j >= steps_per_group)
    def _():
        for blk in range(tm // PERM_BLOCK):
            rows = slice(blk * PERM_BLOCK, (blk + 1) * PERM_BLOCK)
            o_ref[rows, :] = jnp.dot(p_ref[...], y[rows, :], preferred_element_type=F32).astype(o_ref.dtype)


def _norm_proj(x, mod5, gains, w, perms, layer, w_layer, tm, tn):
    b, s, d = x.shape
    n = w.shape[-1]
    steps_per_group = (n // len(ATTN_WINDOWS)) // tn
    kern = functools.partial(_norm_proj_kernel, tm=tm, chunk=256, steps_per_group=steps_per_group)
    return pl.pallas_call(
        kern,
        out_shape=jax.ShapeDtypeStruct((b, s, n), BF16),
        grid=(b, s // tm, n // tn),
        in_specs=[
            pl.BlockSpec((None, tm, d), lambda bb, i, j: (bb, i, 0)),
            _row_in(layer, d, 3),
            _mod_in(layer, 0, d, 3),
            _mod_in(layer, 1, d, 3),
            pl.BlockSpec((None, d, tn), lambda bb, i, j: (w_layer, 0, j)),
            pl.BlockSpec((None, PERM_BLOCK, PERM_BLOCK),
                         lambda bb, i, j: (jnp.maximum(j // steps_per_group, 1) - 1, 0, 0)),
        ],
        out_specs=pl.BlockSpec((None, tm, tn), lambda bb, i, j: (bb, i, j)),
        scratch_shapes=[pltpu.VMEM((tm, d), BF16)],
        compiler_params=_params(3, 2),
        name="norm_qkv_proj",
    )(x, gains, mod5, mod5, w, perms)


def _attn_kernel(q_ref, kp_ref, km_ref, kn_ref, vp_ref, vm_ref, vn_ref, pt_ref, o_ref, lse_ref,
                 kc_ref, vc_ref, op_ref, lp_ref, *, nblk, ub, n_u, dilation, slopes):
    i = pl.program_id(1)
    r = pl.program_id(2)
    tq = nblk * ub
    width = q_ref.shape[-1]

    def gather(dst, prev, main, nxt):
        dst[0:HALF, :] = prev[...].reshape(HALF, width)
        dst[HALF:HALF + tq, :] = main[...].reshape(tq, width)
        dst[HALF + tq:, :] = nxt[...].reshape(HALF, width)

    gather(kc_ref, kp_ref, km_ref, kn_ref)
    gather(vc_ref, vp_ref, vm_ref, vn_ref)

    sub = 2 * HALF
    span = sub + 2 * HALF
    row = lax.broadcasted_iota(jnp.int32, (sub, span), 0)
    col = lax.broadcasted_iota(jnp.int32, (sub, span), 1)
    adu = jnp.abs(col - HALF - row)
    band = adu <= HALF
    dist = (adu * dilation).astype(F32)
    lane = lax.broadcasted_iota(jnp.int32, (sub, LANES), 1)
    scale = HEAD_DIM ** -0.5
    blocks_per_sub = sub // ub

    for sb in range(tq // sub):
        u_key = i * tq + (sb * sub - HALF) + col
        valid = band & (u_key >= 0) & (u_key < n_u)
        lse_tile = jnp.zeros((sub, LANES), F32)
        rows = slice(sb * sub, (sb + 1) * sub)
        for h in range(HEADS_PER_GROUP):
            cs = slice(h * HEAD_DIM, (h + 1) * HEAD_DIM)
            q = q_ref[sb * blocks_per_sub:(sb + 1) * blocks_per_sub, :, cs].reshape(sub, HEAD_DIM)
            k = kc_ref[sb * sub:sb * sub + span, cs]
            v = vc_ref[sb * sub:sb * sub + span, cs]
            s = lax.dot_general(q, k, (((1,), (1,)), ((), ())), preferred_element_type=F32) * scale
            s = s - slopes[h] * dist
            s = jnp.where(valid, s, MASK_VALUE)
            m = jnp.max(s, axis=-1, keepdims=True)
            p = jnp.exp(s - m)
            l = jnp.sum(p, axis=-1, keepdims=True)
            o = jnp.dot(p.astype(BF16), v, preferred_element_type=F32) * (1.0 / l)
            if dilation == 1:
                o_ref[rows, cs] = o.astype(o_ref.dtype)
            else:
                op_ref[r, rows, cs] = o.astype(op_ref.dtype)
            lse_tile = jnp.where(lane == h, m + jnp.log(l), lse_tile)
        if dilation == 1:
            lse_ref[rows, :] = lse_tile
        else:
            lp_ref[r, rows, :] = lse_tile

    if dilation > 1:
        @pl.when(r == dilation - 1)
        def _():
            pt = pt_ref[...]
            for blk in range(nblk):
                urows = slice(blk * ub, (blk + 1) * ub)
                nat = slice(blk * PERM_BLOCK, (blk + 1) * PERM_BLOCK)
                ob = jnp.concatenate([op_ref[rr, urows, :] for rr in range(dilation)], axis=0)
                o_ref[nat, :] = jnp.dot(pt, ob, preferred_element_type=F32).astype(o_ref.dtype)
                lb = jnp.concatenate([lp_ref[rr, urows, :] for rr in range(dilation)], axis=0)
                hi = lb.astype(BF16)
                rest = lb - hi.astype(F32)
                mid = rest.astype(BF16)
                lo = (rest - mid.astype(F32)).astype(BF16)
                lse_ref[nat, :] = (jnp.dot(pt, hi, preferred_element_type=F32)
                                   + jnp.dot(pt, mid, preferred_element_type=F32)
                                   + jnp.dot(pt, lo, preferred_element_type=F32))


def _group_attention(qkv, group, perms_t):
    b, s, width = qkv.shape
    _, dilation = ATTN_WINDOWS[group]
    n_u = s // dilation
    ub = HALF if dilation == 1 else PERM_BLOCK // dilation
    rb = ub * dilation
    tq = min({1: 512, 4: 256, 16: 128}[dilation], n_u)
    nblk = tq // ub
    hb = HALF // ub
    n_slabs = width // GROUP_WIDTH
    view = qkv.reshape(b, s // rb, dilation, ub, width)
    n_halo = (s // rb) // hb
    slopes = tuple(float(np.exp2(np.float32(-8.0 * (group * HEADS_PER_GROUP + h + 1) / N_ATTN_HEADS)))
                   for h in range(HEADS_PER_GROUP))

    def main(t):
        return pl.BlockSpec((None, nblk, None, ub, GROUP_WIDTH),
                            lambda bb, i, r: (bb, i, r, 0, group * 3 + t))

    def halo_p(t):
        return pl.BlockSpec((None, hb, None, ub, GROUP_WIDTH),
                            lambda bb, i, r: (bb, jnp.maximum(i * (nblk // hb) - 1, 0), r, 0, group * 3 + t))

    def halo_n(t):
        return pl.BlockSpec((None, hb, None, ub, GROUP_WIDTH),
                            lambda bb, i, r: (bb, jnp.minimum((i + 1) * (nblk // hb), n_halo - 1), r, 0,
                                              group * 3 + t))

    assert nblk % hb == 0 and width == n_slabs * GROUP_WIDTH
    pt = perms_t[max(group - 1, 0)]
    kern = functools.partial(_attn_kernel, nblk=nblk, ub=ub, n_u=n_u, dilation=dilation, slopes=slopes)
    scratch = [pltpu.VMEM((tq + 2 * HALF, GROUP_WIDTH), BF16), pltpu.VMEM((tq + 2 * HALF, GROUP_WIDTH), BF16),
               pltpu.VMEM((dilation, tq, GROUP_WIDTH), BF16), pltpu.VMEM((dilation, tq, LANES), F32)]
    rows = nblk * rb
    return pl.pallas_call(
        kern,
        out_shape=(jax.ShapeDtypeStruct((b, s, GROUP_WIDTH), BF16),
                   jax.ShapeDtypeStruct((b, s, LANES), F32)),
        grid=(b, s // rows, dilation),
        in_specs=[main(0), halo_p(1), main(1), halo_n(1), halo_p(2), main(2), halo_n(2),
                  pl.BlockSpec(pt.shape, lambda bb, i, r: (0, 0))],
        out_specs=(pl.BlockSpec((None, rows, GROUP_WIDTH), lambda bb, i, r: (bb, i, 0)),
                   pl.BlockSpec((None, rows, LANES), lambda bb, i, r: (bb, i, 0))),
        scratch_shapes=scratch,
        compiler_params=_params(3, 2),
        name=f"dilated_attention_g{group}",
    )(view, view, view, view, view, view, view, pt)


def _merge_proj_kernel(o0_ref, o1_ref, o2_ref, l0_ref, l1_ref, l2_ref, x_ref, gate_ref, w_ref,
                       out_ref, mix_ref):
    l0, l1, l2 = l0_ref[...], l1_ref[...], l2_ref[...]
    m = jnp.maximum(jnp.maximum(l0, l1), l2)
    e0, e1, e2 = jnp.exp(l0 - m), jnp.exp(l1 - m), jnp.exp(l2 - m)
    inv = 1.0 / (e0 + e1 + e2)
    for g, (o_ref, e) in enumerate(((o0_ref, e0), (o1_ref, e1), (o2_ref, e2))):
        alpha = e * inv
        for h in range(HEADS_PER_GROUP):
            src = slice(h * HEAD_DIM, (h + 1) * HEAD_DIM)
            dst = slice((g * HEADS_PER_GROUP + h) * HEAD_DIM, (g * HEADS_PER_GROUP + h + 1) * HEAD_DIM)
            mix_ref[:, dst] = (alpha[:, h:h + 1] * o_ref[:, src].astype(F32)).astype(mix_ref.dtype)
    y = jnp.dot(mix_ref[...], w_ref[...], preferred_element_type=F32)
    out_ref[...] = x_ref[...] + gate_ref[...] * y


def _merge_proj(outs, lses, x, mod5, w_o, layer, w_layer, tm):
    b, s, d = x.shape
    width = w_o.shape[1]
    tok = lambda c: pl.BlockSpec((None, tm, c), lambda bb, i: (bb, i, 0))
    return pl.pallas_call(
        _merge_proj_kernel,
        out_shape=jax.ShapeDtypeStruct((b, s, d), F32),
        grid=(b, s // tm),
        in_specs=[tok(GROUP_WIDTH)] * 3 + [tok(LANES)] * 3 + [
            tok(d),
            _mod_in(layer, 2, d, 2),
            pl.BlockSpec((None, width, d), lambda bb, i: (w_layer, 0, 0)),
        ],
        out_specs=tok(d),
        scratch_shapes=[pltpu.VMEM((tm, width), BF16)],
        compiler_params=_params(2, 2),
        name="attn_merge_out_proj",
    )(*outs, *lses, x, mod5, w_o)


K1_GROUP = 8


def _dft_tables(s):
    n2 = FFT_N2
    n1 = s // n2
    c = FOURIER_GROUP_DIM
    ang_c = 2.0 * np.pi * np.outer(np.arange(c), np.arange(c)) / c
    cs_chan = np.concatenate([np.cos(ang_c), np.sin(ang_c)], axis=1) / np.sqrt(c)
    a1 = 2.0 * np.pi * np.outer(np.arange(n1), np.arange(n1)) / n1
    c1, s1 = np.cos(a1), np.sin(a1)
    w1 = np.kron(np.block([[c1, s1], [-s1, c1]]), np.eye(ROW_TILE))
    s2 = (np.arange(n2 // ROW_TILE)[:, None, None] * ROW_TILE + np.arange(ROW_TILE)[None, None, :])
    th = 2.0 * np.pi * np.arange(n1)[None, :, None] * s2 / s
    th = th.reshape(n2 // ROW_TILE, n1 * ROW_TILE, 1)
    a2 = 2.0 * np.pi * np.outer(np.arange(n2), np.arange(n2)) / n2
    eye = np.eye(K1_GROUP)
    m2 = np.concatenate([np.einsum("ks,ab->kabs", f(a2), eye).reshape(n2 * K1_GROUP, K1_GROUP * n2)
                         for f in (np.cos, np.sin)], axis=1)
    return (jnp.asarray(cs_chan, BF16), jnp.asarray(w1, BF16),
            jnp.asarray(np.cos(th), F32), jnp.asarray(np.sin(th), F32), jnp.asarray(m2, BF16))


def _chan_dft_kernel(x_ref, g_ref, sh_ref, sc_ref, cs_ref, zr_ref, zi_ref, h_ref, *, tm, chunk):
    _ada_norm_rows(x_ref, g_ref[...], sh_ref[...], sc_ref[...], h_ref, tm, chunk)
    c = FOURIER_GROUP_DIM
    for g in range(x_ref.shape[-1] // c):
        cols = slice(g * c, (g + 1) * c)
        r = jnp.dot(h_ref[:, cols], cs_ref[...], preferred_element_type=F32)
        zr_ref[:, cols] = r[:, :c].astype(zr_ref.dtype)
        zi_ref[:, cols] = (-r[:, c:]).astype(zi_ref.dtype)


def _chan_dft(x, mod5, gains, cs_chan, layer, tm):
    b, s, d = x.shape
    tok = pl.BlockSpec((None, tm, d), lambda bb, i: (bb, i, 0))
    kern = functools.partial(_chan_dft_kernel, tm=tm, chunk=256)
    return pl.pallas_call(
        kern,
        out_shape=(jax.ShapeDtypeStruct((b, s, d), BF16),) * 2,
        grid=(b, s // tm),
        in_specs=[tok, _row_in(layer, d, 2), _mod_in(layer, 0, d, 2), _mod_in(layer, 1, d, 2),
                  _const_spec(cs_chan.shape, 2)],
        out_specs=(tok, tok),
        scratch_shapes=[pltpu.VMEM((tm, d), BF16)],
        compiler_params=_params(2, 2),
        name="fourier_channel_dft",
    )(x, gains, mod5, mod5, cs_chan)


def _pos_dft1_kernel(zr_ref, zi_ref, w_ref, tc_ref, ts_ref, ur_ref, ui_ref, *, rows):
    tc = zr_ref.shape[-1]
    z = jnp.concatenate([zr_ref[...].reshape(rows, tc), zi_ref[...].reshape(rows, tc)], axis=0)
    t = jnp.dot(w_ref[...], z, preferred_element_type=F32)
    tr, ti = t[:rows], t[rows:]
    c, sn = tc_ref[...], ts_ref[...]
    ur_ref[...] = (tr * c + ti * sn).astype(ur_ref.dtype).reshape(ur_ref.shape)
    ui_ref[...] = (ti * c - tr * sn).astype(ui_ref.dtype).reshape(ui_ref.shape)


def _pos_dft1(zr, zi, w1, tw_c, tw_s):
    b, s, d = zr.shape
    n2 = FFT_N2
    n1 = s // n2
    rows = n1 * ROW_TILE
    tc = min(d, (1024 * 1024) // rows)
    view = lambda a: a.reshape(b, n1, n2 // ROW_TILE, ROW_TILE, d)
    blk = pl.BlockSpec((None, n1, None, ROW_TILE, tc), lambda bb, j, c: (bb, 0, j, 0, c))
    tw = pl.BlockSpec((None, rows, 1), lambda bb, j, c: (j, 0, 0))
    kern = functools.partial(_pos_dft1_kernel, rows=rows)
    ur, ui = pl.pallas_call(
        kern,
        out_shape=(jax.ShapeDtypeStruct((b, n1, n2 // ROW_TILE, ROW_TILE, d), BF16),) * 2,
        grid=(b, n2 // ROW_TILE, d // tc),
        in_specs=[blk, blk, _const_spec(w1.shape, 3), tw, tw],
        out_specs=(blk, blk),
        compiler_params=_params(3, 3),
        name="fourier_pos_dft_stage1",
    )(view(zr), view(zi), w1, tw_c, tw_s)
    return ur.reshape(b, s, d), ui.reshape(b, s, d)


def _pos_dft2_proj_kernel(ur_ref, ui_ref, m2_ref, w_ref, b_ref, x_ref, gate_ref, out_ref, *, inv_norm):
    rows, d = m2_ref.shape[0], x_ref.shape[-1]
    half = m2_ref.shape[1] // 2
    y = (jnp.dot(m2_ref[:, :half], ur_ref[...], preferred_element_type=F32)
         + jnp.dot(m2_ref[:, half:], ui_ref[...], preferred_element_type=F32)) * inv_norm
    proj = jnp.dot(y.astype(BF16), w_ref[...], preferred_element_type=F32) + b_ref[...]
    out = x_ref[...].reshape(rows, d) + gate_ref[...] * proj
    out_ref[...] = out.reshape(out_ref.shape)


def _pos_dft2_proj(ur, ui, m2, x, mod5, w_f, b_f, layer, w_layer):
    b, s, d = x.shape
    n2 = FFT_N2
    n1 = s // n2
    n_q = 4
    k2_rows = n2 // n_q
    u_blk = pl.BlockSpec((None, K1_GROUP * n2, d), lambda bb, a, q: (bb, a, 0))
    x_view = x.reshape(b, n2, n1 // K1_GROUP, K1_GROUP, d)
    x_blk = pl.BlockSpec((None, k2_rows, None, K1_GROUP, d), lambda bb, a, q: (bb, q, a, 0, 0))
    kern = functools.partial(_pos_dft2_proj_kernel, inv_norm=float(1.0 / np.sqrt(s)))
    out = pl.pallas_call(
        kern,
        out_shape=jax.ShapeDtypeStruct(x_view.shape, F32),
        grid=(b, n1 // K1_GROUP, n_q),
        in_specs=[u_blk, u_blk,
                  pl.BlockSpec((k2_rows * K1_GROUP, 2 * K1_GROUP * n2), lambda bb, a, q: (q, 0)),
                  pl.BlockSpec((None, d, d), lambda bb, a, q: (w_layer, 0, 0), pipeline_mode=pl.Buffered(1)),
                  _row_in(w_layer, d, 3), x_blk, _mod_in(layer, 2, d, 3)],
        out_specs=x_blk,
        compiler_params=_params(3, 2),
        name="fourier_pos_dft_stage2_proj",
    )(ur, ui, m2, w_f, b_f, x_view, mod5)
    return out.reshape(b, s, d)


def _mlp_kernel(x_ref, g_ref, sh_ref, sc_ref, gate_ref, w1_ref, b1_ref, w2_ref, b2_ref, fg_ref,
                out_ref, h_ref, acc_ref, *, tm, chunk, final_norm):
    j = pl.program_id(2)

    @pl.when(j == 0)
    def _():
        _ada_norm_rows(x_ref, g_ref[...], sh_ref[...], sc_ref[...], h_ref, tm, chunk)
        acc_ref[...] = jnp.zeros_like(acc_ref)

    u = jnp.dot(h_ref[...], w1_ref[...], preferred_element_type=F32) + b1_ref[...]
    u = jnp.maximum(u, 0.0)
    acc_ref[...] += jnp.dot((u * u).astype(BF16), w2_ref[...], preferred_element_type=F32)

    @pl.when(j == pl.num_programs(2) - 1)
    def _():
        y = x_ref[...] + gate_ref[...] * (acc_ref[...] + b2_ref[...])
        if final_norm:
            ms = jnp.mean(y * y, axis=-1, keepdims=True)
            y = y * lax.rsqrt(ms + RMS_EPS) * fg_ref[...]
        out_ref[...] = y


def _mlp(x, mod5, gains, w1, b1, w2, b2, final_g, layer, tm, tf, final_norm):
    b, s, d = x.shape
    dff = w1.shape[-1]
    kern = functools.partial(_mlp_kernel, tm=tm, chunk=256, final_norm=final_norm)
    tok = pl.BlockSpec((None, tm, d), lambda bb, i, j: (bb, i, 0))
    return pl.pallas_call(
        kern,
        out_shape=jax.ShapeDtypeStruct((b, s, d), F32),
        grid=(b, s // tm, dff // tf),
        in_specs=[
            tok,
            _row_in(layer, d, 3),
            _mod_in(layer, 3, d, 3),
            _mod_in(layer, 4, d, 3),
            _mod_in(layer, 5, d, 3),
            pl.BlockSpec((None, d, tf), lambda bb, i, j: (layer, 0, j)),
            pl.BlockSpec((None, 1, tf), lambda bb, i, j: (layer, 0, j)),
            pl.BlockSpec((None, tf, d), lambda bb, i, j: (layer, j, 0)),
            _row_in(layer, d, 3),
            pl.BlockSpec((1, d), lambda bb, i, j: (0, 0)),
        ],
        out_specs=tok,
        scratch_shapes=[pltpu.VMEM((tm, d), BF16), pltpu.VMEM((tm, d), F32)],
        compiler_params=_params(3, 2),
        name="sqrelu_mlp",
    )(x, gains, mod5, mod5, mod5, w1, b1, w2, b2, final_g)


def _trunk(x, mod5, p):
    depth = p["w1"].shape[0]
    s = x.shape[1]
    cs_chan, w1c, tw_c, tw_s, m2 = _dft_tables(s)
    for i in range(depth):
        sub = i // 2
        if i % 2 == 0:
            qkv = _norm_proj(x, mod5, p["norm1_g"], p["w_qkv"], p["perms"], i, sub, tm=1024, tn=1152)
            outs, lses = zip(*[_group_attention(qkv, g, p["perms_t"]) for g in range(len(ATTN_WINDOWS))])
            x = _merge_proj(outs, lses, x, mod5, p["w_o"], i, sub, tm=512)
        else:
            zr, zi = _chan_dft(x, mod5, p["norm1_g"], cs_chan, i, tm=512)
            ur, ui = _pos_dft1(zr, zi, w1c, tw_c, tw_s)
            x = _pos_dft2_proj(ur, ui, m2, x, mod5, p["w_f"], p["b_f"], i, sub)
        x = _mlp(x, mod5, p["norm2_g"], p["w1"], p["b1"], p["w2"], p["b2"], p["final_g"], i,
                 tm=512, tf=1024, final_norm=(i == depth - 1))
    return x


def kernel(x_prompt, x_sample, c_prompt, c_sample, w_ada, b_ada, norm1_g, norm2_g, w_qkv, w_o, w_f, b_f,
           w1, b1, w2, b2, final_g):
    depth, d, _ = w_ada.shape
    n_p, n_s = c_prompt.shape[0], c_sample.shape[0]
    rows = -(-(n_p + n_s) // ROW_TILE) * ROW_TILE
    c_all = jnp.concatenate([c_prompt, c_sample, jnp.zeros((rows - n_p - n_s, d), F32)], axis=0)
    mod = _modulation(c_all, w_ada, b_ada)
    mod_p = mod[:, :n_p].reshape(depth, n_p, N_MOD, 1, d)
    mod_s = mod[:, n_p:n_p + n_s].reshape(depth, n_s, N_MOD, 1, d)

    row3 = lambda a: a.reshape(a.shape[0], 1, a.shape[-1])
    perms = _residue_perms()
    params = {
        "norm1_g": row3(norm1_g), "norm2_g": row3(norm2_g),
        "w_qkv": w_qkv.astype(BF16), "w_o": w_o.astype(BF16),
        "w_f": w_f.astype(BF16), "b_f": row3(b_f),
        "w1": w1.astype(BF16), "b1": row3(b1), "w2": w2.astype(BF16), "b2": row3(b2),
        "final_g": final_g.reshape(1, d),
        "perms": jnp.asarray(perms, BF16),
        "perms_t": jnp.asarray(np.transpose(perms, (0, 2, 1)), BF16),
    }
    return (_trunk(x_prompt, mod_p, params), _trunk(x_sample, mod_s, params))
```

```python
import functools

import numpy as np
import jax
import jax.numpy as jnp
from jax import lax
from jax.experimental import pallas as pl
from jax.experimental.pallas import tpu as pltpu

F32 = jnp.float32
BF16 = jnp.bfloat16

N_MOD = 6
RMS_EPS = 1e-6
MASK_VALUE = -1e30
ATTN_WINDOWS = ((128, 1), (512, 4), (2048, 16))
HEADS_PER_GROUP = 6
HEAD_DIM = 128
N_ATTN_HEADS = len(ATTN_WINDOWS) * HEADS_PER_GROUP
GROUP_WIDTH = HEADS_PER_GROUP * HEAD_DIM
FOURIER_GROUP_DIM = 256
HALF = 64
FFT_N2 = 128
LANES = 128
ROW_TILE = 16
PERM_BLOCK = 256
VMEM_LIMIT = 56 * 1024 * 1024


def _params(n_axes, n_parallel):
    sem = ("parallel",) * n_parallel + ("arbitrary",) * (n_axes - n_parallel)
    return pltpu.CompilerParams(dimension_semantics=sem, vmem_limit_bytes=VMEM_LIMIT)


def _const_spec(shape, n_grid):
    zeros = (0,) * len(shape)
    imap = (lambda a, b: zeros) if n_grid == 2 else (lambda a, b, c: zeros)
    return pl.BlockSpec(shape, imap, pipeline_mode=pl.Buffered(1))


NORM_CHUNK = 64


def _norm_scratch(tm, d):
    return [pltpu.VMEM((tm, LANES), F32), pltpu.VMEM((2, d), F32)]


def _ada_norm_rows(x_ref, gain, shift, scale, h_ref, rs_ref, ab_ref, rows, load=None):
    d = h_ref.shape[-1]
    slabs = d // LANES
    if load is None:
        load = lambda c, cols: x_ref[pl.ds(pl.multiple_of(c * NORM_CHUNK, NORM_CHUNK), NORM_CHUNK), cols]
    ab_ref[0:1, :] = gain * (1.0 + scale)
    ab_ref[1:2, :] = shift

    def stats(c, carry):
        r0 = pl.multiple_of(c * NORM_CHUNK, NORM_CHUNK)
        acc = jnp.zeros((NORM_CHUNK, LANES), F32)
        for t in range(slabs):
            xt = load(c, slice(t * LANES, (t + 1) * LANES))
            acc = acc + xt * xt
        ms = jnp.sum(acc, axis=-1, keepdims=True) * (1.0 / d)
        rs_ref[pl.ds(r0, NORM_CHUNK), :] = jnp.broadcast_to(lax.rsqrt(ms + RMS_EPS), (NORM_CHUNK, LANES))
        return carry

    lax.fori_loop(0, rows // NORM_CHUNK, stats, 0, unroll=4)

    def apply(c, carry):
        r0 = pl.multiple_of(c * NORM_CHUNK, NORM_CHUNK)
        rs = rs_ref[pl.ds(r0, NORM_CHUNK), :]
        for t in range(slabs):
            cols = slice(t * LANES, (t + 1) * LANES)
            h = load(c, cols) * rs * ab_ref[0:1, cols] + ab_ref[1:2, cols]
            h_ref[pl.ds(r0, NORM_CHUNK), cols] = h.astype(h_ref.dtype)
        return carry

    lax.fori_loop(0, rows // NORM_CHUNK, apply, 0)


def _mod_kernel(c_ref, w_ref, b_ref, o_ref):
    c = c_ref[...]
    act = (c * jax.nn.sigmoid(c)).astype(BF16)
    w = w_ref[...].astype(BF16)
    o_ref[...] = jnp.dot(act, w, preferred_element_type=F32) + b_ref[...]


def _modulation(c_all, w_ada, b_ada):
    depth, d, n = w_ada.shape
    rows = c_all.shape[0]
    tn = 1024
    return pl.pallas_call(
        _mod_kernel,
        out_shape=jax.ShapeDtypeStruct((depth, rows, n), F32),
        grid=(depth, n // tn),
        in_specs=[
            pl.BlockSpec((rows, d), lambda l, j: (0, 0)),
            pl.BlockSpec((None, d, tn), lambda l, j: (l, 0, j)),
            pl.BlockSpec((None, 1, tn), lambda l, j: (l, 0, j)),
        ],
        out_specs=pl.BlockSpec((None, rows, tn), lambda l, j: (l, 0, j)),
        compiler_params=_params(2, 2),
        name="adaln_modulation",
    )(c_all, w_ada, b_ada.reshape(depth, 1, n))


def _mod_in(layer, which, d, n_grid):
    if n_grid == 2:
        imap = lambda b, i: (layer, b, which, 0, 0)
    else:
        imap = lambda b, i, j: (layer, b, which, 0, 0)
    return pl.BlockSpec((None, None, None, 1, d), imap)


def _row_in(layer, d, n_grid):
    if n_grid == 2:
        imap = lambda b, i: (layer, 0, 0)
    else:
        imap = lambda b, i, j: (layer, 0, 0)
    return pl.BlockSpec((None, 1, d), imap)


def _residue_perms():
    mats = []
    for _, dil in ATTN_WINDOWS[1:]:
        ub = PERM_BLOCK // dil
        p = np.zeros((PERM_BLOCK, PERM_BLOCK), np.float32)
        nat = np.arange(PERM_BLOCK)
        p[(nat % dil) * ub + nat // dil, nat] = 1.0
        mats.append(p)
    return np.stack(mats)


def _norm_proj_kernel(x_ref, g_ref, sh_ref, sc_ref, w_ref, p_ref, o_ref, h_ref, hp_ref, rs_ref, ab_ref, *,
                      tm, steps_per_group):
    j = pl.program_id(2)

    @pl.when(j == 0)
    def _():
        _ada_norm_rows(x_ref, g_ref[...], sh_ref[...], sc_ref[...], h_ref, rs_ref, ab_ref, tm)

    @pl.when((j >= steps_per_group) & (j % steps_per_group == 0))
    def _():
        for blk in range(tm // PERM_BLOCK):
            rows = slice(blk * PERM_BLOCK, (blk + 1) * PERM_BLOCK)
            hp_ref[rows, :] = jnp.dot(p_ref[...], h_ref[rows, :], preferred_element_type=F32).astype(hp_ref.dtype)

    @pl.when(j < steps_per_group)
    def _():
        o_ref[...] = jnp.dot(h_ref[...], w_ref[...], preferred_element_type=F32).astype(o_ref.dtype)

    @pl.when(j >= steps_per_group)
    def _():
        o_ref[...] = jnp.dot(hp_ref[...], w_ref[...], preferred_element_type=F32).astype(o_ref.dtype)


def _norm_proj(x, mod5, gains, w, perms, layer, w_layer, tm, tn):
    b, s, d = x.shape
    n = w.shape[-1]
    steps_per_group = (n // len(ATTN_WINDOWS)) // tn
    kern = functools.partial(_norm_proj_kernel, tm=tm, steps_per_group=steps_per_group)
    return pl.pallas_call(
        kern,
        out_shape=jax.ShapeDtypeStruct((b, s, n), BF16),
        grid=(b, s // tm, n // tn),
        in_specs=[
            pl.BlockSpec((None, tm, d), lambda bb, i, j: (bb, i, 0)),
            _row_in(layer, d, 3),
            _mod_in(layer, 0, d, 3),
            _mod_in(layer, 1, d, 3),
            pl.BlockSpec((None, d, tn), lambda bb, i, j: (w_layer, 0, j)),
            pl.BlockSpec((None, PERM_BLOCK, PERM_BLOCK),
                         lambda bb, i, j: (jnp.maximum(j // steps_per_group, 1) - 1, 0, 0)),
        ],
        out_specs=pl.BlockSpec((None, tm, tn), lambda bb, i, j: (bb, i, j)),
        scratch_shapes=[pltpu.VMEM((tm, d), BF16), pltpu.VMEM((tm, d), BF16)] + _norm_scratch(tm, d),
        compiler_params=_params(3, 2),
        name="norm_qkv_proj",
    )(x, gains, mod5, mod5, w, perms)


def _attn_kernel(q_ref, kp_ref, km_ref, kn_ref, vp_ref, vm_ref, vn_ref, pt_ref, o_ref, lse_ref,
                 kc_ref, vc_ref, op_ref, lp_ref, *, nblk, ub, n_u, dilation, slopes):
    i = pl.program_id(1)
    tq = nblk * ub
    width = q_ref.shape[-1]
    sub = 2 * HALF
    span = sub + 2 * HALF
    row = lax.broadcasted_iota(jnp.int32, (sub, span), 0)
    col = lax.broadcasted_iota(jnp.int32, (sub, span), 1)
    adu = jnp.abs(col - HALF - row)
    band = adu <= HALF
    dist = (adu * dilation).astype(F32)
    lane = lax.broadcasted_iota(jnp.int32, (sub, LANES), 1)
    scale = HEAD_DIM ** -0.5
    blocks_per_sub = sub // ub

    def one_class(r):
        def gather(dst, prev, main, nxt):
            dst[0:HALF, :] = prev[:, r].reshape(HALF, width)
            dst[HALF:HALF + tq, :] = main[:, r].reshape(tq, width)
            dst[HALF + tq:, :] = nxt[:, r].reshape(HALF, width)

        gather(kc_ref, kp_ref, km_ref, kn_ref)
        gather(vc_ref, vp_ref, vm_ref, vn_ref)
        for sb in range(tq // sub):
            u_key = i * tq + (sb * sub - HALF) + col
            valid = band & (u_key >= 0) & (u_key < n_u)
            lse_tile = jnp.zeros((sub, LANES), F32)
            rows = slice(sb * sub, (sb + 1) * sub)
            for h in range(HEADS_PER_GROUP):
                cs = slice(h * HEAD_DIM, (h + 1) * HEAD_DIM)
                q = q_ref[sb * blocks_per_sub:(sb + 1) * blocks_per_sub, r, :, cs].reshape(sub, HEAD_DIM)
                k = kc_ref[sb * sub:sb * sub + span, cs]
                v = vc_ref[sb * sub:sb * sub + span, cs]
                s = lax.dot_general(q, k, (((1,), (1,)), ((), ())), preferred_element_type=F32) * scale
                s = s - slopes[h] * dist
                s = jnp.where(valid, s, MASK_VALUE)
                m = jnp.max(s, axis=-1, keepdims=True)
                p = jnp.exp(s - m)
                l = jnp.sum(p, axis=-1, keepdims=True)
                o = jnp.dot(p.astype(BF16), v, preferred_element_type=F32) * (1.0 / l)
                if dilation == 1:
                    o_ref[rows, cs] = o.astype(o_ref.dtype)
                else:
                    op_ref[r, rows, cs] = o.astype(op_ref.dtype)
                lse_tile = jnp.where(lane == h, m + jnp.log(l), lse_tile)
            if dilation == 1:
                lse_ref[rows, :] = lse_tile
            else:
                lp_ref[r, rows, :] = lse_tile

    if dilation == 1:
        one_class(0)
        return

    def class_step(r, carry):
        one_class(r)
        return carry

    lax.fori_loop(0, dilation, class_step, 0)
    pt = pt_ref[...]
    for blk in range(nblk):
        urows = slice(blk * ub, (blk + 1) * ub)
        nat = slice(blk * PERM_BLOCK, (blk + 1) * PERM_BLOCK)
        ob = jnp.concatenate([op_ref[rr, urows, :] for rr in range(dilation)], axis=0)
        o_ref[nat, :] = jnp.dot(pt, ob, preferred_element_type=F32).astype(o_ref.dtype)
        lb = jnp.concatenate([lp_ref[rr, urows, :] for rr in range(dilation)], axis=0)
        hi = lb.astype(BF16)
        rest = lb - hi.astype(F32)
        mid = rest.astype(BF16)
        lo = (rest - mid.astype(F32)).astype(BF16)
        lse_ref[nat, :] = (jnp.dot(pt, hi, preferred_element_type=F32)
                           + jnp.dot(pt, mid, preferred_element_type=F32)
                           + jnp.dot(pt, lo, preferred_element_type=F32))


def _group_attention(qkv, group, perms_t):
    b, s, width = qkv.shape
    _, dilation = ATTN_WINDOWS[group]
    n_u = s // dilation
    ub = HALF if dilation == 1 else PERM_BLOCK // dilation
    rb = ub * dilation
    tq = min({1: 512, 4: 256, 16: 128}[dilation], n_u)
    nblk = tq // ub
    hb = HALF // ub
    n_slabs = width // GROUP_WIDTH
    view = qkv.reshape(b, s // rb, dilation, ub, width)
    n_halo = (s // rb) // hb
    slopes = tuple(float(np.exp2(np.float32(-8.0 * (group * HEADS_PER_GROUP + h + 1) / N_ATTN_HEADS)))
                   for h in range(HEADS_PER_GROUP))

    def main(t):
        return pl.BlockSpec((None, nblk, dilation, ub, GROUP_WIDTH),
                            lambda bb, i: (bb, i, 0, 0, group * 3 + t))

    def halo_p(t):
        return pl.BlockSpec((None, hb, dilation, ub, GROUP_WIDTH),
                            lambda bb, i: (bb, jnp.maximum(i * (nblk // hb) - 1, 0), 0, 0, group * 3 + t))

    def halo_n(t):
        return pl.BlockSpec((None, hb, dilation, ub, GROUP_WIDTH),
                            lambda bb, i: (bb, jnp.minimum((i + 1) * (nblk // hb), n_halo - 1), 0, 0,
                                           group * 3 + t))

    assert nblk % hb == 0 and width == n_slabs * GROUP_WIDTH
    pt = perms_t[max(group - 1, 0)]
    kern = functools.partial(_attn_kernel, nblk=nblk, ub=ub, n_u=n_u, dilation=dilation, slopes=slopes)
    scratch = [pltpu.VMEM((tq + 2 * HALF, GROUP_WIDTH), BF16), pltpu.VMEM((tq + 2 * HALF, GROUP_WIDTH), BF16),
               pltpu.VMEM((dilation, tq, GROUP_WIDTH), BF16), pltpu.VMEM((dilation, tq, LANES), F32)]
    rows = nblk * rb
    return pl.pallas_call(
        kern,
        out_shape=(jax.ShapeDtypeStruct((b, s, GROUP_WIDTH), BF16),
                   jax.ShapeDtypeStruct((b, s, LANES), F32)),
        grid=(b, s // rows),
        in_specs=[main(0), halo_p(1), main(1), halo_n(1), halo_p(2), main(2), halo_n(2),
                  _const_spec(pt.shape, 2)],
        out_specs=(pl.BlockSpec((None, rows, GROUP_WIDTH), lambda bb, i: (bb, i, 0)),
                   pl.BlockSpec((None, rows, LANES), lambda bb, i: (bb, i, 0))),
        scratch_shapes=scratch,
        compiler_params=_params(2, 2),
        name=f"dilated_attention_g{group}",
    )(view, view, view, view, view, view, view, pt)


def _merge_proj_kernel(o0_ref, o1_ref, o2_ref, l0_ref, l1_ref, l2_ref, x_ref, gate_ref, w_ref,
                       out_ref, mix_ref):
    l0, l1, l2 = l0_ref[...], l1_ref[...], l2_ref[...]
    m = jnp.maximum(jnp.maximum(l0, l1), l2)
    e0, e1, e2 = jnp.exp(l0 - m), jnp.exp(l1 - m), jnp.exp(l2 - m)
    inv = 1.0 / (e0 + e1 + e2)
    for g, (o_ref, e) in enumerate(((o0_ref, e0), (o1_ref, e1), (o2_ref, e2))):
        alpha = e * inv
        for h in range(HEADS_PER_GROUP):
            src = slice(h * HEAD_DIM, (h + 1) * HEAD_DIM)
            dst = slice((g * HEADS_PER_GROUP + h) * HEAD_DIM, (g * HEADS_PER_GROUP + h + 1) * HEAD_DIM)
            mix_ref[:, dst] = (alpha[:, h:h + 1] * o_ref[:, src].astype(F32)).astype(mix_ref.dtype)
    y = jnp.dot(mix_ref[...], w_ref[...], preferred_element_type=F32)
    out_ref[...] = x_ref[...] + gate_ref[...] * y


def _merge_proj(outs, lses, x, mod5, w_o, layer, w_layer, tm):
    b, s, d = x.shape
    width = w_o.shape[1]
    tok = lambda c: pl.BlockSpec((None, tm, c), lambda bb, i: (bb, i, 0))
    return pl.pallas_call(
        _merge_proj_kernel,
        out_shape=jax.ShapeDtypeStruct((b, s, d), F32),
        grid=(b, s // tm),
        in_specs=[tok(GROUP_WIDTH)] * 3 + [tok(LANES)] * 3 + [
            tok(d),
            _mod_in(layer, 2, d, 2),
            pl.BlockSpec((None, width, d), lambda bb, i: (w_layer, 0, 0)),
        ],
        out_specs=tok(d),
        scratch_shapes=[pltpu.VMEM((tm, width), BF16)],
        compiler_params=_params(2, 2),
        name="attn_merge_out_proj",
    )(*outs, *lses, x, mod5, w_o)


K1_GROUP = 8
FUSED_STAGE1_ROWS = 256


def _dft_tables(s):
    n2 = FFT_N2
    n1 = s // n2
    c = FOURIER_GROUP_DIM
    ang_c = 2.0 * np.pi * np.outer(np.arange(c), np.arange(c)) / c
    cs_chan = np.concatenate([np.cos(ang_c), np.sin(ang_c)], axis=1) / np.sqrt(c)
    a1 = 2.0 * np.pi * np.outer(np.arange(n1), np.arange(n1)) / n1
    c1, s1 = np.cos(a1), np.sin(a1)
    w1 = np.kron(np.block([[c1, s1], [-s1, c1]]), np.eye(ROW_TILE))
    s2 = (np.arange(n2 // ROW_TILE)[:, None, None] * ROW_TILE + np.arange(ROW_TILE)[None, None, :])
    th = 2.0 * np.pi * np.arange(n1)[None, :, None] * s2 / s
    th = th.reshape(n2 // ROW_TILE, n1 * ROW_TILE, 1)
    a2 = 2.0 * np.pi * np.outer(np.arange(n2), np.arange(n2)) / n2
    eye = np.eye(K1_GROUP)
    m2 = np.concatenate([np.einsum("ks,ab->kabs", f(a2), eye).reshape(n2 * K1_GROUP, K1_GROUP * n2)
                         for f in (np.cos, np.sin)], axis=1)
    return (jnp.asarray(cs_chan, BF16), jnp.asarray(w1, BF16),
            jnp.asarray(np.cos(th), F32), jnp.asarray(np.sin(th), F32), jnp.asarray(m2, BF16))


def _chan_dft_kernel(x_ref, g_ref, sh_ref, sc_ref, cs_ref, zr_ref, zi_ref, h_ref, rs_ref, ab_ref, *, tm):
    _ada_norm_rows(x_ref, g_ref[...], sh_ref[...], sc_ref[...], h_ref, rs_ref, ab_ref, tm)
    c = FOURIER_GROUP_DIM
    for g in range(x_ref.shape[-1] // c):
        cols = slice(g * c, (g + 1) * c)
        r = jnp.dot(h_ref[:, cols], cs_ref[...], preferred_element_type=F32)
        zr_ref[:, cols] = r[:, :c].astype(zr_ref.dtype)
        zi_ref[:, cols] = (-r[:, c:]).astype(zi_ref.dtype)


def _chan_dft(x, mod5, gains, cs_chan, layer, tm):
    b, s, d = x.shape
    tok = pl.BlockSpec((None, tm, d), lambda bb, i: (bb, i, 0))
    kern = functools.partial(_chan_dft_kernel, tm=tm)
    return pl.pallas_call(
        kern,
        out_shape=(jax.ShapeDtypeStruct((b, s, d), BF16),) * 2,
        grid=(b, s // tm),
        in_specs=[tok, _row_in(layer, d, 2), _mod_in(layer, 0, d, 2), _mod_in(layer, 1, d, 2),
                  _const_spec(cs_chan.shape, 2)],
        out_specs=(tok, tok),
        scratch_shapes=[pltpu.VMEM((tm, d), BF16)] + _norm_scratch(tm, d),
        compiler_params=_params(2, 2),
        name="fourier_channel_dft",
    )(x, gains, mod5, mod5, cs_chan)


def _pos_dft1_kernel(zr_ref, zi_ref, w_ref, tc_ref, ts_ref, ur_ref, ui_ref, *, rows):
    tc = zr_ref.shape[-1]
    z = jnp.concatenate([zr_ref[...].reshape(rows, tc), zi_ref[...].reshape(rows, tc)], axis=0)
    t = jnp.dot(w_ref[...], z, preferred_element_type=F32)
    tr, ti = t[:rows], t[rows:]
    c, sn = tc_ref[...], ts_ref[...]
    ur_ref[...] = (tr * c + ti * sn).astype(ur_ref.dtype).reshape(ur_ref.shape)
    ui_ref[...] = (ti * c - tr * sn).astype(ui_ref.dtype).reshape(ui_ref.shape)


def _pos_dft1(zr, zi, w1, tw_c, tw_s):
    b, s, d = zr.shape
    n2 = FFT_N2
    n1 = s // n2
    rows = n1 * ROW_TILE
    tc = min(d, (1024 * 1024) // rows)
    view = lambda a: a.reshape(b, n1, n2 // ROW_TILE, ROW_TILE, d)
    blk = pl.BlockSpec((None, n1, None, ROW_TILE, tc), lambda bb, j, c: (bb, 0, j, 0, c))
    tw = pl.BlockSpec((None, rows, 1), lambda bb, j, c: (j, 0, 0))
    kern = functools.partial(_pos_dft1_kernel, rows=rows)
    ur, ui = pl.pallas_call(
        kern,
        out_shape=(jax.ShapeDtypeStruct((b, n1, n2 // ROW_TILE, ROW_TILE, d), BF16),) * 2,
        grid=(b, n2 // ROW_TILE, d // tc),
        in_specs=[blk, blk, _const_spec(w1.shape, 3), tw, tw],
        out_specs=(blk, blk),
        compiler_params=_params(3, 3),
        name="fourier_pos_dft_stage1",
    )(view(zr), view(zi), w1, tw_c, tw_s)
    return ur.reshape(b, s, d), ui.reshape(b, s, d)


def _chan_pos1_kernel(x_ref, g_ref, sh_ref, sc_ref, cs_ref, w_ref, tc_ref, ts_ref, ur_ref, ui_ref,
                      h_ref, z_ref, rs_ref, ab_ref, *, rows):
    groups = NORM_CHUNK // ROW_TILE
    load = lambda c, cols: x_ref[pl.ds(pl.multiple_of(c * groups, groups), groups), :, cols].reshape(
        NORM_CHUNK, LANES)
    _ada_norm_rows(x_ref, g_ref[...], sh_ref[...], sc_ref[...], h_ref, rs_ref, ab_ref, rows, load)
    c = FOURIER_GROUP_DIM
    for g in range(h_ref.shape[-1] // c):
        cols = slice(g * c, (g + 1) * c)
        r = jnp.dot(h_ref[:, cols], cs_ref[...], preferred_element_type=F32)
        z_ref[0:rows, cols] = r[:, :c].astype(z_ref.dtype)
        z_ref[rows:, cols] = (-r[:, c:]).astype(z_ref.dtype)
    t = jnp.dot(w_ref[...], z_ref[...], preferred_element_type=F32)
    tr, ti = t[:rows], t[rows:]
    cw, sw = tc_ref[...], ts_ref[...]
    ur_ref[...] = (tr * cw + ti * sw).astype(ur_ref.dtype).reshape(ur_ref.shape)
    ui_ref[...] = (ti * cw - tr * sw).astype(ui_ref.dtype).reshape(ui_ref.shape)


def _chan_pos1(x, mod5, gains, cs_chan, w1, tw_c, tw_s, layer):
    b, s, d = x.shape
    n2 = FFT_N2
    n1 = s // n2
    rows = n1 * ROW_TILE
    shape5 = (b, n1, n2 // ROW_TILE, ROW_TILE, d)
    blk = pl.BlockSpec((None, n1, None, ROW_TILE, d), lambda bb, j: (bb, 0, j, 0, 0))
    tw = pl.BlockSpec((None, rows, 1), lambda bb, j: (j, 0, 0))
    kern = functools.partial(_chan_pos1_kernel, rows=rows)
    ur, ui = pl.pallas_call(
        kern,
        out_shape=(jax.ShapeDtypeStruct(shape5, BF16),) * 2,
        grid=(b, n2 // ROW_TILE),
        in_specs=[blk, _row_in(layer, d, 2), _mod_in(layer, 0, d, 2), _mod_in(layer, 1, d, 2),
                  _const_spec(cs_chan.shape, 2), _const_spec(w1.shape, 2), tw, tw],
        out_specs=(blk, blk),
        scratch_shapes=[pltpu.VMEM((rows, d), BF16), pltpu.VMEM((2 * rows, d), BF16)] + _norm_scratch(rows, d),
        compiler_params=_params(2, 2),
        name="fourier_chan_pos_dft_stage1",
    )(x.reshape(shape5), gains, mod5, mod5, cs_chan, w1, tw_c, tw_s)
    return ur.reshape(b, s, d), ui.reshape(b, s, d)


def _pos_dft2_proj_kernel(ur_ref, ui_ref, m2_ref, w_ref, b_ref, x_ref, gate_ref, out_ref, *, inv_norm):
    rows, d = m2_ref.shape[0], x_ref.shape[-1]
    half = m2_ref.shape[1] // 2
    y = (jnp.dot(m2_ref[:, :half], ur_ref[...], preferred_element_type=F32)
         + jnp.dot(m2_ref[:, half:], ui_ref[...], preferred_element_type=F32)) * inv_norm
    proj = jnp.dot(y.astype(BF16), w_ref[...], preferred_element_type=F32) + b_ref[...]
    out = x_ref[...].reshape(rows, d) + gate_ref[...] * proj
    out_ref[...] = out.reshape(out_ref.shape)


def _pos_dft2_proj(ur, ui, m2, x, mod5, w_f, b_f, layer, w_layer):
    b, s, d = x.shape
    n2 = FFT_N2
    n1 = s // n2
    n_q = 4
    k2_rows = n2 // n_q
    u_blk = pl.BlockSpec((None, K1_GROUP * n2, d), lambda bb, a, q: (bb, a, 0))
    x_view = x.reshape(b, n2, n1 // K1_GROUP, K1_GROUP, d)
    x_blk = pl.BlockSpec((None, k2_rows, None, K1_GROUP, d), lambda bb, a, q: (bb, q, a, 0, 0))
    kern = functools.partial(_pos_dft2_proj_kernel, inv_norm=float(1.0 / np.sqrt(s)))
    out = pl.pallas_call(
        kern,
        out_shape=jax.ShapeDtypeStruct(x_view.shape, F32),
        grid=(b, n1 // K1_GROUP, n_q),
        in_specs=[u_blk, u_blk,
                  pl.BlockSpec((k2_rows * K1_GROUP, 2 * K1_GROUP * n2), lambda bb, a, q: (q, 0)),
                  pl.BlockSpec((None, d, d), lambda bb, a, q: (w_layer, 0, 0), pipeline_mode=pl.Buffered(1)),
                  _row_in(w_layer, d, 3), x_blk, _mod_in(layer, 2, d, 3)],
        out_specs=x_blk,
        compiler_params=_params(3, 2),
        name="fourier_pos_dft_stage2_proj",
    )(ur, ui, m2, w_f, b_f, x_view, mod5)
    return out.reshape(b, s, d)


def _mlp_kernel(x_ref, g_ref, sh_ref, sc_ref, gate_ref, w1_ref, b1_ref, w2_ref, b2_ref, fg_ref,
                out_ref, h_ref, acc_ref, rs_ref, ab_ref, *, tm, final_norm):
    j = pl.program_id(2)

    @pl.when(j == 0)
    def _():
        _ada_norm_rows(x_ref, g_ref[...], sh_ref[...], sc_ref[...], h_ref, rs_ref, ab_ref, tm)
        acc_ref[...] = jnp.zeros_like(acc_ref)

    u = jnp.dot(h_ref[...], w1_ref[...], preferred_element_type=F32) + b1_ref[...]
    u = jnp.maximum(u, 0.0)
    acc_ref[...] += jnp.dot((u * u).astype(BF16), w2_ref[...], preferred_element_type=F32)

    @pl.when(j == pl.num_programs(2) - 1)
    def _():
        y = x_ref[...] + gate_ref[...] * (acc_ref[...] + b2_ref[...])
        if final_norm:
            ms = jnp.mean(y * y, axis=-1, keepdims=True)
            y = y * lax.rsqrt(ms + RMS_EPS) * fg_ref[...]
        out_ref[...] = y


def _mlp(x, mod5, gains, w1, b1, w2, b2, final_g, layer, tm, tf, final_norm):
    b, s, d = x.shape
    dff = w1.shape[-1]
    kern = functools.partial(_mlp_kernel, tm=tm, final_norm=final_norm)
    tok = pl.BlockSpec((None, tm, d), lambda bb, i, j: (bb, i, 0))
    return pl.pallas_call(
        kern,
        out_shape=jax.ShapeDtypeStruct((b, s, d), F32),
        grid=(b, s // tm, dff // tf),
        in_specs=[
            tok,
            _row_in(layer, d, 3),
            _mod_in(layer, 3, d, 3),
            _mod_in(layer, 4, d, 3),
            _mod_in(layer, 5, d, 3),
            pl.BlockSpec((None, d, tf), lambda bb, i, j: (layer, 0, j)),
            pl.BlockSpec((None, 1, tf), lambda bb, i, j: (layer, 0, j)),
            pl.BlockSpec((None, tf, d), lambda bb, i, j: (layer, j, 0)),
            _row_in(layer, d, 3),
            pl.BlockSpec((1, d), lambda bb, i, j: (0, 0)),
        ],
        out_specs=tok,
        scratch_shapes=[pltpu.VMEM((tm, d), BF16), pltpu.VMEM((tm, d), F32)] + _norm_scratch(tm, d),
        compiler_params=_params(3, 2),
        name="sqrelu_mlp",
    )(x, gains, mod5, mod5, mod5, w1, b1, w2, b2, final_g)


def _trunk(x, mod5, p):
    depth = p["w1"].shape[0]
    s = x.shape[1]
    cs_chan, w1c, tw_c, tw_s, m2 = _dft_tables(s)
    for i in range(depth):
        sub = i // 2
        if i % 2 == 0:
            qkv = _norm_proj(x, mod5, p["norm1_g"], p["w_qkv"], p["perms"], i, sub, tm=1024, tn=768)
            outs, lses = zip(*[_group_attention(qkv, g, p["perms_t"]) for g in range(len(ATTN_WINDOWS))])
            x = _merge_proj(outs, lses, x, mod5, p["w_o"], i, sub, tm=512)
        else:
            if (s // FFT_N2) * ROW_TILE <= FUSED_STAGE1_ROWS:
                ur, ui = _chan_pos1(x, mod5, p["norm1_g"], cs_chan, w1c, tw_c, tw_s, i)
            else:
                zr, zi = _chan_dft(x, mod5, p["norm1_g"], cs_chan, i, tm=512)
                ur, ui = _pos_dft1(zr, zi, w1c, tw_c, tw_s)
            x = _pos_dft2_proj(ur, ui, m2, x, mod5, p["w_f"], p["b_f"], i, sub)
        x = _mlp(x, mod5, p["norm2_g"], p["w1"], p["b1"], p["w2"], p["b2"], p["final_g"], i,
                 tm=512, tf=1024, final_norm=(i == depth - 1))
    return x


def kernel(x_prompt, x_sample, c_prompt, c_sample, w_ada, b_ada, norm1_g, norm2_g, w_qkv, w_o, w_f, b_f,
           w1, b1, w2, b2, final_g):
    depth, d, _ = w_ada.shape
    n_p, n_s = c_prompt.shape[0], c_sample.shape[0]
    rows = -(-(n_p + n_s) // ROW_TILE) * ROW_TILE
    c_all = jnp.concatenate([c_prompt, c_sample, jnp.zeros((rows - n_p - n_s, d), F32)], axis=0)
    mod = _modulation(c_all, w_ada, b_ada)
    mod_p = mod[:, :n_p].reshape(depth, n_p, N_MOD, 1, d)
    mod_s = mod[:, n_p:n_p + n_s].reshape(depth, n_s, N_MOD, 1, d)

    row3 = lambda a: a.reshape(a.shape[0], 1, a.shape[-1])
    perms = _residue_perms()
    params = {
        "norm1_g": row3(norm1_g), "norm2_g": row3(norm2_g),
        "w_qkv": w_qkv.astype(BF16), "w_o": w_o.astype(BF16),
        "w_f": w_f.astype(BF16), "b_f": row3(b_f),
        "w1": w1.astype(BF16), "b1": row3(b1), "w2": w2.astype(BF16), "b2": row3(b2),
        "final_g": final_g.reshape(1, d),
        "perms": jnp.asarray(perms, BF16),
        "perms_t": jnp.asarray(np.transpose(perms, (0, 2, 1)), BF16),
    }
    return (_trunk(x_prompt, mod_p, params), _trunk(x_sample, mod_s, params))
```

```python
import functools

import numpy as np
import jax
import jax.numpy as jnp
from jax import lax
from jax.experimental import pallas as pl
from jax.experimental.pallas import tpu as pltpu

F32 = jnp.float32
BF16 = jnp.bfloat16

N_MOD = 6
RMS_EPS = 1e-6
MASK_VALUE = -1e30
ATTN_WINDOWS = ((128, 1), (512, 4), (2048, 16))
HEADS_PER_GROUP = 6
HEAD_DIM = 128
N_ATTN_HEADS = len(ATTN_WINDOWS) * HEADS_PER_GROUP
GROUP_WIDTH = HEADS_PER_GROUP * HEAD_DIM
FOURIER_GROUP_DIM = 256
HALF = 64
FFT_N2 = 128
LANES = 128
ROW_TILE = 16
PERM_BLOCK = 256
VMEM_LIMIT = 56 * 1024 * 1024


def _params(n_axes, n_parallel):
    sem = ("parallel",) * n_parallel + ("arbitrary",) * (n_axes - n_parallel)
    return pltpu.CompilerParams(dimension_semantics=sem, vmem_limit_bytes=VMEM_LIMIT)


def _const_spec(shape, n_grid):
    zeros = (0,) * len(shape)
    imap = (lambda a, b: zeros) if n_grid == 2 else (lambda a, b, c: zeros)
    return pl.BlockSpec(shape, imap, pipeline_mode=pl.Buffered(1))


NORM_CHUNK = 64


def _norm_scratch(tm, d):
    return [pltpu.VMEM((tm, LANES), F32), pltpu.VMEM((2, d), F32)]


def _ada_norm_rows(x_ref, gain, shift, scale, h_ref, rs_ref, ab_ref, rows, load=None):
    d = h_ref.shape[-1]
    slabs = d // LANES
    if load is None:
        load = lambda c, cols: x_ref[pl.ds(pl.multiple_of(c * NORM_CHUNK, NORM_CHUNK), NORM_CHUNK), cols]
    ab_ref[0:1, :] = gain * (1.0 + scale)
    ab_ref[1:2, :] = shift

    def stats(c, carry):
        r0 = pl.multiple_of(c * NORM_CHUNK, NORM_CHUNK)
        acc = jnp.zeros((NORM_CHUNK, LANES), F32)
        for t in range(slabs):
            xt = load(c, slice(t * LANES, (t + 1) * LANES))
            acc = acc + xt * xt
        ms = jnp.sum(acc, axis=-1, keepdims=True) * (1.0 / d)
        rs_ref[pl.ds(r0, NORM_CHUNK), :] = jnp.broadcast_to(lax.rsqrt(ms + RMS_EPS), (NORM_CHUNK, LANES))
        return carry

    lax.fori_loop(0, rows // NORM_CHUNK, stats, 0, unroll=4)

    def apply(c, carry):
        r0 = pl.multiple_of(c * NORM_CHUNK, NORM_CHUNK)
        rs = rs_ref[pl.ds(r0, NORM_CHUNK), :]
        for t in range(slabs):
            cols = slice(t * LANES, (t + 1) * LANES)
            h = load(c, cols) * rs * ab_ref[0:1, cols] + ab_ref[1:2, cols]
            h_ref[pl.ds(r0, NORM_CHUNK), cols] = h.astype(h_ref.dtype)
        return carry

    lax.fori_loop(0, rows // NORM_CHUNK, apply, 0)


def _ada_norm_chunks(x_ref, gain, shift, scale, h_ref, row0, rows):
    d = h_ref.shape[-1]
    slabs = d // LANES
    a = gain * (1.0 + scale)
    for c in range(rows // NORM_CHUNK):
        r0 = pl.multiple_of(row0 + c * NORM_CHUNK, NORM_CHUNK)
        acc = jnp.zeros((NORM_CHUNK, LANES), F32)
        for t in range(slabs):
            xt = x_ref[pl.ds(r0, NORM_CHUNK), t * LANES:(t + 1) * LANES]
            acc = acc + xt * xt
        ms = jnp.sum(acc, axis=-1, keepdims=True) * (1.0 / d)
        rs = jnp.broadcast_to(lax.rsqrt(ms + RMS_EPS), (NORM_CHUNK, LANES))
        for t in range(slabs):
            cols = slice(t * LANES, (t + 1) * LANES)
            h = x_ref[pl.ds(r0, NORM_CHUNK), cols] * rs * a[:, cols] + shift[:, cols]
            h_ref[pl.ds(r0, NORM_CHUNK), cols] = h.astype(h_ref.dtype)


def _mod_kernel(c_ref, w_ref, b_ref, o_ref):
    c = c_ref[...]
    act = (c * jax.nn.sigmoid(c)).astype(BF16)
    w = w_ref[...].astype(BF16)
    o_ref[...] = jnp.dot(act, w, preferred_element_type=F32) + b_ref[...]


def _modulation(c_all, w_ada, b_ada):
    depth, d, n = w_ada.shape
    rows = c_all.shape[0]
    tn = 1024
    return pl.pallas_call(
        _mod_kernel,
        out_shape=jax.ShapeDtypeStruct((depth, rows, n), F32),
        grid=(depth, n // tn),
        in_specs=[
            pl.BlockSpec((rows, d), lambda l, j: (0, 0)),
            pl.BlockSpec((None, d, tn), lambda l, j: (l, 0, j)),
            pl.BlockSpec((None, 1, tn), lambda l, j: (l, 0, j)),
        ],
        out_specs=pl.BlockSpec((None, rows, tn), lambda l, j: (l, 0, j)),
        compiler_params=_params(2, 2),
        name="adaln_modulation",
    )(c_all, w_ada, b_ada.reshape(depth, 1, n))


def _mod_in(layer, which, d, n_grid):
    if n_grid == 2:
        imap = lambda b, i: (layer, b, which, 0, 0)
    else:
        imap = lambda b, i, j: (layer, b, which, 0, 0)
    return pl.BlockSpec((None, None, None, 1, d), imap)


def _row_in(layer, d, n_grid):
    if n_grid == 2:
        imap = lambda b, i: (layer, 0, 0)
    else:
        imap = lambda b, i, j: (layer, 0, 0)
    return pl.BlockSpec((None, 1, d), imap)


def _residue_perms():
    mats = []
    for _, dil in ATTN_WINDOWS[1:]:
        ub = PERM_BLOCK // dil
        p = np.zeros((PERM_BLOCK, PERM_BLOCK), np.float32)
        nat = np.arange(PERM_BLOCK)
        p[(nat % dil) * ub + nat // dil, nat] = 1.0
        mats.append(p)
    return np.stack(mats)


def _norm_proj_kernel(x_ref, g_ref, sh_ref, sc_ref, w_ref, p_ref, o_ref, h_ref, hp_ref, rs_ref, ab_ref, *,
                      tm, steps_per_group):
    j = pl.program_id(2)

    @pl.when(j == 0)
    def _():
        _ada_norm_rows(x_ref, g_ref[...], sh_ref[...], sc_ref[...], h_ref, rs_ref, ab_ref, tm)

    @pl.when((j >= steps_per_group) & (j % steps_per_group == 0))
    def _():
        for blk in range(tm // PERM_BLOCK):
            rows = slice(blk * PERM_BLOCK, (blk + 1) * PERM_BLOCK)
            hp_ref[rows, :] = jnp.dot(p_ref[...], h_ref[rows, :], preferred_element_type=F32).astype(hp_ref.dtype)

    @pl.when(j < steps_per_group)
    def _():
        o_ref[...] = jnp.dot(h_ref[...], w_ref[...], preferred_element_type=F32).astype(o_ref.dtype)

    @pl.when(j >= steps_per_group)
    def _():
        o_ref[...] = jnp.dot(hp_ref[...], w_ref[...], preferred_element_type=F32).astype(o_ref.dtype)


def _norm_proj(x, mod5, gains, w, perms, layer, w_layer, tm, tn):
    b, s, d = x.shape
    n = w.shape[-1]
    steps_per_group = (n // len(ATTN_WINDOWS)) // tn
    kern = functools.partial(_norm_proj_kernel, tm=tm, steps_per_group=steps_per_group)
    return pl.pallas_call(
        kern,
        out_shape=jax.ShapeDtypeStruct((b, s, n), BF16),
        grid=(b, s // tm, n // tn),
        in_specs=[
            pl.BlockSpec((None, tm, d), lambda bb, i, j: (bb, i, 0)),
            _row_in(layer, d, 3),
            _mod_in(layer, 0, d, 3),
            _mod_in(layer, 1, d, 3),
            pl.BlockSpec((None, d, tn), lambda bb, i, j: (w_layer, 0, j)),
            pl.BlockSpec((None, PERM_BLOCK, PERM_BLOCK),
                         lambda bb, i, j: (jnp.maximum(j // steps_per_group, 1) - 1, 0, 0)),
        ],
        out_specs=pl.BlockSpec((None, tm, tn), lambda bb, i, j: (bb, i, j)),
        scratch_shapes=[pltpu.VMEM((tm, d), BF16), pltpu.VMEM((tm, d), BF16)] + _norm_scratch(tm, d),
        compiler_params=_params(3, 2),
        name="norm_qkv_proj",
    )(x, gains, mod5, mod5, w, perms)


def _attn_kernel(q_ref, kp_ref, km_ref, kn_ref, vp_ref, vm_ref, vn_ref, pt_ref, o_ref, lse_ref,
                 kc_ref, vc_ref, op_ref, lp_ref, *, nblk, ub, n_u, dilation, slopes):
    i = pl.program_id(1)
    tq = nblk * ub
    width = q_ref.shape[-1]
    sub = 2 * HALF
    span = sub + 2 * HALF
    row = lax.broadcasted_iota(jnp.int32, (sub, span), 0)
    col = lax.broadcasted_iota(jnp.int32, (sub, span), 1)
    adu = jnp.abs(col - HALF - row)
    band = adu <= HALF
    dist = (adu * dilation).astype(F32)
    lane = lax.broadcasted_iota(jnp.int32, (sub, LANES), 1)
    scale = HEAD_DIM ** -0.5
    blocks_per_sub = sub // ub

    def one_class(r):
        def gather(dst, prev, main, nxt):
            dst[0:HALF, :] = prev[:, r].reshape(HALF, width)
            dst[HALF:HALF + tq, :] = main[:, r].reshape(tq, width)
            dst[HALF + tq:, :] = nxt[:, r].reshape(HALF, width)

        gather(kc_ref, kp_ref, km_ref, kn_ref)
        gather(vc_ref, vp_ref, vm_ref, vn_ref)
        for sb in range(tq // sub):
            u_key = i * tq + (sb * sub - HALF) + col
            valid = band & (u_key >= 0) & (u_key < n_u)
            lse_tile = jnp.zeros((sub, LANES), F32)
            rows = slice(sb * sub, (sb + 1) * sub)
            for h in range(HEADS_PER_GROUP):
                cs = slice(h * HEAD_DIM, (h + 1) * HEAD_DIM)
                q = q_ref[sb * blocks_per_sub:(sb + 1) * blocks_per_sub, r, :, cs].reshape(sub, HEAD_DIM)
                k = kc_ref[sb * sub:sb * sub + span, cs]
                v = vc_ref[sb * sub:sb * sub + span, cs]
                s = lax.dot_general(q, k, (((1,), (1,)), ((), ())), preferred_element_type=F32) * scale
                s = s - slopes[h] * dist
                s = jnp.where(valid, s, MASK_VALUE)
                m = jnp.max(s, axis=-1, keepdims=True)
                p = jnp.exp(s - m)
                l = jnp.sum(p, axis=-1, keepdims=True)
                o = jnp.dot(p.astype(BF16), v, preferred_element_type=F32) * (1.0 / l)
                if dilation == 1:
                    o_ref[rows, cs] = o.astype(o_ref.dtype)
                else:
                    op_ref[r, rows, cs] = o.astype(op_ref.dtype)
                lse_tile = jnp.where(lane == h, m + jnp.log(l), lse_tile)
            if dilation == 1:
                lse_ref[rows, :] = lse_tile
            else:
                lp_ref[r, rows, :] = lse_tile

    if dilation == 1:
        one_class(0)
        return

    def class_step(r, carry):
        one_class(r)
        return carry

    lax.fori_loop(0, dilation, class_step, 0)
    pt = pt_ref[...]
    for blk in range(nblk):
        urows = slice(blk * ub, (blk + 1) * ub)
        nat = slice(blk * PERM_BLOCK, (blk + 1) * PERM_BLOCK)
        ob = jnp.concatenate([op_ref[rr, urows, :] for rr in range(dilation)], axis=0)
        o_ref[nat, :] = jnp.dot(pt, ob, preferred_element_type=F32).astype(o_ref.dtype)
        lb = jnp.concatenate([lp_ref[rr, urows, :] for rr in range(dilation)], axis=0)
        hi = lb.astype(BF16)
        rest = lb - hi.astype(F32)
        mid = rest.astype(BF16)
        lo = (rest - mid.astype(F32)).astype(BF16)
        lse_ref[nat, :] = (jnp.dot(pt, hi, preferred_element_type=F32)
                           + jnp.dot(pt, mid, preferred_element_type=F32)
                           + jnp.dot(pt, lo, preferred_element_type=F32))


def _group_attention(qkv, group, perms_t):
    b, s, width = qkv.shape
    _, dilation = ATTN_WINDOWS[group]
    n_u = s // dilation
    ub = HALF if dilation == 1 else PERM_BLOCK // dilation
    rb = ub * dilation
    tq = min({1: 512, 4: 256, 16: 128}[dilation], n_u)
    nblk = tq // ub
    hb = HALF // ub
    n_slabs = width // GROUP_WIDTH
    view = qkv.reshape(b, s // rb, dilation, ub, width)
    n_halo = (s // rb) // hb
    slopes = tuple(float(np.exp2(np.float32(-8.0 * (group * HEADS_PER_GROUP + h + 1) / N_ATTN_HEADS)))
                   for h in range(HEADS_PER_GROUP))

    def main(t):
        return pl.BlockSpec((None, nblk, dilation, ub, GROUP_WIDTH),
                            lambda bb, i: (bb, i, 0, 0, group * 3 + t))

    def halo_p(t):
        return pl.BlockSpec((None, hb, dilation, ub, GROUP_WIDTH),
                            lambda bb, i: (bb, jnp.maximum(i * (nblk // hb) - 1, 0), 0, 0, group * 3 + t))

    def halo_n(t):
        return pl.BlockSpec((None, hb, dilation, ub, GROUP_WIDTH),
                            lambda bb, i: (bb, jnp.minimum((i + 1) * (nblk // hb), n_halo - 1), 0, 0,
                                           group * 3 + t))

    assert nblk % hb == 0 and width == n_slabs * GROUP_WIDTH
    pt = perms_t[max(group - 1, 0)]
    kern = functools.partial(_attn_kernel, nblk=nblk, ub=ub, n_u=n_u, dilation=dilation, slopes=slopes)
    scratch = [pltpu.VMEM((tq + 2 * HALF, GROUP_WIDTH), BF16), pltpu.VMEM((tq + 2 * HALF, GROUP_WIDTH), BF16),
               pltpu.VMEM((dilation, tq, GROUP_WIDTH), BF16), pltpu.VMEM((dilation, tq, LANES), F32)]
    rows = nblk * rb
    return pl.pallas_call(
        kern,
        out_shape=(jax.ShapeDtypeStruct((b, s, GROUP_WIDTH), BF16),
                   jax.ShapeDtypeStruct((b, s, LANES), F32)),
        grid=(b, s // rows),
        in_specs=[main(0), halo_p(1), main(1), halo_n(1), halo_p(2), main(2), halo_n(2),
                  _const_spec(pt.shape, 2)],
        out_specs=(pl.BlockSpec((None, rows, GROUP_WIDTH), lambda bb, i: (bb, i, 0)),
                   pl.BlockSpec((None, rows, LANES), lambda bb, i: (bb, i, 0))),
        scratch_shapes=scratch,
        compiler_params=_params(2, 2),
        name=f"dilated_attention_g{group}",
    )(view, view, view, view, view, view, view, pt)


def _merge_proj_kernel(o0_ref, o1_ref, o2_ref, l0_ref, l1_ref, l2_ref, x_ref, gate_ref, w_ref,
                       out_ref, mix_ref):
    l0, l1, l2 = l0_ref[...], l1_ref[...], l2_ref[...]
    m = jnp.maximum(jnp.maximum(l0, l1), l2)
    e0, e1, e2 = jnp.exp(l0 - m), jnp.exp(l1 - m), jnp.exp(l2 - m)
    inv = 1.0 / (e0 + e1 + e2)
    for g, (o_ref, e) in enumerate(((o0_ref, e0), (o1_ref, e1), (o2_ref, e2))):
        alpha = e * inv
        for h in range(HEADS_PER_GROUP):
            src = slice(h * HEAD_DIM, (h + 1) * HEAD_DIM)
            dst = slice((g * HEADS_PER_GROUP + h) * HEAD_DIM, (g * HEADS_PER_GROUP + h + 1) * HEAD_DIM)
            mix_ref[:, dst] = (alpha[:, h:h + 1] * o_ref[:, src].astype(F32)).astype(mix_ref.dtype)
    y = jnp.dot(mix_ref[...], w_ref[...], preferred_element_type=F32)
    out_ref[...] = x_ref[...] + gate_ref[...] * y


def _merge_proj(outs, lses, x, mod5, w_o, layer, w_layer, tm):
    b, s, d = x.shape
    width = w_o.shape[1]
    tok = lambda c: pl.BlockSpec((None, tm, c), lambda bb, i: (bb, i, 0))
    return pl.pallas_call(
        _merge_proj_kernel,
        out_shape=jax.ShapeDtypeStruct((b, s, d), F32),
        grid=(b, s // tm),
        in_specs=[tok(GROUP_WIDTH)] * 3 + [tok(LANES)] * 3 + [
            tok(d),
            _mod_in(layer, 2, d, 2),
            pl.BlockSpec((None, width, d), lambda bb, i: (w_layer, 0, 0)),
        ],
        out_specs=tok(d),
        scratch_shapes=[pltpu.VMEM((tm, width), BF16)],
        compiler_params=_params(2, 2),
        name="attn_merge_out_proj",
    )(*outs, *lses, x, mod5, w_o)


K1_GROUP = 8
FUSED_STAGE1_ROWS = 256


def _dft_tables(s):
    n2 = FFT_N2
    n1 = s // n2
    c = FOURIER_GROUP_DIM
    ang_c = 2.0 * np.pi * np.outer(np.arange(c), np.arange(c)) / c
    cs_chan = np.concatenate([np.cos(ang_c), np.sin(ang_c)], axis=1) / np.sqrt(c)
    a1 = 2.0 * np.pi * np.outer(np.arange(n1), np.arange(n1)) / n1
    c1, s1 = np.cos(a1), np.sin(a1)
    w1 = np.kron(np.block([[c1, s1], [-s1, c1]]), np.eye(ROW_TILE))
    s2 = (np.arange(n2 // ROW_TILE)[:, None, None] * ROW_TILE + np.arange(ROW_TILE)[None, None, :])
    th = 2.0 * np.pi * np.arange(n1)[None, :, None] * s2 / s
    th = th.reshape(n2 // ROW_TILE, n1 * ROW_TILE, 1)
    a2 = 2.0 * np.pi * np.outer(np.arange(n2), np.arange(n2)) / n2
    cs2 = np.concatenate([np.cos(a2), np.sin(a2)], axis=1)
    return (jnp.asarray(cs_chan, BF16), jnp.asarray(w1, BF16),
            jnp.asarray(np.cos(th), F32), jnp.asarray(np.sin(th), F32), jnp.asarray(cs2, BF16))


def _chan_dft_kernel(x_ref, g_ref, sh_ref, sc_ref, cs_ref, zr_ref, zi_ref, h_ref, rs_ref, ab_ref, *, tm):
    _ada_norm_rows(x_ref, g_ref[...], sh_ref[...], sc_ref[...], h_ref, rs_ref, ab_ref, tm)
    c = FOURIER_GROUP_DIM
    for g in range(x_ref.shape[-1] // c):
        cols = slice(g * c, (g + 1) * c)
        r = jnp.dot(h_ref[:, cols], cs_ref[...], preferred_element_type=F32)
        zr_ref[:, cols] = r[:, :c].astype(zr_ref.dtype)
        zi_ref[:, cols] = (-r[:, c:]).astype(zi_ref.dtype)


def _chan_dft(x, mod5, gains, cs_chan, layer, tm):
    b, s, d = x.shape
    tok = pl.BlockSpec((None, tm, d), lambda bb, i: (bb, i, 0))
    kern = functools.partial(_chan_dft_kernel, tm=tm)
    return pl.pallas_call(
        kern,
        out_shape=(jax.ShapeDtypeStruct((b, s, d), BF16),) * 2,
        grid=(b, s // tm),
        in_specs=[tok, _row_in(layer, d, 2), _mod_in(layer, 0, d, 2), _mod_in(layer, 1, d, 2),
                  _const_spec(cs_chan.shape, 2)],
        out_specs=(tok, tok),
        scratch_shapes=[pltpu.VMEM((tm, d), BF16)] + _norm_scratch(tm, d),
        compiler_params=_params(2, 2),
        name="fourier_channel_dft",
    )(x, gains, mod5, mod5, cs_chan)


def _pos_dft1_kernel(zr_ref, zi_ref, w_ref, tc_ref, ts_ref, ur_ref, ui_ref, *, rows):
    tc = zr_ref.shape[-1]
    z = jnp.concatenate([zr_ref[...].reshape(rows, tc), zi_ref[...].reshape(rows, tc)], axis=0)
    t = jnp.dot(w_ref[...], z, preferred_element_type=F32)
    tr, ti = t[:rows], t[rows:]
    c, sn = tc_ref[...], ts_ref[...]
    ur_ref[...] = (tr * c + ti * sn).astype(ur_ref.dtype).reshape(ur_ref.shape)
    ui_ref[...] = (ti * c - tr * sn).astype(ui_ref.dtype).reshape(ui_ref.shape)


def _pos_dft1(zr, zi, w1, tw_c, tw_s):
    b, s, d = zr.shape
    n2 = FFT_N2
    n1 = s // n2
    rows = n1 * ROW_TILE
    tc = min(d, (1024 * 1024) // rows)
    view = lambda a: a.reshape(b, n1, n2 // ROW_TILE, ROW_TILE, d)
    blk = pl.BlockSpec((None, n1, None, ROW_TILE, tc), lambda bb, j, c: (bb, 0, j, 0, c))
    tw = pl.BlockSpec((None, rows, 1), lambda bb, j, c: (j, 0, 0))
    kern = functools.partial(_pos_dft1_kernel, rows=rows)
    ur, ui = pl.pallas_call(
        kern,
        out_shape=(jax.ShapeDtypeStruct((b, n1, n2 // ROW_TILE, ROW_TILE, d), BF16),) * 2,
        grid=(b, n2 // ROW_TILE, d // tc),
        in_specs=[blk, blk, _const_spec(w1.shape, 3), tw, tw],
        out_specs=(blk, blk),
        compiler_params=_params(3, 3),
        name="fourier_pos_dft_stage1",
    )(view(zr), view(zi), w1, tw_c, tw_s)
    return ur.reshape(b, s, d), ui.reshape(b, s, d)


def _chan_pos1_kernel(x_ref, g_ref, sh_ref, sc_ref, cs_ref, w_ref, tc_ref, ts_ref, ur_ref, ui_ref,
                      h_ref, z_ref, rs_ref, ab_ref, *, rows):
    groups = NORM_CHUNK // ROW_TILE
    load = lambda c, cols: x_ref[pl.ds(pl.multiple_of(c * groups, groups), groups), :, cols].reshape(
        NORM_CHUNK, LANES)
    _ada_norm_rows(x_ref, g_ref[...], sh_ref[...], sc_ref[...], h_ref, rs_ref, ab_ref, rows, load)
    c = FOURIER_GROUP_DIM
    for g in range(h_ref.shape[-1] // c):
        cols = slice(g * c, (g + 1) * c)
        r = jnp.dot(h_ref[:, cols], cs_ref[...], preferred_element_type=F32)
        z_ref[0:rows, cols] = r[:, :c].astype(z_ref.dtype)
        z_ref[rows:, cols] = (-r[:, c:]).astype(z_ref.dtype)
    t = jnp.dot(w_ref[...], z_ref[...], preferred_element_type=F32)
    tr, ti = t[:rows], t[rows:]
    cw, sw = tc_ref[...], ts_ref[...]
    ur_ref[...] = (tr * cw + ti * sw).astype(ur_ref.dtype).reshape(ur_ref.shape)
    ui_ref[...] = (ti * cw - tr * sw).astype(ui_ref.dtype).reshape(ui_ref.shape)


def _chan_pos1(x, mod5, gains, cs_chan, w1, tw_c, tw_s, layer):
    b, s, d = x.shape
    n2 = FFT_N2
    n1 = s // n2
    rows = n1 * ROW_TILE
    shape5 = (b, n1, n2 // ROW_TILE, ROW_TILE, d)
    blk = pl.BlockSpec((None, n1, None, ROW_TILE, d), lambda bb, j: (bb, 0, j, 0, 0))
    tw = pl.BlockSpec((None, rows, 1), lambda bb, j: (j, 0, 0))
    kern = functools.partial(_chan_pos1_kernel, rows=rows)
    ur, ui = pl.pallas_call(
        kern,
        out_shape=(jax.ShapeDtypeStruct(shape5, BF16),) * 2,
        grid=(b, n2 // ROW_TILE),
        in_specs=[blk, _row_in(layer, d, 2), _mod_in(layer, 0, d, 2), _mod_in(layer, 1, d, 2),
                  _const_spec(cs_chan.shape, 2), _const_spec(w1.shape, 2), tw, tw],
        out_specs=(blk, blk),
        scratch_shapes=[pltpu.VMEM((rows, d), BF16), pltpu.VMEM((2 * rows, d), BF16)] + _norm_scratch(rows, d),
        compiler_params=_params(2, 2),
        name="fourier_chan_pos_dft_stage1",
    )(x.reshape(shape5), gains, mod5, mod5, cs_chan, w1, tw_c, tw_s)
    return ur.reshape(b, s, d), ui.reshape(b, s, d)


def _pos_dft2_proj_kernel(ur_ref, ui_ref, cs2_ref, w_ref, b_ref, x_ref, gate_ref, out_ref, ys_ref, yb_ref, *,
                          inv_norm):
    n2 = cs2_ref.shape[0]
    d = ur_ref.shape[-1]
    slabs = d // LANES

    @pl.when(pl.program_id(2) == 0)
    def _():
        for k in range(K1_GROUP):
            rows = slice(k * n2, (k + 1) * n2)
            u = jnp.concatenate([ur_ref[rows, :], ui_ref[rows, :]], axis=0)
            y = jnp.dot(cs2_ref[...], u, preferred_element_type=F32) * inv_norm
            for sl in range(slabs):
                ys_ref[sl, pl.ds(k, n2, stride=K1_GROUP), :] = y[:, sl * LANES:(sl + 1) * LANES]
        for sl in range(slabs):
            yb_ref[:, sl * LANES:(sl + 1) * LANES] = ys_ref[sl].astype(yb_ref.dtype)

    proj = jnp.dot(yb_ref[...], w_ref[...], preferred_element_type=F32) + b_ref[...]
    out = x_ref[...].reshape(proj.shape) + gate_ref[...] * proj
    out_ref[...] = out.reshape(out_ref.shape)


def _pos_dft2_proj(ur, ui, cs2, x, mod5, w_f, b_f, layer, w_layer):
    b, s, d = x.shape
    n2 = FFT_N2
    n1 = s // n2
    tn = 512
    rows = K1_GROUP * n2
    u_blk = pl.BlockSpec((None, rows, d), lambda bb, a, c: (bb, a, 0))
    x_view = x.reshape(b, n2, n1 // K1_GROUP, K1_GROUP, d)
    x_blk = pl.BlockSpec((None, n2, None, K1_GROUP, tn), lambda bb, a, c: (bb, 0, a, 0, c))
    kern = functools.partial(_pos_dft2_proj_kernel, inv_norm=float(1.0 / np.sqrt(s)))
    out = pl.pallas_call(
        kern,
        out_shape=jax.ShapeDtypeStruct(x_view.shape, F32),
        grid=(b, n1 // K1_GROUP, d // tn),
        in_specs=[u_blk, u_blk, _const_spec(cs2.shape, 3),
                  pl.BlockSpec((None, d, tn), lambda bb, a, c: (w_layer, 0, c)),
                  pl.BlockSpec((None, 1, tn), lambda bb, a, c: (w_layer, 0, c)),
                  x_blk,
                  pl.BlockSpec((None, None, None, 1, tn), lambda bb, a, c: (layer, bb, 2, 0, c))],
        out_specs=x_blk,
        scratch_shapes=[pltpu.VMEM((d // LANES, rows, LANES), F32), pltpu.VMEM((rows, d), BF16)],
        compiler_params=_params(3, 2),
        name="fourier_pos_dft_stage2_proj",
    )(ur, ui, cs2, w_f, b_f, x_view, mod5)
    return out.reshape(b, s, d)


def _mlp_kernel(x_ref, xn_ref, g_ref, sh_ref, sc_ref, shn_ref, scn_ref, gate_ref, w1_ref, b1_ref, w2_ref, b2_ref,
                fg_ref, out_ref, h0_ref, h1_ref, rs_ref, ab_ref, *, tm, final_norm):
    i, j = pl.program_id(1), pl.program_id(2)
    n_j = pl.num_programs(2)
    tile = pl.program_id(0) * pl.num_programs(1) + i
    gain = g_ref[...]

    @pl.when((tile == 0) & (j == 0))
    def _():
        _ada_norm_rows(x_ref, gain, sh_ref[...], sc_ref[...], h0_ref, rs_ref, ab_ref, tm)

    @pl.when(j == 0)
    def _():
        out_ref[...] = jnp.zeros_like(out_ref)

    rows_per_step = tm // n_j

    def chunk_step(h_cur, h_next):
        _ada_norm_chunks(xn_ref, gain, shn_ref[...], scn_ref[...], h_next, j * rows_per_step, rows_per_step)
        u = jnp.dot(h_cur[...], w1_ref[...], preferred_element_type=F32) + b1_ref[...]
        u = jnp.maximum(u, 0.0)
        out_ref[...] += jnp.dot((u * u).astype(BF16), w2_ref[...], preferred_element_type=F32)

    @pl.when(tile % 2 == 0)
    def _():
        chunk_step(h0_ref, h1_ref)

    @pl.when(tile % 2 == 1)
    def _():
        chunk_step(h1_ref, h0_ref)

    @pl.when(j == n_j - 1)
    def _():
        y = x_ref[...] + gate_ref[...] * (out_ref[...] + b2_ref[...])
        if final_norm:
            ms = jnp.mean(y * y, axis=-1, keepdims=True)
            y = y * lax.rsqrt(ms + RMS_EPS) * fg_ref[...]
        out_ref[...] = y


def _mlp(x, mod5, gains, w1, b1, w2, b2, final_g, layer, tm, tf, final_norm):
    b, s, d = x.shape
    dff = w1.shape[-1]
    n_i, n_j = s // tm, dff // tf
    assert (tm // n_j) % NORM_CHUNK == 0
    kern = functools.partial(_mlp_kernel, tm=tm, final_norm=final_norm)

    def next_tile(bb, i):
        t1 = jnp.minimum(bb * n_i + i + 1, b * n_i - 1)
        return t1 // n_i, t1 % n_i

    tok = pl.BlockSpec((None, tm, d), lambda bb, i, j: (bb, i, 0))
    tok_next = pl.BlockSpec((None, tm, d), lambda bb, i, j: (*next_tile(bb, i), 0))
    mod_next = lambda which: pl.BlockSpec((None, None, None, 1, d),
                                          lambda bb, i, j: (layer, next_tile(bb, i)[0], which, 0, 0))
    return pl.pallas_call(
        kern,
        out_shape=jax.ShapeDtypeStruct((b, s, d), F32),
        grid=(b, n_i, n_j),
        in_specs=[
            tok,
            tok_next,
            _row_in(layer, d, 3),
            _mod_in(layer, 3, d, 3),
            _mod_in(layer, 4, d, 3),
            mod_next(3),
            mod_next(4),
            _mod_in(layer, 5, d, 3),
            pl.BlockSpec((None, d, tf), lambda bb, i, j: (layer, 0, j)),
            pl.BlockSpec((None, 1, tf), lambda bb, i, j: (layer, 0, j)),
            pl.BlockSpec((None, tf, d), lambda bb, i, j: (layer, j, 0)),
            _row_in(layer, d, 3),
            pl.BlockSpec((1, d), lambda bb, i, j: (0, 0)),
        ],
        out_specs=tok,
        scratch_shapes=[pltpu.VMEM((tm, d), BF16), pltpu.VMEM((tm, d), BF16)] + _norm_scratch(tm, d),
        compiler_params=_params(3, 0),
        name="sqrelu_mlp",
    )(x, x, gains, mod5, mod5, mod5, mod5, mod5, w1, b1, w2, b2, final_g)


def _trunk(x, mod5, p):
    depth = p["w1"].shape[0]
    s = x.shape[1]
    cs_chan, w1c, tw_c, tw_s, cs2 = _dft_tables(s)
    for i in range(depth):
        sub = i // 2
        if i % 2 == 0:
            qkv = _norm_proj(x, mod5, p["norm1_g"], p["w_qkv"], p["perms"], i, sub, tm=1024, tn=768)
            outs, lses = zip(*[_group_attention(qkv, g, p["perms_t"]) for g in range(len(ATTN_WINDOWS))])
            x = _merge_proj(outs, lses, x, mod5, p["w_o"], i, sub, tm=512)
        else:
            if (s // FFT_N2) * ROW_TILE <= FUSED_STAGE1_ROWS:
                ur, ui = _chan_pos1(x, mod5, p["norm1_g"], cs_chan, w1c, tw_c, tw_s, i)
            else:
                zr, zi = _chan_dft(x, mod5, p["norm1_g"], cs_chan, i, tm=512)
                ur, ui = _pos_dft1(zr, zi, w1c, tw_c, tw_s)
            x = _pos_dft2_proj(ur, ui, cs2, x, mod5, p["w_f"], p["b_f"], i, sub)
        x = _mlp(x, mod5, p["norm2_g"], p["w1"], p["b1"], p["w2"], p["b2"], p["final_g"], i,
                 tm=512, tf=1024, final_norm=(i == depth - 1))
    return x


def kernel(x_prompt, x_sample, c_prompt, c_sample, w_ada, b_ada, norm1_g, norm2_g, w_qkv, w_o, w_f, b_f,
           w1, b1, w2, b2, final_g):
    depth, d, _ = w_ada.shape
    n_p, n_s = c_prompt.shape[0], c_sample.shape[0]
    rows = -(-(n_p + n_s) // ROW_TILE) * ROW_TILE
    c_all = jnp.concatenate([c_prompt, c_sample, jnp.zeros((rows - n_p - n_s, d), F32)], axis=0)
    mod = _modulation(c_all, w_ada, b_ada)
    mod_p = mod[:, :n_p].reshape(depth, n_p, N_MOD, 1, d)
    mod_s = mod[:, n_p:n_p + n_s].reshape(depth, n_s, N_MOD, 1, d)

    row3 = lambda a: a.reshape(a.shape[0], 1, a.shape[-1])
    perms = _residue_perms()
    params = {
        "norm1_g": row3(norm1_g), "norm2_g": row3(norm2_g),
        "w_qkv": w_qkv.astype(BF16), "w_o": w_o.astype(BF16),
        "w_f": w_f.astype(BF16), "b_f": row3(b_f),
        "w1": w1.astype(BF16), "b1": row3(b1), "w2": w2.astype(BF16), "b2": row3(b2),
        "final_g": final_g.reshape(1, d),
        "perms": jnp.asarray(perms, BF16),
        "perms_t": jnp.asarray(np.transpose(perms, (0, 2, 1)), BF16),
    }
    return (_trunk(x_prompt, mod_p, params), _trunk(x_sample, mod_s, params))
```

```python
import functools

import numpy as np
import jax
import jax.numpy as jnp
from jax import lax
from jax.experimental import pallas as pl
from jax.experimental.pallas import tpu as pltpu

F32 = jnp.float32
BF16 = jnp.bfloat16

N_MOD = 6
RMS_EPS = 1e-6
MASK_VALUE = -1e30
ATTN_WINDOWS = ((128, 1), (512, 4), (2048, 16))
HEADS_PER_GROUP = 6
HEAD_DIM = 128
N_ATTN_HEADS = len(ATTN_WINDOWS) * HEADS_PER_GROUP
GROUP_WIDTH = HEADS_PER_GROUP * HEAD_DIM
FOURIER_GROUP_DIM = 256
HALF = 64
FFT_N2 = 128
LANES = 128
ROW_TILE = 16
PERM_BLOCK = 256
VMEM_LIMIT = 56 * 1024 * 1024


def _params(n_axes, n_parallel):
    sem = ("parallel",) * n_parallel + ("arbitrary",) * (n_axes - n_parallel)
    return pltpu.CompilerParams(dimension_semantics=sem, vmem_limit_bytes=VMEM_LIMIT)


def _const_spec(shape, n_grid):
    zeros = (0,) * len(shape)
    imap = (lambda a, b: zeros) if n_grid == 2 else (lambda a, b, c: zeros)
    return pl.BlockSpec(shape, imap, pipeline_mode=pl.Buffered(1))


NORM_CHUNK = 64


def _norm_scratch(tm, d):
    return [pltpu.VMEM((tm, LANES), F32), pltpu.VMEM((2, d), F32)]


def _ada_norm_rows(x_ref, gain, shift, scale, h_ref, rs_ref, ab_ref, rows, load=None):
    d = h_ref.shape[-1]
    slabs = d // LANES
    if load is None:
        load = lambda c, cols: x_ref[pl.ds(pl.multiple_of(c * NORM_CHUNK, NORM_CHUNK), NORM_CHUNK), cols]
    ab_ref[0:1, :] = gain * (1.0 + scale)
    ab_ref[1:2, :] = shift

    def stats(c, carry):
        r0 = pl.multiple_of(c * NORM_CHUNK, NORM_CHUNK)
        acc = jnp.zeros((NORM_CHUNK, LANES), F32)
        for t in range(slabs):
            xt = load(c, slice(t * LANES, (t + 1) * LANES))
            acc = acc + xt * xt
        ms = jnp.sum(acc, axis=-1, keepdims=True) * (1.0 / d)
        rs_ref[pl.ds(r0, NORM_CHUNK), :] = jnp.broadcast_to(lax.rsqrt(ms + RMS_EPS), (NORM_CHUNK, LANES))
        return carry

    lax.fori_loop(0, rows // NORM_CHUNK, stats, 0, unroll=4)

    def apply(c, carry):
        r0 = pl.multiple_of(c * NORM_CHUNK, NORM_CHUNK)
        rs = rs_ref[pl.ds(r0, NORM_CHUNK), :]
        for t in range(slabs):
            cols = slice(t * LANES, (t + 1) * LANES)
            h = load(c, cols) * rs * ab_ref[0:1, cols] + ab_ref[1:2, cols]
            h_ref[pl.ds(r0, NORM_CHUNK), cols] = h.astype(h_ref.dtype)
        return carry

    lax.fori_loop(0, rows // NORM_CHUNK, apply, 0)


def _ada_norm_chunks(x_ref, gain, shift, scale, h_ref, row0, rows):
    d = h_ref.shape[-1]
    slabs = d // LANES
    a = gain * (1.0 + scale)
    for c in range(rows // NORM_CHUNK):
        r0 = pl.multiple_of(row0 + c * NORM_CHUNK, NORM_CHUNK)
        acc = jnp.zeros((NORM_CHUNK, LANES), F32)
        for t in range(slabs):
            xt = x_ref[pl.ds(r0, NORM_CHUNK), t * LANES:(t + 1) * LANES]
            acc = acc + xt * xt
        ms = jnp.sum(acc, axis=-1, keepdims=True) * (1.0 / d)
        rs = jnp.broadcast_to(lax.rsqrt(ms + RMS_EPS), (NORM_CHUNK, LANES))
        for t in range(slabs):
            cols = slice(t * LANES, (t + 1) * LANES)
            h = x_ref[pl.ds(r0, NORM_CHUNK), cols] * rs * a[:, cols] + shift[:, cols]
            h_ref[pl.ds(r0, NORM_CHUNK), cols] = h.astype(h_ref.dtype)


def _mod_kernel(c_ref, w_ref, b_ref, o_ref):
    c = c_ref[...]
    act = (c * jax.nn.sigmoid(c)).astype(BF16)
    w = w_ref[...].astype(BF16)
    o_ref[...] = jnp.dot(act, w, preferred_element_type=F32) + b_ref[...]


def _modulation(c_all, w_ada, b_ada):
    depth, d, n = w_ada.shape
    rows = c_all.shape[0]
    tn = 1024
    return pl.pallas_call(
        _mod_kernel,
        out_shape=jax.ShapeDtypeStruct((depth, rows, n), F32),
        grid=(depth, n // tn),
        in_specs=[
            pl.BlockSpec((rows, d), lambda l, j: (0, 0)),
            pl.BlockSpec((None, d, tn), lambda l, j: (l, 0, j)),
            pl.BlockSpec((None, 1, tn), lambda l, j: (l, 0, j)),
        ],
        out_specs=pl.BlockSpec((None, rows, tn), lambda l, j: (l, 0, j)),
        compiler_params=_params(2, 2),
        name="adaln_modulation",
    )(c_all, w_ada, b_ada.reshape(depth, 1, n))


def _mod_in(layer, which, d, n_grid):
    if n_grid == 2:
        imap = lambda b, i: (layer, b, which, 0, 0)
    else:
        imap = lambda b, i, j: (layer, b, which, 0, 0)
    return pl.BlockSpec((None, None, None, 1, d), imap)


def _row_in(layer, d, n_grid):
    if n_grid == 2:
        imap = lambda b, i: (layer, 0, 0)
    else:
        imap = lambda b, i, j: (layer, 0, 0)
    return pl.BlockSpec((None, 1, d), imap)


def _residue_perms():
    mats = []
    for _, dil in ATTN_WINDOWS[1:]:
        ub = PERM_BLOCK // dil
        p = np.zeros((PERM_BLOCK, PERM_BLOCK), np.float32)
        nat = np.arange(PERM_BLOCK)
        p[(nat % dil) * ub + nat // dil, nat] = 1.0
        mats.append(p)
    return np.stack(mats)


def _norm_proj_kernel(xa_ref, xb_ref, g_ref, sh_ref, sc_ref, shn_ref, scn_ref, w_ref, p_ref, o_ref,
                      h0_ref, h1_ref, hp_ref, rs_ref, ab_ref, *, tm, steps_per_group):
    i, j = pl.program_id(1), pl.program_id(2)
    tile = pl.program_id(0) * pl.num_programs(1) + i
    half = tm // 2
    spg = steps_per_group
    gain = g_ref[...]
    rows_per_step = -(-half // (spg * NORM_CHUNK)) * NORM_CHUNK

    @pl.when((tile == 0) & (j == 0))
    def _():
        _ada_norm_rows(xa_ref, gain, sh_ref[...], sc_ref[...], h0_ref.at[0:half], rs_ref, ab_ref, half)
        _ada_norm_rows(xb_ref, gain, sh_ref[...], sc_ref[...], h0_ref.at[half:tm], rs_ref, ab_ref, half)

    def norm_ahead(x_ref, h_rows, step):
        row0 = jnp.minimum(step * rows_per_step, half - rows_per_step)
        _ada_norm_chunks(x_ref, gain, shn_ref[...], scn_ref[...], h_rows, row0, rows_per_step)

    def project(lhs_ref):
        o_ref[...] = jnp.dot(lhs_ref[...], w_ref[...], preferred_element_type=F32).astype(o_ref.dtype)

    def column_step(h_cur, h_next):
        @pl.when((j >= spg) & (j % spg == 0))
        def _():
            for blk in range(tm // PERM_BLOCK):
                rows = slice(blk * PERM_BLOCK, (blk + 1) * PERM_BLOCK)
                hp_ref[rows, :] = jnp.dot(p_ref[...], h_cur[rows, :],
                                          preferred_element_type=F32).astype(hp_ref.dtype)

        @pl.when(j < spg)
        def _():
            project(h_cur)

        @pl.when((j >= spg) & (j < 2 * spg))
        def _():
            norm_ahead(xa_ref, h_next.at[0:half], j - spg)
            project(hp_ref)

        @pl.when(j >= 2 * spg)
        def _():
            norm_ahead(xb_ref, h_next.at[half:tm], j - 2 * spg)
            project(hp_ref)

    @pl.when(tile % 2 == 0)
    def _():
        column_step(h0_ref, h1_ref)

    @pl.when(tile % 2 == 1)
    def _():
        column_step(h1_ref, h0_ref)


def _norm_proj(x, mod5, gains, w, perms, layer, w_layer, tm, tn):
    b, s, d = x.shape
    n = w.shape[-1]
    n_i = s // tm
    spg = (n // len(ATTN_WINDOWS)) // tn
    assert n // tn == len(ATTN_WINDOWS) * spg
    kern = functools.partial(_norm_proj_kernel, tm=tm, steps_per_group=spg)

    def tile_ahead(bb, i, ahead):
        t1 = jnp.minimum(bb * n_i + i + ahead, b * n_i - 1)
        return t1 // n_i, t1 % n_i

    def x_half(which, first_step):
        def imap(bb, i, j):
            b1, i1 = tile_ahead(bb, i, (j >= first_step).astype(jnp.int32))
            return b1, 2 * i1 + which, 0
        return pl.BlockSpec((None, tm // 2, d), imap)

    mod_next = lambda which: pl.BlockSpec((None, None, None, 1, d),
                                          lambda bb, i, j: (layer, tile_ahead(bb, i, 1)[0], which, 0, 0))
    return pl.pallas_call(
        kern,
        out_shape=jax.ShapeDtypeStruct((b, s, n), BF16),
        grid=(b, n_i, n // tn),
        in_specs=[
            x_half(0, spg),
            x_half(1, 2 * spg),
            _row_in(layer, d, 3),
            _mod_in(layer, 0, d, 3),
            _mod_in(layer, 1, d, 3),
            mod_next(0),
            mod_next(1),
            pl.BlockSpec((None, d, tn), lambda bb, i, j: (w_layer, 0, j)),
            pl.BlockSpec((None, PERM_BLOCK, PERM_BLOCK),
                         lambda bb, i, j: (jnp.maximum(j // spg, 1) - 1, 0, 0)),
        ],
        out_specs=pl.BlockSpec((None, tm, tn), lambda bb, i, j: (bb, i, j)),
        scratch_shapes=[pltpu.VMEM((tm, d), BF16)] * 3 + _norm_scratch(tm // 2, d),
        compiler_params=_params(3, 0),
        name="norm_qkv_proj",
    )(x, x, gains, mod5, mod5, mod5, mod5, w, perms)


def _attn_kernel(q_ref, kp_ref, km_ref, kn_ref, vp_ref, vm_ref, vn_ref, pt_ref, o_ref, lse_ref,
                 kc_ref, vc_ref, op_ref, lp_ref, *, nblk, ub, n_u, dilation, slopes):
    i = pl.program_id(1)
    tq = nblk * ub
    width = q_ref.shape[-1]
    sub = 2 * HALF
    span = sub + 2 * HALF
    row = lax.broadcasted_iota(jnp.int32, (sub, span), 0)
    col = lax.broadcasted_iota(jnp.int32, (sub, span), 1)
    adu = jnp.abs(col - HALF - row)
    band = adu <= HALF
    dist = (adu * dilation).astype(F32)
    lane = lax.broadcasted_iota(jnp.int32, (sub, LANES), 1)
    scale = HEAD_DIM ** -0.5
    blocks_per_sub = sub // ub

    def one_class(r):
        def gather(dst, prev, main, nxt):
            dst[0:HALF, :] = prev[:, r].reshape(HALF, width)
            dst[HALF:HALF + tq, :] = main[:, r].reshape(tq, width)
            dst[HALF + tq:, :] = nxt[:, r].reshape(HALF, width)

        gather(kc_ref, kp_ref, km_ref, kn_ref)
        gather(vc_ref, vp_ref, vm_ref, vn_ref)
        for sb in range(tq // sub):
            u_key = i * tq + (sb * sub - HALF) + col
            valid = band & (u_key >= 0) & (u_key < n_u)
            lse_tile = jnp.zeros((sub, LANES), F32)
            rows = slice(sb * sub, (sb + 1) * sub)
            for h in range(HEADS_PER_GROUP):
                cs = slice(h * HEAD_DIM, (h + 1) * HEAD_DIM)
                q = q_ref[sb * blocks_per_sub:(sb + 1) * blocks_per_sub, r, :, cs].reshape(sub, HEAD_DIM)
                k = kc_ref[sb * sub:sb * sub + span, cs]
                v = vc_ref[sb * sub:sb * sub + span, cs]
                s = lax.dot_general(q, k, (((1,), (1,)), ((), ())), preferred_element_type=F32) * scale
                s = s - slopes[h] * dist
                s = jnp.where(valid, s, MASK_VALUE)
                m = jnp.max(s, axis=-1, keepdims=True)
                p = jnp.exp(s - m)
                l = jnp.sum(p, axis=-1, keepdims=True)
                o = jnp.dot(p.astype(BF16), v, preferred_element_type=F32) * (1.0 / l)
                if dilation == 1:
                    o_ref[rows, cs] = o.astype(o_ref.dtype)
                else:
                    op_ref[r, rows, cs] = o.astype(op_ref.dtype)
                lse_tile = jnp.where(lane == h, m + jnp.log(l), lse_tile)
            if dilation == 1:
                lse_ref[rows, :] = lse_tile
            else:
                lp_ref[r, rows, :] = lse_tile

    if dilation == 1:
        one_class(0)
        return

    def class_step(r, carry):
        one_class(r)
        return carry

    lax.fori_loop(0, dilation, class_step, 0)
    pt = pt_ref[...]
    for blk in range(nblk):
        urows = slice(blk * ub, (blk + 1) * ub)
        nat = slice(blk * PERM_BLOCK, (blk + 1) * PERM_BLOCK)
        ob = jnp.concatenate([op_ref[rr, urows, :] for rr in range(dilation)], axis=0)
        o_ref[nat, :] = jnp.dot(pt, ob, preferred_element_type=F32).astype(o_ref.dtype)
        lb = jnp.concatenate([lp_ref[rr, urows, :] for rr in range(dilation)], axis=0)
        hi = lb.astype(BF16)
        rest = lb - hi.astype(F32)
        mid = rest.astype(BF16)
        lo = (rest - mid.astype(F32)).astype(BF16)
        lse_ref[nat, :] = (jnp.dot(pt, hi, preferred_element_type=F32)
                           + jnp.dot(pt, mid, preferred_element_type=F32)
                           + jnp.dot(pt, lo, preferred_element_type=F32))


def _group_attention(qkv, group, perms_t):
    b, s, width = qkv.shape
    _, dilation = ATTN_WINDOWS[group]
    n_u = s // dilation
    ub = HALF if dilation == 1 else PERM_BLOCK // dilation
    rb = ub * dilation
    tq = min({1: 512, 4: 256, 16: 128}[dilation], n_u)
    nblk = tq // ub
    hb = HALF // ub
    n_slabs = width // GROUP_WIDTH
    view = qkv.reshape(b, s // rb, dilation, ub, width)
    n_halo = (s // rb) // hb
    slopes = tuple(float(np.exp2(np.float32(-8.0 * (group * HEADS_PER_GROUP + h + 1) / N_ATTN_HEADS)))
                   for h in range(HEADS_PER_GROUP))

    def main(t):
        return pl.BlockSpec((None, nblk, dilation, ub, GROUP_WIDTH),
                            lambda bb, i: (bb, i, 0, 0, group * 3 + t))

    def halo_p(t):
        return pl.BlockSpec((None, hb, dilation, ub, GROUP_WIDTH),
                            lambda bb, i: (bb, jnp.maximum(i * (nblk // hb) - 1, 0), 0, 0, group * 3 + t))

    def halo_n(t):
        return pl.BlockSpec((None, hb, dilation, ub, GROUP_WIDTH),
                            lambda bb, i: (bb, jnp.minimum((i + 1) * (nblk // hb), n_halo - 1), 0, 0,
                                           group * 3 + t))

    assert nblk % hb == 0 and width == n_slabs * GROUP_WIDTH
    pt = perms_t[max(group - 1, 0)]
    kern = functools.partial(_attn_kernel, nblk=nblk, ub=ub, n_u=n_u, dilation=dilation, slopes=slopes)
    scratch = [pltpu.VMEM((tq + 2 * HALF, GROUP_WIDTH), BF16), pltpu.VMEM((tq + 2 * HALF, GROUP_WIDTH), BF16),
               pltpu.VMEM((dilation, tq, GROUP_WIDTH), BF16), pltpu.VMEM((dilation, tq, LANES), F32)]
    rows = nblk * rb
    return pl.pallas_call(
        kern,
        out_shape=(jax.ShapeDtypeStruct((b, s, GROUP_WIDTH), BF16),
                   jax.ShapeDtypeStruct((b, s, LANES), F32)),
        grid=(b, s // rows),
        in_specs=[main(0), halo_p(1), main(1), halo_n(1), halo_p(2), main(2), halo_n(2),
                  _const_spec(pt.shape, 2)],
        out_specs=(pl.BlockSpec((None, rows, GROUP_WIDTH), lambda bb, i: (bb, i, 0)),
                   pl.BlockSpec((None, rows, LANES), lambda bb, i: (bb, i, 0))),
        scratch_shapes=scratch,
        compiler_params=_params(2, 2),
        name=f"dilated_attention_g{group}",
    )(view, view, view, view, view, view, view, pt)


def _merge_proj_kernel(o0_ref, o1_ref, o2_ref, l0_ref, l1_ref, l2_ref, x_ref, gate_ref, w_ref,
                       out_ref, mix_ref):
    l0, l1, l2 = l0_ref[...], l1_ref[...], l2_ref[...]
    m = jnp.maximum(jnp.maximum(l0, l1), l2)
    e0, e1, e2 = jnp.exp(l0 - m), jnp.exp(l1 - m), jnp.exp(l2 - m)
    inv = 1.0 / (e0 + e1 + e2)
    for g, (o_ref, e) in enumerate(((o0_ref, e0), (o1_ref, e1), (o2_ref, e2))):
        alpha = e * inv
        for h in range(HEADS_PER_GROUP):
            src = slice(h * HEAD_DIM, (h + 1) * HEAD_DIM)
            dst = slice((g * HEADS_PER_GROUP + h) * HEAD_DIM, (g * HEADS_PER_GROUP + h + 1) * HEAD_DIM)
            mix_ref[:, dst] = (alpha[:, h:h + 1] * o_ref[:, src].astype(F32)).astype(mix_ref.dtype)
    y = jnp.dot(mix_ref[...], w_ref[...], preferred_element_type=F32)
    out_ref[...] = x_ref[...] + gate_ref[...] * y


def _merge_proj(outs, lses, x, mod5, w_o, layer, w_layer, tm):
    b, s, d = x.shape
    width = w_o.shape[1]
    tok = lambda c: pl.BlockSpec((None, tm, c), lambda bb, i: (bb, i, 0))
    return pl.pallas_call(
        _merge_proj_kernel,
        out_shape=jax.ShapeDtypeStruct((b, s, d), F32),
        grid=(b, s // tm),
        in_specs=[tok(GROUP_WIDTH)] * 3 + [tok(LANES)] * 3 + [
            tok(d),
            _mod_in(layer, 2, d, 2),
            pl.BlockSpec((None, width, d), lambda bb, i: (w_layer, 0, 0)),
        ],
        out_specs=tok(d),
        scratch_shapes=[pltpu.VMEM((tm, width), BF16)],
        compiler_params=_params(2, 2),
        name="attn_merge_out_proj",
    )(*outs, *lses, x, mod5, w_o)


K1_GROUP = 8
FUSED_STAGE1_ROWS = 256
WF_COLS = 512


def _dft_tables(s):
    n2 = FFT_N2
    n1 = s // n2
    c = FOURIER_GROUP_DIM
    ang_c = 2.0 * np.pi * np.outer(np.arange(c), np.arange(c)) / c
    cs_chan = np.concatenate([np.cos(ang_c), np.sin(ang_c)], axis=1) / np.sqrt(c)
    a1 = 2.0 * np.pi * np.outer(np.arange(n1), np.arange(n1)) / n1
    c1, s1 = np.cos(a1), np.sin(a1)
    w1 = np.kron(np.block([[c1, s1], [-s1, c1]]), np.eye(ROW_TILE))
    s2 = (np.arange(n2 // ROW_TILE)[:, None, None] * ROW_TILE + np.arange(ROW_TILE)[None, None, :])
    th = 2.0 * np.pi * np.arange(n1)[None, :, None] * s2 / s
    th = th.reshape(n2 // ROW_TILE, n1 * ROW_TILE, 1)
    a2 = 2.0 * np.pi * np.outer(np.arange(n2), np.arange(n2)) / n2
    cs2 = np.concatenate([np.cos(a2), np.sin(a2)], axis=1)
    return (jnp.asarray(cs_chan, BF16), jnp.asarray(w1, BF16),
            jnp.asarray(np.cos(th), F32), jnp.asarray(np.sin(th), F32), jnp.asarray(cs2, BF16))


def _chan_dft_kernel(x_ref, g_ref, sh_ref, sc_ref, cs_ref, zr_ref, zi_ref, h_ref, rs_ref, ab_ref, *, tm):
    _ada_norm_rows(x_ref, g_ref[...], sh_ref[...], sc_ref[...], h_ref, rs_ref, ab_ref, tm)
    c = FOURIER_GROUP_DIM
    for g in range(x_ref.shape[-1] // c):
        cols = slice(g * c, (g + 1) * c)
        r = jnp.dot(h_ref[:, cols], cs_ref[...], preferred_element_type=F32)
        zr_ref[:, cols] = r[:, :c].astype(zr_ref.dtype)
        zi_ref[:, cols] = (-r[:, c:]).astype(zi_ref.dtype)


def _chan_dft(x, mod5, gains, cs_chan, layer, tm):
    b, s, d = x.shape
    tok = pl.BlockSpec((None, tm, d), lambda bb, i: (bb, i, 0))
    kern = functools.partial(_chan_dft_kernel, tm=tm)
    return pl.pallas_call(
        kern,
        out_shape=(jax.ShapeDtypeStruct((b, s, d), BF16),) * 2,
        grid=(b, s // tm),
        in_specs=[tok, _row_in(layer, d, 2), _mod_in(layer, 0, d, 2), _mod_in(layer, 1, d, 2),
                  _const_spec(cs_chan.shape, 2)],
        out_specs=(tok, tok),
        scratch_shapes=[pltpu.VMEM((tm, d), BF16)] + _norm_scratch(tm, d),
        compiler_params=_params(2, 2),
        name="fourier_channel_dft",
    )(x, gains, mod5, mod5, cs_chan)


def _pos_dft1_kernel(zr_ref, zi_ref, w_ref, tc_ref, ts_ref, ur_ref, ui_ref, *, rows):
    tc = zr_ref.shape[-1]
    z = jnp.concatenate([zr_ref[...].reshape(rows, tc), zi_ref[...].reshape(rows, tc)], axis=0)
    t = jnp.dot(w_ref[...], z, preferred_element_type=F32)
    tr, ti = t[:rows], t[rows:]
    c, sn = tc_ref[...], ts_ref[...]
    ur_ref[...] = (tr * c + ti * sn).astype(ur_ref.dtype).reshape(ur_ref.shape)
    ui_ref[...] = (ti * c - tr * sn).astype(ui_ref.dtype).reshape(ui_ref.shape)


def _pos_dft1(zr, zi, w1, tw_c, tw_s):
    b, s, d = zr.shape
    n2 = FFT_N2
    n1 = s // n2
    rows = n1 * ROW_TILE
    tc = min(d, (1024 * 1024) // rows)
    view = lambda a: a.reshape(b, n1, n2 // ROW_TILE, ROW_TILE, d)
    blk = pl.BlockSpec((None, n1, None, ROW_TILE, tc), lambda bb, j, c: (bb, 0, j, 0, c))
    tw = pl.BlockSpec((None, rows, 1), lambda bb, j, c: (j, 0, 0))
    kern = functools.partial(_pos_dft1_kernel, rows=rows)
    ur, ui = pl.pallas_call(
        kern,
        out_shape=(jax.ShapeDtypeStruct((b, n1, n2 // ROW_TILE, ROW_TILE, d), BF16),) * 2,
        grid=(b, n2 // ROW_TILE, d // tc),
        in_specs=[blk, blk, _const_spec(w1.shape, 3), tw, tw],
        out_specs=(blk, blk),
        compiler_params=_params(3, 3),
        name="fourier_pos_dft_stage1",
    )(view(zr), view(zi), w1, tw_c, tw_s)
    return ur.reshape(b, s, d), ui.reshape(b, s, d)


def _chan_pos1_kernel(x_ref, g_ref, sh_ref, sc_ref, cs_ref, w_ref, tc_ref, ts_ref, ur_ref, ui_ref,
                      h_ref, z_ref, rs_ref, ab_ref, *, rows):
    groups = NORM_CHUNK // ROW_TILE
    load = lambda c, cols: x_ref[pl.ds(pl.multiple_of(c * groups, groups), groups), :, cols].reshape(
        NORM_CHUNK, LANES)
    _ada_norm_rows(x_ref, g_ref[...], sh_ref[...], sc_ref[...], h_ref, rs_ref, ab_ref, rows, load)
    c = FOURIER_GROUP_DIM
    for g in range(h_ref.shape[-1] // c):
        cols = slice(g * c, (g + 1) * c)
        r = jnp.dot(h_ref[:, cols], cs_ref[...], preferred_element_type=F32)
        z_ref[0:rows, cols] = r[:, :c].astype(z_ref.dtype)
        z_ref[rows:, cols] = (-r[:, c:]).astype(z_ref.dtype)
    t = jnp.dot(w_ref[...], z_ref[...], preferred_element_type=F32)
    tr, ti = t[:rows], t[rows:]
    cw, sw = tc_ref[...], ts_ref[...]
    ur_ref[...] = (tr * cw + ti * sw).astype(ur_ref.dtype).reshape(ur_ref.shape)
    ui_ref[...] = (ti * cw - tr * sw).astype(ui_ref.dtype).reshape(ui_ref.shape)


def _chan_pos1(x, mod5, gains, cs_chan, w1, tw_c, tw_s, layer):
    b, s, d = x.shape
    n2 = FFT_N2
    n1 = s // n2
    rows = n1 * ROW_TILE
    shape5 = (b, n1, n2 // ROW_TILE, ROW_TILE, d)
    blk = pl.BlockSpec((None, n1, None, ROW_TILE, d), lambda bb, j: (bb, 0, j, 0, 0))
    tw = pl.BlockSpec((None, rows, 1), lambda bb, j: (j, 0, 0))
    kern = functools.partial(_chan_pos1_kernel, rows=rows)
    ur, ui = pl.pallas_call(
        kern,
        out_shape=(jax.ShapeDtypeStruct(shape5, BF16),) * 2,
        grid=(b, n2 // ROW_TILE),
        in_specs=[blk, _row_in(layer, d, 2), _mod_in(layer, 0, d, 2), _mod_in(layer, 1, d, 2),
                  _const_spec(cs_chan.shape, 2), _const_spec(w1.shape, 2), tw, tw],
        out_specs=(blk, blk),
        scratch_shapes=[pltpu.VMEM((rows, d), BF16), pltpu.VMEM((2 * rows, d), BF16)] + _norm_scratch(rows, d),
        compiler_params=_params(2, 2),
        name="fourier_chan_pos_dft_stage1",
    )(x.reshape(shape5), gains, mod5, mod5, cs_chan, w1, tw_c, tw_s)
    return ur.reshape(b, s, d), ui.reshape(b, s, d)


def _pos_dft2_proj_kernel(ur_ref, ui_ref, cs2_ref, w_ref, b_ref, x_ref, gate_ref, out_ref, ys_ref, yb_ref, *,
                          inv_norm):
    n2 = cs2_ref.shape[0]
    d = ur_ref.shape[-1]
    slabs = d // LANES

    @pl.when(pl.program_id(2) == 0)
    def _():
        for k in range(K1_GROUP):
            rows = slice(k * n2, (k + 1) * n2)
            u = jnp.concatenate([ur_ref[rows, :], ui_ref[rows, :]], axis=0)
            y = jnp.dot(cs2_ref[...], u, preferred_element_type=F32) * inv_norm
            for sl in range(slabs):
                ys_ref[sl, pl.ds(k, n2, stride=K1_GROUP), :] = y[:, sl * LANES:(sl + 1) * LANES]
        for sl in range(slabs):
            yb_ref[:, sl * LANES:(sl + 1) * LANES] = ys_ref[sl].astype(yb_ref.dtype)

    proj = jnp.dot(yb_ref[...], w_ref[pl.program_id(2)], preferred_element_type=F32) + b_ref[...]
    out = x_ref[...].reshape(proj.shape) + gate_ref[...] * proj
    out_ref[...] = out.reshape(out_ref.shape)


def _pos_dft2_proj(ur, ui, cs2, x, mod5, w_f, b_f, layer, w_layer):
    b, s, d = x.shape
    n2 = FFT_N2
    n1 = s // n2
    tn = w_f.shape[-1]
    rows = K1_GROUP * n2
    u_blk = pl.BlockSpec((None, rows, d), lambda bb, a, c: (bb, a, 0))
    x_view = x.reshape(b, n2, n1 // K1_GROUP, K1_GROUP, d)
    x_blk = pl.BlockSpec((None, n2, None, K1_GROUP, tn), lambda bb, a, c: (bb, 0, a, 0, c))
    kern = functools.partial(_pos_dft2_proj_kernel, inv_norm=float(1.0 / np.sqrt(s)))
    out = pl.pallas_call(
        kern,
        out_shape=jax.ShapeDtypeStruct(x_view.shape, F32),
        grid=(b, n1 // K1_GROUP, d // tn),
        in_specs=[u_blk, u_blk, _const_spec(cs2.shape, 3),
                  pl.BlockSpec((None, d // tn, d, tn), lambda bb, a, c: (w_layer, 0, 0, 0),
                               pipeline_mode=pl.Buffered(1)),
                  pl.BlockSpec((None, 1, tn), lambda bb, a, c: (w_layer, 0, c)),
                  x_blk,
                  pl.BlockSpec((None, None, None, 1, tn), lambda bb, a, c: (layer, bb, 2, 0, c))],
        out_specs=x_blk,
        scratch_shapes=[pltpu.VMEM((d // LANES, rows, LANES), F32), pltpu.VMEM((rows, d), BF16)],
        compiler_params=_params(3, 2),
        name="fourier_pos_dft_stage2_proj",
    )(ur, ui, cs2, w_f, b_f, x_view, mod5)
    return out.reshape(b, s, d)


def _mlp_kernel(x_ref, xn_ref, g_ref, sh_ref, sc_ref, shn_ref, scn_ref, gate_ref, w1_ref, b1_ref, w2_ref, b2_ref,
                fg_ref, out_ref, h0_ref, h1_ref, rs_ref, ab_ref, *, tm, final_norm):
    i, j = pl.program_id(1), pl.program_id(2)
    n_j = pl.num_programs(2)
    tile = pl.program_id(0) * pl.num_programs(1) + i
    gain = g_ref[...]

    @pl.when((tile == 0) & (j == 0))
    def _():
        _ada_norm_rows(x_ref, gain, sh_ref[...], sc_ref[...], h0_ref, rs_ref, ab_ref, tm)

    @pl.when(j == 0)
    def _():
        out_ref[...] = jnp.zeros_like(out_ref)

    rows_per_step = 2 * tm // n_j
    row0 = jnp.clip(j - 1, 0, tm // rows_per_step - 1) * rows_per_step

    def chunk_step(h_cur, h_next):
        _ada_norm_chunks(xn_ref, gain, shn_ref[...], scn_ref[...], h_next, row0, rows_per_step)
        u = jnp.dot(h_cur[...], w1_ref[...], preferred_element_type=F32) + b1_ref[...]
        u = jnp.maximum(u, 0.0)
        out_ref[...] += jnp.dot((u * u).astype(BF16), w2_ref[...], preferred_element_type=F32)

    @pl.when(tile % 2 == 0)
    def _():
        chunk_step(h0_ref, h1_ref)

    @pl.when(tile % 2 == 1)
    def _():
        chunk_step(h1_ref, h0_ref)

    @pl.when(j == n_j - 1)
    def _():
        y = x_ref[...] + gate_ref[...] * (out_ref[...] + b2_ref[...])
        if final_norm:
            ms = jnp.mean(y * y, axis=-1, keepdims=True)
            y = y * lax.rsqrt(ms + RMS_EPS) * fg_ref[...]
        out_ref[...] = y


def _mlp(x, mod5, gains, w1, b1, w2, b2, final_g, layer, tm, tf, final_norm):
    b, s, d = x.shape
    dff = w1.shape[-1]
    n_i, n_j = s // tm, dff // tf
    assert (2 * tm // n_j) % NORM_CHUNK == 0 and n_j >= 4
    kern = functools.partial(_mlp_kernel, tm=tm, final_norm=final_norm)

    def next_tile(bb, i, ahead=1):
        t1 = jnp.minimum(bb * n_i + i + ahead, b * n_i - 1)
        return t1 // n_i, t1 % n_i

    tok = pl.BlockSpec((None, tm, d), lambda bb, i, j: (bb, i, 0))
    tok_next = pl.BlockSpec((None, tm, d), lambda bb, i, j: (*next_tile(bb, i, jnp.minimum(j, 1)), 0))
    mod_next = lambda which: pl.BlockSpec((None, None, None, 1, d),
                                          lambda bb, i, j: (layer, next_tile(bb, i)[0], which, 0, 0))
    return pl.pallas_call(
        kern,
        out_shape=jax.ShapeDtypeStruct((b, s, d), F32),
        grid=(b, n_i, n_j),
        in_specs=[
            tok,
            tok_next,
            _row_in(layer, d, 3),
            _mod_in(layer, 3, d, 3),
            _mod_in(layer, 4, d, 3),
            mod_next(3),
            mod_next(4),
            _mod_in(layer, 5, d, 3),
            pl.BlockSpec((None, d, tf), lambda bb, i, j: (layer, 0, j)),
            pl.BlockSpec((None, 1, tf), lambda bb, i, j: (layer, 0, j)),
            pl.BlockSpec((None, tf, d), lambda bb, i, j: (layer, j, 0)),
            _row_in(layer, d, 3),
            pl.BlockSpec((1, d), lambda bb, i, j: (0, 0)),
        ],
        out_specs=tok,
        scratch_shapes=[pltpu.VMEM((tm, d), BF16), pltpu.VMEM((tm, d), BF16)] + _norm_scratch(tm, d),
        compiler_params=_params(3, 0),
        name="sqrelu_mlp",
    )(x, x, gains, mod5, mod5, mod5, mod5, mod5, w1, b1, w2, b2, final_g)


def _trunk(x, mod5, p):
    depth = p["w1"].shape[0]
    s = x.shape[1]
    cs_chan, w1c, tw_c, tw_s, cs2 = _dft_tables(s)
    for i in range(depth):
        sub = i // 2
        if i % 2 == 0:
            qkv = _norm_proj(x, mod5, p["norm1_g"], p["w_qkv"], p["perms"], i, sub, tm=1024, tn=768)
            outs, lses = zip(*[_group_attention(qkv, g, p["perms_t"]) for g in range(len(ATTN_WINDOWS))])
            x = _merge_proj(outs, lses, x, mod5, p["w_o"], i, sub, tm=512)
        else:
            if (s // FFT_N2) * ROW_TILE <= FUSED_STAGE1_ROWS:
                ur, ui = _chan_pos1(x, mod5, p["norm1_g"], cs_chan, w1c, tw_c, tw_s, i)
            else:
                zr, zi = _chan_dft(x, mod5, p["norm1_g"], cs_chan, i, tm=512)
                ur, ui = _pos_dft1(zr, zi, w1c, tw_c, tw_s)
            x = _pos_dft2_proj(ur, ui, cs2, x, mod5, p["w_f"], p["b_f"], i, sub)
        x = _mlp(x, mod5, p["norm2_g"], p["w1"], p["b1"], p["w2"], p["b2"], p["final_g"], i,
                 tm=512, tf=1024, final_norm=(i == depth - 1))
    return x


def kernel(x_prompt, x_sample, c_prompt, c_sample, w_ada, b_ada, norm1_g, norm2_g, w_qkv, w_o, w_f, b_f,
           w1, b1, w2, b2, final_g):
    depth, d, _ = w_ada.shape
    n_p, n_s = c_prompt.shape[0], c_sample.shape[0]
    rows = -(-(n_p + n_s) // ROW_TILE) * ROW_TILE
    c_all = jnp.concatenate([c_prompt, c_sample, jnp.zeros((rows - n_p - n_s, d), F32)], axis=0)
    mod = _modulation(c_all, w_ada, b_ada)
    mod_p = mod[:, :n_p].reshape(depth, n_p, N_MOD, 1, d)
    mod_s = mod[:, n_p:n_p + n_s].reshape(depth, n_s, N_MOD, 1, d)

    row3 = lambda a: a.reshape(a.shape[0], 1, a.shape[-1])
    perms = _residue_perms()
    params = {
        "norm1_g": row3(norm1_g), "norm2_g": row3(norm2_g),
        "w_qkv": w_qkv.astype(BF16), "w_o": w_o.astype(BF16),
        "w_f": w_f.astype(BF16).reshape(w_f.shape[0], d, d // WF_COLS, WF_COLS).transpose(0, 2, 1, 3),
        "b_f": row3(b_f),
        "w1": w1.astype(BF16), "b1": row3(b1), "w2": w2.astype(BF16), "b2": row3(b2),
        "final_g": final_g.reshape(1, d),
        "perms": jnp.asarray(perms, BF16),
        "perms_t": jnp.asarray(np.transpose(perms, (0, 2, 1)), BF16),
    }
    return (_trunk(x_prompt, mod_p, params), _trunk(x_sample, mod_s, params))
```

```python
import functools

import numpy as np
import jax
import jax.numpy as jnp
from jax import lax
from jax.experimental import pallas as pl
from jax.experimental.pallas import tpu as pltpu

F32 = jnp.float32
BF16 = jnp.bfloat16

N_MOD = 6
RMS_EPS = 1e-6
MASK_VALUE = -1e30
ATTN_WINDOWS = ((128, 1), (512, 4), (2048, 16))
HEADS_PER_GROUP = 6
HEAD_DIM = 128
N_ATTN_HEADS = len(ATTN_WINDOWS) * HEADS_PER_GROUP
GROUP_WIDTH = HEADS_PER_GROUP * HEAD_DIM
FOURIER_GROUP_DIM = 256
HALF = 64
FFT_N2 = 128
LANES = 128
ROW_TILE = 16
PERM_BLOCK = 256
VMEM_LIMIT = 56 * 1024 * 1024


def _params(n_axes, n_parallel):
    sem = ("parallel",) * n_parallel + ("arbitrary",) * (n_axes - n_parallel)
    return pltpu.CompilerParams(dimension_semantics=sem, vmem_limit_bytes=VMEM_LIMIT)


def _const_spec(shape, n_grid):
    zeros = (0,) * len(shape)
    imap = (lambda a, b: zeros) if n_grid == 2 else (lambda a, b, c: zeros)
    return pl.BlockSpec(shape, imap, pipeline_mode=pl.Buffered(1))


NORM_CHUNK = 64


def _norm_scratch(tm, d):
    return [pltpu.VMEM((tm, LANES), F32), pltpu.VMEM((2, d), F32)]


def _ada_norm_rows(x_ref, gain, shift, scale, h_ref, rs_ref, ab_ref, rows, load=None):
    d = h_ref.shape[-1]
    slabs = d // LANES
    if load is None:
        load = lambda c, cols: x_ref[pl.ds(pl.multiple_of(c * NORM_CHUNK, NORM_CHUNK), NORM_CHUNK), cols]
    ab_ref[0:1, :] = gain * (1.0 + scale)
    ab_ref[1:2, :] = shift

    def stats(c, carry):
        r0 = pl.multiple_of(c * NORM_CHUNK, NORM_CHUNK)
        acc = jnp.zeros((NORM_CHUNK, LANES), F32)
        for t in range(slabs):
            xt = load(c, slice(t * LANES, (t + 1) * LANES))
            acc = acc + xt * xt
        ms = jnp.sum(acc, axis=-1, keepdims=True) * (1.0 / d)
        rs_ref[pl.ds(r0, NORM_CHUNK), :] = jnp.broadcast_to(lax.rsqrt(ms + RMS_EPS), (NORM_CHUNK, LANES))
        return carry

    lax.fori_loop(0, rows // NORM_CHUNK, stats, 0, unroll=4)

    def apply(c, carry):
        r0 = pl.multiple_of(c * NORM_CHUNK, NORM_CHUNK)
        rs = rs_ref[pl.ds(r0, NORM_CHUNK), :]
        for t in range(slabs):
            cols = slice(t * LANES, (t + 1) * LANES)
            h = load(c, cols) * rs * ab_ref[0:1, cols] + ab_ref[1:2, cols]
            h_ref[pl.ds(r0, NORM_CHUNK), cols] = h.astype(h_ref.dtype)
        return carry

    lax.fori_loop(0, rows // NORM_CHUNK, apply, 0)


def _ada_norm_chunks(x_ref, gain, shift, scale, h_ref, row0, rows):
    d = h_ref.shape[-1]
    slabs = d // LANES
    a = gain * (1.0 + scale)
    for c in range(rows // NORM_CHUNK):
        r0 = pl.multiple_of(row0 + c * NORM_CHUNK, NORM_CHUNK)
        acc = jnp.zeros((NORM_CHUNK, LANES), F32)
        for t in range(slabs):
            xt = x_ref[pl.ds(r0, NORM_CHUNK), t * LANES:(t + 1) * LANES]
            acc = acc + xt * xt
        ms = jnp.sum(acc, axis=-1, keepdims=True) * (1.0 / d)
        rs = jnp.broadcast_to(lax.rsqrt(ms + RMS_EPS), (NORM_CHUNK, LANES))
        for t in range(slabs):
            cols = slice(t * LANES, (t + 1) * LANES)
            h = x_ref[pl.ds(r0, NORM_CHUNK), cols] * rs * a[:, cols] + shift[:, cols]
            h_ref[pl.ds(r0, NORM_CHUNK), cols] = h.astype(h_ref.dtype)


def _mod_kernel(c_ref, w_ref, b_ref, o_ref):
    c = c_ref[...]
    act = (c * jax.nn.sigmoid(c)).astype(BF16)
    w = w_ref[...].astype(BF16)
    o_ref[...] = jnp.dot(act, w, preferred_element_type=F32) + b_ref[...]


def _modulation(c_all, w_ada, b_ada):
    depth, d, n = w_ada.shape
    rows = c_all.shape[0]
    tn = 1024
    return pl.pallas_call(
        _mod_kernel,
        out_shape=jax.ShapeDtypeStruct((depth, rows, n), F32),
        grid=(depth, n // tn),
        in_specs=[
            pl.BlockSpec((rows, d), lambda l, j: (0, 0)),
            pl.BlockSpec((None, d, tn), lambda l, j: (l, 0, j)),
            pl.BlockSpec((None, 1, tn), lambda l, j: (l, 0, j)),
        ],
        out_specs=pl.BlockSpec((None, rows, tn), lambda l, j: (l, 0, j)),
        compiler_params=_params(2, 2),
        name="adaln_modulation",
    )(c_all, w_ada, b_ada.reshape(depth, 1, n))


def _mod_in(layer, which, d, n_grid):
    if n_grid == 2:
        imap = lambda b, i: (layer, b, which, 0, 0)
    else:
        imap = lambda b, i, j: (layer, b, which, 0, 0)
    return pl.BlockSpec((None, None, None, 1, d), imap)


def _row_in(layer, d, n_grid):
    if n_grid == 2:
        imap = lambda b, i: (layer, 0, 0)
    else:
        imap = lambda b, i, j: (layer, 0, 0)
    return pl.BlockSpec((None, 1, d), imap)


def _residue_perms():
    mats = []
    for _, dil in ATTN_WINDOWS[1:]:
        ub = PERM_BLOCK // dil
        p = np.zeros((PERM_BLOCK, PERM_BLOCK), np.float32)
        nat = np.arange(PERM_BLOCK)
        p[(nat % dil) * ub + nat // dil, nat] = 1.0
        mats.append(p)
    return np.stack(mats)


def _norm_proj_kernel(xa_ref, xb_ref, g_ref, sh_ref, sc_ref, shn_ref, scn_ref, w_ref, p_ref, o_ref,
                      h0_ref, h1_ref, hp_ref, rs_ref, ab_ref, *, tm, steps_per_group):
    i, j = pl.program_id(1), pl.program_id(2)
    tile = pl.program_id(0) * pl.num_programs(1) + i
    half = tm // 2
    spg = steps_per_group
    gain = g_ref[...]
    rows_per_step = -(-half // (spg * NORM_CHUNK)) * NORM_CHUNK

    @pl.when((tile == 0) & (j == 0))
    def _():
        _ada_norm_rows(xa_ref, gain, sh_ref[...], sc_ref[...], h0_ref.at[0:half], rs_ref, ab_ref, half)
        _ada_norm_rows(xb_ref, gain, sh_ref[...], sc_ref[...], h0_ref.at[half:tm], rs_ref, ab_ref, half)

    def norm_ahead(x_ref, h_rows, step):
        row0 = jnp.minimum(step * rows_per_step, half - rows_per_step)
        _ada_norm_chunks(x_ref, gain, shn_ref[...], scn_ref[...], h_rows, row0, rows_per_step)

    def project(lhs_ref):
        o_ref[...] = jnp.dot(lhs_ref[...], w_ref[...], preferred_element_type=F32).astype(o_ref.dtype)

    def column_step(h_cur, h_next):
        @pl.when((j >= spg) & (j % spg == 0))
        def _():
            for blk in range(tm // PERM_BLOCK):
                rows = slice(blk * PERM_BLOCK, (blk + 1) * PERM_BLOCK)
                hp_ref[rows, :] = jnp.dot(p_ref[...], h_cur[rows, :],
                                          preferred_element_type=F32).astype(hp_ref.dtype)

        @pl.when(j < spg)
        def _():
            project(h_cur)

        @pl.when((j >= spg) & (j < 2 * spg))
        def _():
            norm_ahead(xa_ref, h_next.at[0:half], j - spg)
            project(hp_ref)

        @pl.when(j >= 2 * spg)
        def _():
            norm_ahead(xb_ref, h_next.at[half:tm], j - 2 * spg)
            project(hp_ref)

    @pl.when(tile % 2 == 0)
    def _():
        column_step(h0_ref, h1_ref)

    @pl.when(tile % 2 == 1)
    def _():
        column_step(h1_ref, h0_ref)


def _norm_proj(x, mod5, gains, w, perms, layer, w_layer, tm, tn):
    b, s, d = x.shape
    n = w.shape[-1]
    n_i = s // tm
    spg = (n // len(ATTN_WINDOWS)) // tn
    assert n // tn == len(ATTN_WINDOWS) * spg
    kern = functools.partial(_norm_proj_kernel, tm=tm, steps_per_group=spg)

    def tile_ahead(bb, i, ahead):
        t1 = jnp.minimum(bb * n_i + i + ahead, b * n_i - 1)
        return t1 // n_i, t1 % n_i

    def x_half(which, first_step):
        def imap(bb, i, j):
            b1, i1 = tile_ahead(bb, i, (j >= first_step).astype(jnp.int32))
            return b1, 2 * i1 + which, 0
        return pl.BlockSpec((None, tm // 2, d), imap)

    mod_next = lambda which: pl.BlockSpec((None, None, None, 1, d),
                                          lambda bb, i, j: (layer, tile_ahead(bb, i, 1)[0], which, 0, 0))
    return pl.pallas_call(
        kern,
        out_shape=jax.ShapeDtypeStruct((b, s, n), BF16),
        grid=(b, n_i, n // tn),
        in_specs=[
            x_half(0, spg),
            x_half(1, 2 * spg),
            _row_in(layer, d, 3),
            _mod_in(layer, 0, d, 3),
            _mod_in(layer, 1, d, 3),
            mod_next(0),
            mod_next(1),
            pl.BlockSpec((None, d, tn), lambda bb, i, j: (w_layer, 0, j)),
            pl.BlockSpec((None, PERM_BLOCK, PERM_BLOCK),
                         lambda bb, i, j: (jnp.maximum(j // spg, 1) - 1, 0, 0)),
        ],
        out_specs=pl.BlockSpec((None, tm, tn), lambda bb, i, j: (bb, i, j)),
        scratch_shapes=[pltpu.VMEM((tm, d), BF16)] * 3 + _norm_scratch(tm // 2, d),
        compiler_params=_params(3, 0),
        name="norm_qkv_proj",
    )(x, x, gains, mod5, mod5, mod5, mod5, w, perms)


def _attn_kernel(q_ref, kp_ref, km_ref, kn_ref, vp_ref, vm_ref, vn_ref, pt_ref, o_ref, lse_ref,
                 kc_ref, vc_ref, op_ref, lp_ref, *, nblk, ub, n_u, dilation, slopes):
    i = pl.program_id(1)
    tq = nblk * ub
    width = q_ref.shape[-1]
    sub = 2 * HALF
    span = sub + 2 * HALF
    row = lax.broadcasted_iota(jnp.int32, (sub, span), 0)
    col = lax.broadcasted_iota(jnp.int32, (sub, span), 1)
    adu = jnp.abs(col - HALF - row)
    band = adu <= HALF
    dist = (adu * dilation).astype(F32)
    lane = lax.broadcasted_iota(jnp.int32, (sub, LANES), 1)
    scale = HEAD_DIM ** -0.5
    blocks_per_sub = sub // ub

    def one_class(r):
        def gather(dst, prev, main, nxt):
            dst[0:HALF, :] = prev[:, r].reshape(HALF, width)
            dst[HALF:HALF + tq, :] = main[:, r].reshape(tq, width)
            dst[HALF + tq:, :] = nxt[:, r].reshape(HALF, width)

        gather(kc_ref, kp_ref, km_ref, kn_ref)
        gather(vc_ref, vp_ref, vm_ref, vn_ref)
        for sb in range(tq // sub):
            u_key = i * tq + (sb * sub - HALF) + col
            valid = band & (u_key >= 0) & (u_key < n_u)
            lse_tile = jnp.zeros((sub, LANES), F32)
            rows = slice(sb * sub, (sb + 1) * sub)
            for h in range(HEADS_PER_GROUP):
                cs = slice(h * HEAD_DIM, (h + 1) * HEAD_DIM)
                q = q_ref[sb * blocks_per_sub:(sb + 1) * blocks_per_sub, r, :, cs].reshape(sub, HEAD_DIM)
                k = kc_ref[sb * sub:sb * sub + span, cs]
                v = vc_ref[sb * sub:sb * sub + span, cs]
                s = lax.dot_general(q, k, (((1,), (1,)), ((), ())), preferred_element_type=F32) * scale
                s = s - slopes[h] * dist
                s = jnp.where(valid, s, MASK_VALUE)
                m = jnp.max(s, axis=-1, keepdims=True)
                p = jnp.exp(s - m)
                l = jnp.sum(p, axis=-1, keepdims=True)
                o = jnp.dot(p.astype(BF16), v, preferred_element_type=F32) * (1.0 / l)
                if dilation == 1:
                    o_ref[rows, cs] = o.astype(o_ref.dtype)
                else:
                    op_ref[r, rows, cs] = o.astype(op_ref.dtype)
                lse_tile = jnp.where(lane == h, m + jnp.log(l), lse_tile)
            if dilation == 1:
                lse_ref[rows, :] = lse_tile
            else:
                lp_ref[r, rows, :] = lse_tile

    if dilation == 1:
        one_class(0)
        return

    def class_step(r, carry):
        one_class(r)
        return carry

    lax.fori_loop(0, dilation, class_step, 0)
    pt = pt_ref[...]
    for blk in range(nblk):
        urows = slice(blk * ub, (blk + 1) * ub)
        nat = slice(blk * PERM_BLOCK, (blk + 1) * PERM_BLOCK)
        ob = jnp.concatenate([op_ref[rr, urows, :] for rr in range(dilation)], axis=0)
        o_ref[nat, :] = jnp.dot(pt, ob, preferred_element_type=F32).astype(o_ref.dtype)
        lb = jnp.concatenate([lp_ref[rr, urows, :] for rr in range(dilation)], axis=0)
        hi = lb.astype(BF16)
        rest = lb - hi.astype(F32)
        mid = rest.astype(BF16)
        lo = (rest - mid.astype(F32)).astype(BF16)
        lse_ref[nat, :] = (jnp.dot(pt, hi, preferred_element_type=F32)
                           + jnp.dot(pt, mid, preferred_element_type=F32)
                           + jnp.dot(pt, lo, preferred_element_type=F32))


def _group_attention(qkv, group, perms_t):
    b, s, width = qkv.shape
    _, dilation = ATTN_WINDOWS[group]
    n_u = s // dilation
    ub = HALF if dilation == 1 else PERM_BLOCK // dilation
    rb = ub * dilation
    tq = min({1: 512, 4: 256, 16: 128}[dilation], n_u)
    nblk = tq // ub
    hb = HALF // ub
    n_slabs = width // GROUP_WIDTH
    view = qkv.reshape(b, s // rb, dilation, ub, width)
    n_halo = (s // rb) // hb
    slopes = tuple(float(np.exp2(np.float32(-8.0 * (group * HEADS_PER_GROUP + h + 1) / N_ATTN_HEADS)))
                   for h in range(HEADS_PER_GROUP))

    def main(t):
        return pl.BlockSpec((None, nblk, dilation, ub, GROUP_WIDTH),
                            lambda bb, i: (bb, i, 0, 0, group * 3 + t))

    def halo_p(t):
        return pl.BlockSpec((None, hb, dilation, ub, GROUP_WIDTH),
                            lambda bb, i: (bb, jnp.maximum(i * (nblk // hb) - 1, 0), 0, 0, group * 3 + t))

    def halo_n(t):
        return pl.BlockSpec((None, hb, dilation, ub, GROUP_WIDTH),
                            lambda bb, i: (bb, jnp.minimum((i + 1) * (nblk // hb), n_halo - 1), 0, 0,
                                           group * 3 + t))

    assert nblk % hb == 0 and width == n_slabs * GROUP_WIDTH
    pt = perms_t[max(group - 1, 0)]
    kern = functools.partial(_attn_kernel, nblk=nblk, ub=ub, n_u=n_u, dilation=dilation, slopes=slopes)
    scratch = [pltpu.VMEM((tq + 2 * HALF, GROUP_WIDTH), BF16), pltpu.VMEM((tq + 2 * HALF, GROUP_WIDTH), BF16),
               pltpu.VMEM((dilation, tq, GROUP_WIDTH), BF16), pltpu.VMEM((dilation, tq, LANES), F32)]
    rows = nblk * rb
    return pl.pallas_call(
        kern,
        out_shape=(jax.ShapeDtypeStruct((b, s, GROUP_WIDTH), BF16),
                   jax.ShapeDtypeStruct((b, s, LANES), F32)),
        grid=(b, s // rows),
        in_specs=[main(0), halo_p(1), main(1), halo_n(1), halo_p(2), main(2), halo_n(2),
                  _const_spec(pt.shape, 2)],
        out_specs=(pl.BlockSpec((None, rows, GROUP_WIDTH), lambda bb, i: (bb, i, 0)),
                   pl.BlockSpec((None, rows, LANES), lambda bb, i: (bb, i, 0))),
        scratch_shapes=scratch,
        compiler_params=_params(2, 2),
        name=f"dilated_attention_g{group}",
    )(view, view, view, view, view, view, view, pt)


def _merge_proj_kernel(o0_ref, o1_ref, o2_ref, l0_ref, l1_ref, l2_ref, x_ref, gate_ref, w_ref,
                       out_ref, mix_ref):
    l0, l1, l2 = l0_ref[...], l1_ref[...], l2_ref[...]
    m = jnp.maximum(jnp.maximum(l0, l1), l2)
    e0, e1, e2 = jnp.exp(l0 - m), jnp.exp(l1 - m), jnp.exp(l2 - m)
    inv = 1.0 / (e0 + e1 + e2)
    for g, (o_ref, e) in enumerate(((o0_ref, e0), (o1_ref, e1), (o2_ref, e2))):
        alpha = e * inv
        for h in range(HEADS_PER_GROUP):
            src = slice(h * HEAD_DIM, (h + 1) * HEAD_DIM)
            dst = slice((g * HEADS_PER_GROUP + h) * HEAD_DIM, (g * HEADS_PER_GROUP + h + 1) * HEAD_DIM)
            mix_ref[:, dst] = (alpha[:, h:h + 1] * o_ref[:, src].astype(F32)).astype(mix_ref.dtype)
    y = jnp.dot(mix_ref[...], w_ref[...], preferred_element_type=F32)
    out_ref[...] = x_ref[...] + gate_ref[...] * y


def _merge_proj(outs, lses, x, mod5, w_o, layer, w_layer, tm):
    b, s, d = x.shape
    width = w_o.shape[1]
    tok = lambda c: pl.BlockSpec((None, tm, c), lambda bb, i: (bb, i, 0))
    return pl.pallas_call(
        _merge_proj_kernel,
        out_shape=jax.ShapeDtypeStruct((b, s, d), F32),
        grid=(b, s // tm),
        in_specs=[tok(GROUP_WIDTH)] * 3 + [tok(LANES)] * 3 + [
            tok(d),
            _mod_in(layer, 2, d, 2),
            pl.BlockSpec((None, width, d), lambda bb, i: (w_layer, 0, 0)),
        ],
        out_specs=tok(d),
        scratch_shapes=[pltpu.VMEM((tm, width), BF16)],
        compiler_params=_params(2, 2),
        name="attn_merge_out_proj",
    )(*outs, *lses, x, mod5, w_o)


K1_GROUP = 8
FUSED_STAGE1_ROWS = 256
WF_COLS = 512


def _dft_tables(s):
    n2 = FFT_N2
    n1 = s // n2
    c = FOURIER_GROUP_DIM
    ang_c = 2.0 * np.pi * np.outer(np.arange(c), np.arange(c)) / c
    cs_chan = np.concatenate([np.cos(ang_c), np.sin(ang_c)], axis=1) / np.sqrt(c)
    a1 = 2.0 * np.pi * np.outer(np.arange(n1), np.arange(n1)) / n1
    c1, s1 = np.cos(a1), np.sin(a1)
    w1 = np.kron(np.block([[c1, s1], [-s1, c1]]), np.eye(ROW_TILE))
    s2 = (np.arange(n2 // ROW_TILE)[:, None, None] * ROW_TILE + np.arange(ROW_TILE)[None, None, :])
    th = 2.0 * np.pi * np.arange(n1)[None, :, None] * s2 / s
    th = th.reshape(n2 // ROW_TILE, n1 * ROW_TILE, 1)
    a2 = 2.0 * np.pi * np.outer(np.arange(n2), np.arange(n2)) / n2
    cs2 = np.concatenate([np.cos(a2), np.sin(a2)], axis=1)
    return (jnp.asarray(cs_chan, BF16), jnp.asarray(w1, BF16),
            jnp.asarray(np.cos(th), F32), jnp.asarray(np.sin(th), F32), jnp.asarray(cs2, BF16))


def _chan_dft_kernel(x_ref, g_ref, sh_ref, sc_ref, cs_ref, zr_ref, zi_ref, h_ref, rs_ref, ab_ref, *, tm):
    _ada_norm_rows(x_ref, g_ref[...], sh_ref[...], sc_ref[...], h_ref, rs_ref, ab_ref, tm)
    c = FOURIER_GROUP_DIM
    for g in range(x_ref.shape[-1] // c):
        cols = slice(g * c, (g + 1) * c)
        r = jnp.dot(h_ref[:, cols], cs_ref[...], preferred_element_type=F32)
        zr_ref[:, cols] = r[:, :c].astype(zr_ref.dtype)
        zi_ref[:, cols] = (-r[:, c:]).astype(zi_ref.dtype)


def _chan_dft(x, mod5, gains, cs_chan, layer, tm):
    b, s, d = x.shape
    tok = pl.BlockSpec((None, tm, d), lambda bb, i: (bb, i, 0))
    kern = functools.partial(_chan_dft_kernel, tm=tm)
    return pl.pallas_call(
        kern,
        out_shape=(jax.ShapeDtypeStruct((b, s, d), BF16),) * 2,
        grid=(b, s // tm),
        in_specs=[tok, _row_in(layer, d, 2), _mod_in(layer, 0, d, 2), _mod_in(layer, 1, d, 2),
                  _const_spec(cs_chan.shape, 2)],
        out_specs=(tok, tok),
        scratch_shapes=[pltpu.VMEM((tm, d), BF16)] + _norm_scratch(tm, d),
        compiler_params=_params(2, 2),
        name="fourier_channel_dft",
    )(x, gains, mod5, mod5, cs_chan)


def _pos_dft1_kernel(zr_ref, zi_ref, w_ref, tc_ref, ts_ref, ur_ref, ui_ref, *, rows):
    tc = zr_ref.shape[-1]
    z = jnp.concatenate([zr_ref[...].reshape(rows, tc), zi_ref[...].reshape(rows, tc)], axis=0)
    t = jnp.dot(w_ref[...], z, preferred_element_type=F32)
    tr, ti = t[:rows], t[rows:]
    c, sn = tc_ref[...], ts_ref[...]
    ur_ref[...] = (tr * c + ti * sn).astype(ur_ref.dtype).reshape(ur_ref.shape)
    ui_ref[...] = (ti * c - tr * sn).astype(ui_ref.dtype).reshape(ui_ref.shape)


def _pos_dft1(zr, zi, w1, tw_c, tw_s):
    b, s, d = zr.shape
    n2 = FFT_N2
    n1 = s // n2
    rows = n1 * ROW_TILE
    tc = min(d, (1024 * 1024) // rows)
    view = lambda a: a.reshape(b, n1, n2 // ROW_TILE, ROW_TILE, d)
    blk = pl.BlockSpec((None, n1, None, ROW_TILE, tc), lambda bb, j, c: (bb, 0, j, 0, c))
    tw = pl.BlockSpec((None, rows, 1), lambda bb, j, c: (j, 0, 0))
    kern = functools.partial(_pos_dft1_kernel, rows=rows)
    ur, ui = pl.pallas_call(
        kern,
        out_shape=(jax.ShapeDtypeStruct((b, n1, n2 // ROW_TILE, ROW_TILE, d), BF16),) * 2,
        grid=(b, n2 // ROW_TILE, d // tc),
        in_specs=[blk, blk, _const_spec(w1.shape, 3), tw, tw],
        out_specs=(blk, blk),
        compiler_params=_params(3, 3),
        name="fourier_pos_dft_stage1",
    )(view(zr), view(zi), w1, tw_c, tw_s)
    return ur.reshape(b, s, d), ui.reshape(b, s, d)


def _chan_pos1_kernel(x_ref, g_ref, sh_ref, sc_ref, cs_ref, w_ref, tc_ref, ts_ref, ur_ref, ui_ref,
                      h_ref, z_ref, rs_ref, ab_ref, *, rows):
    groups = NORM_CHUNK // ROW_TILE
    load = lambda c, cols: x_ref[pl.ds(pl.multiple_of(c * groups, groups), groups), :, cols].reshape(
        NORM_CHUNK, LANES)
    _ada_norm_rows(x_ref, g_ref[...], sh_ref[...], sc_ref[...], h_ref, rs_ref, ab_ref, rows, load)
    c = FOURIER_GROUP_DIM
    for g in range(h_ref.shape[-1] // c):
        cols = slice(g * c, (g + 1) * c)
        r = jnp.dot(h_ref[:, cols], cs_ref[...], preferred_element_type=F32)
        z_ref[0:rows, cols] = r[:, :c].astype(z_ref.dtype)
        z_ref[rows:, cols] = (-r[:, c:]).astype(z_ref.dtype)
    t = jnp.dot(w_ref[...], z_ref[...], preferred_element_type=F32)
    tr, ti = t[:rows], t[rows:]
    cw, sw = tc_ref[...], ts_ref[...]
    ur_ref[...] = (tr * cw + ti * sw).astype(ur_ref.dtype).reshape(ur_ref.shape)
    ui_ref[...] = (ti * cw - tr * sw).astype(ui_ref.dtype).reshape(ui_ref.shape)


def _chan_pos1(x, mod5, gains, cs_chan, w1, tw_c, tw_s, layer):
    b, s, d = x.shape
    n2 = FFT_N2
    n1 = s // n2
    rows = n1 * ROW_TILE
    shape5 = (b, n1, n2 // ROW_TILE, ROW_TILE, d)
    blk = pl.BlockSpec((None, n1, None, ROW_TILE, d), lambda bb, j: (bb, 0, j, 0, 0))
    tw = pl.BlockSpec((None, rows, 1), lambda bb, j: (j, 0, 0))
    kern = functools.partial(_chan_pos1_kernel, rows=rows)
    ur, ui = pl.pallas_call(
        kern,
        out_shape=(jax.ShapeDtypeStruct(shape5, BF16),) * 2,
        grid=(b, n2 // ROW_TILE),
        in_specs=[blk, _row_in(layer, d, 2), _mod_in(layer, 0, d, 2), _mod_in(layer, 1, d, 2),
                  _const_spec(cs_chan.shape, 2), _const_spec(w1.shape, 2), tw, tw],
        out_specs=(blk, blk),
        scratch_shapes=[pltpu.VMEM((rows, d), BF16), pltpu.VMEM((2 * rows, d), BF16)] + _norm_scratch(rows, d),
        compiler_params=_params(2, 2),
        name="fourier_chan_pos_dft_stage1",
    )(x.reshape(shape5), gains, mod5, mod5, cs_chan, w1, tw_c, tw_s)
    return ur.reshape(b, s, d), ui.reshape(b, s, d)


def _pos_dft2_proj_kernel(ur_ref, ui_ref, cs2_ref, w_ref, b_ref, x_ref, gate_ref, out_ref, ys_ref, yb_ref, *,
                          inv_norm):
    n2 = cs2_ref.shape[0]
    d = ur_ref.shape[-1]
    slabs = d // LANES

    @pl.when(pl.program_id(2) == 0)
    def _():
        for k in range(K1_GROUP):
            rows = slice(k * n2, (k + 1) * n2)
            u = jnp.concatenate([ur_ref[rows, :], ui_ref[rows, :]], axis=0)
            y = jnp.dot(cs2_ref[...], u, preferred_element_type=F32) * inv_norm
            for sl in range(slabs):
                ys_ref[sl, pl.ds(k, n2, stride=K1_GROUP), :] = y[:, sl * LANES:(sl + 1) * LANES]
        for sl in range(slabs):
            yb_ref[:, sl * LANES:(sl + 1) * LANES] = ys_ref[sl].astype(yb_ref.dtype)

    proj = jnp.dot(yb_ref[...], w_ref[pl.program_id(2)], preferred_element_type=F32) + b_ref[...]
    out = x_ref[...].reshape(proj.shape) + gate_ref[...] * proj
    out_ref[...] = out.reshape(out_ref.shape)


def _pos_dft2_proj(ur, ui, cs2, x, mod5, w_f, b_f, layer, w_layer):
    b, s, d = x.shape
    n2 = FFT_N2
    n1 = s // n2
    tn = w_f.shape[-1]
    rows = K1_GROUP * n2
    u_blk = pl.BlockSpec((None, rows, d), lambda bb, a, c: (bb, a, 0))
    x_view = x.reshape(b, n2, n1 // K1_GROUP, K1_GROUP, d)
    x_blk = pl.BlockSpec((None, n2, None, K1_GROUP, tn), lambda bb, a, c: (bb, 0, a, 0, c))
    kern = functools.partial(_pos_dft2_proj_kernel, inv_norm=float(1.0 / np.sqrt(s)))
    out = pl.pallas_call(
        kern,
        out_shape=jax.ShapeDtypeStruct(x_view.shape, F32),
        grid=(b, n1 // K1_GROUP, d // tn),
        in_specs=[u_blk, u_blk, _const_spec(cs2.shape, 3),
                  pl.BlockSpec((None, d // tn, d, tn), lambda bb, a, c: (w_layer, 0, 0, 0),
                               pipeline_mode=pl.Buffered(1)),
                  pl.BlockSpec((None, 1, tn), lambda bb, a, c: (w_layer, 0, c)),
                  x_blk,
                  pl.BlockSpec((None, None, None, 1, tn), lambda bb, a, c: (layer, bb, 2, 0, c))],
        out_specs=x_blk,
        scratch_shapes=[pltpu.VMEM((d // LANES, rows, LANES), F32), pltpu.VMEM((rows, d), BF16)],
        compiler_params=_params(3, 2),
        name="fourier_pos_dft_stage2_proj",
    )(ur, ui, cs2, w_f, b_f, x_view, mod5)
    return out.reshape(b, s, d)


def _mlp_kernel(x_ref, g_ref, sh_ref, sc_ref, gate_ref, w1_ref, b1_ref, w2_ref, b2_ref, fg_ref,
                out_ref, h_ref, rs_ref, ab_ref, *, tm, final_norm):
    j = pl.program_id(2)

    @pl.when(j == 0)
    def _():
        _ada_norm_rows(x_ref, g_ref[...], sh_ref[...], sc_ref[...], h_ref, rs_ref, ab_ref, tm)
        out_ref[...] = jnp.zeros_like(out_ref)

    u = jnp.dot(h_ref[...], w1_ref[...], preferred_element_type=F32) + b1_ref[...]
    u = jnp.maximum(u, 0.0)
    out_ref[...] += jnp.dot((u * u).astype(BF16), w2_ref[...], preferred_element_type=F32)

    @pl.when(j == pl.num_programs(2) - 1)
    def _():
        y = x_ref[...] + gate_ref[...] * (out_ref[...] + b2_ref[...])
        if final_norm:
            ms = jnp.mean(y * y, axis=-1, keepdims=True)
            y = y * lax.rsqrt(ms + RMS_EPS) * fg_ref[...]
        out_ref[...] = y


def _mlp(x, mod5, gains, w1, b1, w2, b2, final_g, layer, tm, tf, final_norm):
    b, s, d = x.shape
    dff = w1.shape[-1]
    kern = functools.partial(_mlp_kernel, tm=tm, final_norm=final_norm)
    tok = pl.BlockSpec((None, tm, d), lambda bb, i, j: (bb, i, 0))
    return pl.pallas_call(
        kern,
        out_shape=jax.ShapeDtypeStruct((b, s, d), F32),
        grid=(b, s // tm, dff // tf),
        in_specs=[
            tok,
            _row_in(layer, d, 3),
            _mod_in(layer, 3, d, 3),
            _mod_in(layer, 4, d, 3),
            _mod_in(layer, 5, d, 3),
            pl.BlockSpec((None, d, tf), lambda bb, i, j: (layer, 0, j)),
            pl.BlockSpec((None, 1, tf), lambda bb, i, j: (layer, 0, j)),
            pl.BlockSpec((None, tf, d), lambda bb, i, j: (layer, j, 0)),
            _row_in(layer, d, 3),
            pl.BlockSpec((1, d), lambda bb, i, j: (0, 0)),
        ],
        out_specs=tok,
        scratch_shapes=[pltpu.VMEM((tm, d), BF16)] + _norm_scratch(tm, d),
        compiler_params=_params(3, 2),
        name="sqrelu_mlp",
    )(x, gains, mod5, mod5, mod5, w1, b1, w2, b2, final_g)


def _trunk(x, mod5, p):
    depth = p["w1"].shape[0]
    s = x.shape[1]
    cs_chan, w1c, tw_c, tw_s, cs2 = _dft_tables(s)
    for i in range(depth):
        sub = i // 2
        if i % 2 == 0:
            qkv = _norm_proj(x, mod5, p["norm1_g"], p["w_qkv"], p["perms"], i, sub, tm=1024, tn=768)
            outs, lses = zip(*[_group_attention(qkv, g, p["perms_t"]) for g in range(len(ATTN_WINDOWS))])
            x = _merge_proj(outs, lses, x, mod5, p["w_o"], i, sub, tm=512)
        else:
            if (s // FFT_N2) * ROW_TILE <= FUSED_STAGE1_ROWS:
                ur, ui = _chan_pos1(x, mod5, p["norm1_g"], cs_chan, w1c, tw_c, tw_s, i)
            else:
                zr, zi = _chan_dft(x, mod5, p["norm1_g"], cs_chan, i, tm=512)
                ur, ui = _pos_dft1(zr, zi, w1c, tw_c, tw_s)
            x = _pos_dft2_proj(ur, ui, cs2, x, mod5, p["w_f"], p["b_f"], i, sub)
        x = _mlp(x, mod5, p["norm2_g"], p["w1"], p["b1"], p["w2"], p["b2"], p["final_g"], i,
                 tm=1024, tf=512, final_norm=(i == depth - 1))
    return x


def kernel(x_prompt, x_sample, c_prompt, c_sample, w_ada, b_ada, norm1_g, norm2_g, w_qkv, w_o, w_f, b_f,
           w1, b1, w2, b2, final_g):
    depth, d, _ = w_ada.shape
    n_p, n_s = c_prompt.shape[0], c_sample.shape[0]
    rows = -(-(n_p + n_s) // ROW_TILE) * ROW_TILE
    c_all = jnp.concatenate([c_prompt, c_sample, jnp.zeros((rows - n_p - n_s, d), F32)], axis=0)
    mod = _modulation(c_all, w_ada, b_ada)
    mod_p = mod[:, :n_p].reshape(depth, n_p, N_MOD, 1, d)
    mod_s = mod[:, n_p:n_p + n_s].reshape(depth, n_s, N_MOD, 1, d)

    row3 = lambda a: a.reshape(a.shape[0], 1, a.shape[-1])
    perms = _residue_perms()
    params = {
        "norm1_g": row3(norm1_g), "norm2_g": row3(norm2_g),
        "w_qkv": w_qkv.astype(BF16), "w_o": w_o.astype(BF16),
        "w_f": w_f.astype(BF16).reshape(w_f.shape[0], d, d // WF_COLS, WF_COLS).transpose(0, 2, 1, 3),
        "b_f": row3(b_f),
        "w1": w1.astype(BF16), "b1": row3(b1), "w2": w2.astype(BF16), "b2": row3(b2),
        "final_g": final_g.reshape(1, d),
        "perms": jnp.asarray(perms, BF16),
        "perms_t": jnp.asarray(np.transpose(perms, (0, 2, 1)), BF16),
    }
    return (_trunk(x_prompt, mod_p, params), _trunk(x_sample, mod_s, params))
```

```python
import functools

import numpy as np
import jax
import jax.numpy as jnp
from jax import lax
from jax.experimental import pallas as pl
from jax.experimental.pallas import tpu as pltpu

F32 = jnp.float32
BF16 = jnp.bfloat16

N_MOD = 6
RMS_EPS = 1e-6
MASK_VALUE = -1e30
ATTN_WINDOWS = ((128, 1), (512, 4), (2048, 16))
HEADS_PER_GROUP = 6
HEAD_DIM = 128
N_ATTN_HEADS = len(ATTN_WINDOWS) * HEADS_PER_GROUP
GROUP_WIDTH = HEADS_PER_GROUP * HEAD_DIM
FOURIER_GROUP_DIM = 256
HALF = 64
FFT_N2 = 128
LANES = 128
ROW_TILE = 16
PERM_BLOCK = 256
VMEM_LIMIT = 56 * 1024 * 1024


def _params(n_axes, n_parallel):
    sem = ("parallel",) * n_parallel + ("arbitrary",) * (n_axes - n_parallel)
    return pltpu.CompilerParams(dimension_semantics=sem, vmem_limit_bytes=VMEM_LIMIT)


def _const_spec(shape, n_grid):
    zeros = (0,) * len(shape)
    imap = (lambda a, b: zeros) if n_grid == 2 else (lambda a, b, c: zeros)
    return pl.BlockSpec(shape, imap, pipeline_mode=pl.Buffered(1))


NORM_CHUNK = 64


def _norm_scratch(tm, d):
    return [pltpu.VMEM((tm, LANES), F32), pltpu.VMEM((2, d), F32)]


def _ada_norm_rows(x_ref, gain, shift, scale, h_ref, rs_ref, ab_ref, rows, load=None):
    d = h_ref.shape[-1]
    slabs = d // LANES
    if load is None:
        load = lambda c, cols: x_ref[pl.ds(pl.multiple_of(c * NORM_CHUNK, NORM_CHUNK), NORM_CHUNK), cols]
    ab_ref[0:1, :] = gain * (1.0 + scale)
    ab_ref[1:2, :] = shift

    def stats(c, carry):
        r0 = pl.multiple_of(c * NORM_CHUNK, NORM_CHUNK)
        acc = jnp.zeros((NORM_CHUNK, LANES), F32)
        for t in range(slabs):
            xt = load(c, slice(t * LANES, (t + 1) * LANES))
            acc = acc + xt * xt
        ms = jnp.sum(acc, axis=-1, keepdims=True) * (1.0 / d)
        rs_ref[pl.ds(r0, NORM_CHUNK), :] = jnp.broadcast_to(lax.rsqrt(ms + RMS_EPS), (NORM_CHUNK, LANES))
        return carry

    lax.fori_loop(0, rows // NORM_CHUNK, stats, 0, unroll=4)

    def apply(c, carry):
        r0 = pl.multiple_of(c * NORM_CHUNK, NORM_CHUNK)
        rs = rs_ref[pl.ds(r0, NORM_CHUNK), :]
        for t in range(slabs):
            cols = slice(t * LANES, (t + 1) * LANES)
            h = load(c, cols) * rs * ab_ref[0:1, cols] + ab_ref[1:2, cols]
            h_ref[pl.ds(r0, NORM_CHUNK), cols] = h.astype(h_ref.dtype)
        return carry

    lax.fori_loop(0, rows // NORM_CHUNK, apply, 0)


def _ada_norm_chunks(x_ref, gain, shift, scale, h_ref, row0, rows):
    d = h_ref.shape[-1]
    slabs = d // LANES
    a = gain * (1.0 + scale)
    for c in range(rows // NORM_CHUNK):
        r0 = pl.multiple_of(row0 + c * NORM_CHUNK, NORM_CHUNK)
        acc = jnp.zeros((NORM_CHUNK, LANES), F32)
        for t in range(slabs):
            xt = x_ref[pl.ds(r0, NORM_CHUNK), t * LANES:(t + 1) * LANES]
            acc = acc + xt * xt
        ms = jnp.sum(acc, axis=-1, keepdims=True) * (1.0 / d)
        rs = jnp.broadcast_to(lax.rsqrt(ms + RMS_EPS), (NORM_CHUNK, LANES))
        for t in range(slabs):
            cols = slice(t * LANES, (t + 1) * LANES)
            h = x_ref[pl.ds(r0, NORM_CHUNK), cols] * rs * a[:, cols] + shift[:, cols]
            h_ref[pl.ds(r0, NORM_CHUNK), cols] = h.astype(h_ref.dtype)


def _mod_kernel(c_ref, w_ref, b_ref, o_ref):
    c = c_ref[...]
    act = (c * jax.nn.sigmoid(c)).astype(BF16)
    w = w_ref[...].astype(BF16)
    o_ref[...] = jnp.dot(act, w, preferred_element_type=F32) + b_ref[...]


def _modulation(c_all, w_ada, b_ada):
    depth, d, n = w_ada.shape
    rows = c_all.shape[0]
    tn = 1024
    return pl.pallas_call(
        _mod_kernel,
        out_shape=jax.ShapeDtypeStruct((depth, rows, n), F32),
        grid=(depth, n // tn),
        in_specs=[
            pl.BlockSpec((rows, d), lambda l, j: (0, 0)),
            pl.BlockSpec((None, d, tn), lambda l, j: (l, 0, j)),
            pl.BlockSpec((None, 1, tn), lambda l, j: (l, 0, j)),
        ],
        out_specs=pl.BlockSpec((None, rows, tn), lambda l, j: (l, 0, j)),
        compiler_params=_params(2, 2),
        name="adaln_modulation",
    )(c_all, w_ada, b_ada.reshape(depth, 1, n))


def _mod_in(layer, which, d, n_grid):
    if n_grid == 2:
        imap = lambda b, i: (layer, b, which, 0, 0)
    else:
        imap = lambda b, i, j: (layer, b, which, 0, 0)
    return pl.BlockSpec((None, None, None, 1, d), imap)


def _row_in(layer, d, n_grid):
    if n_grid == 2:
        imap = lambda b, i: (layer, 0, 0)
    else:
        imap = lambda b, i, j: (layer, 0, 0)
    return pl.BlockSpec((None, 1, d), imap)


def _residue_perms():
    mats = []
    for _, dil in ATTN_WINDOWS[1:]:
        ub = PERM_BLOCK // dil
        p = np.zeros((PERM_BLOCK, PERM_BLOCK), np.float32)
        nat = np.arange(PERM_BLOCK)
        p[(nat % dil) * ub + nat // dil, nat] = 1.0
        mats.append(p)
    return np.stack(mats)


def _norm_proj_kernel(xa_ref, xb_ref, g_ref, sh_ref, sc_ref, shn_ref, scn_ref, w_ref, p_ref, o_ref,
                      h0_ref, h1_ref, hp_ref, rs_ref, ab_ref, *, tm, steps_per_group):
    i, j = pl.program_id(1), pl.program_id(2)
    tile = pl.program_id(0) * pl.num_programs(1) + i
    half = tm // 2
    spg = steps_per_group
    gain = g_ref[...]
    rows_per_step = -(-half // (spg * NORM_CHUNK)) * NORM_CHUNK

    @pl.when((tile == 0) & (j == 0))
    def _():
        _ada_norm_rows(xa_ref, gain, sh_ref[...], sc_ref[...], h0_ref.at[0:half], rs_ref, ab_ref, half)
        _ada_norm_rows(xb_ref, gain, sh_ref[...], sc_ref[...], h0_ref.at[half:tm], rs_ref, ab_ref, half)

    def norm_ahead(x_ref, h_rows, step):
        row0 = jnp.minimum(step * rows_per_step, half - rows_per_step)
        _ada_norm_chunks(x_ref, gain, shn_ref[...], scn_ref[...], h_rows, row0, rows_per_step)

    def project(lhs_ref):
        o_ref[...] = jnp.dot(lhs_ref[...], w_ref[...], preferred_element_type=F32).astype(o_ref.dtype)

    def column_step(h_cur, h_next):
        @pl.when((j >= spg) & (j % spg == 0))
        def _():
            for blk in range(tm // PERM_BLOCK):
                rows = slice(blk * PERM_BLOCK, (blk + 1) * PERM_BLOCK)
                hp_ref[rows, :] = jnp.dot(p_ref[...], h_cur[rows, :],
                                          preferred_element_type=F32).astype(hp_ref.dtype)

        @pl.when(j < spg)
        def _():
            project(h_cur)

        @pl.when((j >= spg) & (j < 2 * spg))
        def _():
            norm_ahead(xa_ref, h_next.at[0:half], j - spg)
            project(hp_ref)

        @pl.when(j >= 2 * spg)
        def _():
            norm_ahead(xb_ref, h_next.at[half:tm], j - 2 * spg)
            project(hp_ref)

    @pl.when(tile % 2 == 0)
    def _():
        column_step(h0_ref, h1_ref)

    @pl.when(tile % 2 == 1)
    def _():
        column_step(h1_ref, h0_ref)


def _norm_proj(x, mod5, gains, w, perms, layer, w_layer, tm, tn):
    b, s, d = x.shape
    n = w.shape[-1]
    n_i = s // tm
    spg = (n // len(ATTN_WINDOWS)) // tn
    assert n // tn == len(ATTN_WINDOWS) * spg
    kern = functools.partial(_norm_proj_kernel, tm=tm, steps_per_group=spg)

    def tile_ahead(bb, i, ahead):
        t1 = jnp.minimum(bb * n_i + i + ahead, b * n_i - 1)
        return t1 // n_i, t1 % n_i

    def x_half(which, first_step):
        def imap(bb, i, j):
            b1, i1 = tile_ahead(bb, i, (j >= first_step).astype(jnp.int32))
            return b1, 2 * i1 + which, 0
        return pl.BlockSpec((None, tm // 2, d), imap)

    mod_next = lambda which: pl.BlockSpec((None, None, None, 1, d),
                                          lambda bb, i, j: (layer, tile_ahead(bb, i, 1)[0], which, 0, 0))
    return pl.pallas_call(
        kern,
        out_shape=jax.ShapeDtypeStruct((b, s, n), BF16),
        grid=(b, n_i, n // tn),
        in_specs=[
            x_half(0, spg),
            x_half(1, 2 * spg),
            _row_in(layer, d, 3),
            _mod_in(layer, 0, d, 3),
            _mod_in(layer, 1, d, 3),
            mod_next(0),
            mod_next(1),
            pl.BlockSpec((None, d, tn), lambda bb, i, j: (w_layer, 0, j)),
            pl.BlockSpec((None, PERM_BLOCK, PERM_BLOCK),
                         lambda bb, i, j: (jnp.maximum(j // spg, 1) - 1, 0, 0)),
        ],
        out_specs=pl.BlockSpec((None, tm, tn), lambda bb, i, j: (bb, i, j)),
        scratch_shapes=[pltpu.VMEM((tm, d), BF16)] * 3 + _norm_scratch(tm // 2, d),
        compiler_params=_params(3, 0),
        name="norm_qkv_proj",
    )(x, x, gains, mod5, mod5, mod5, mod5, w, perms)


def _attn_kernel(q_ref, kp_ref, km_ref, kn_ref, vp_ref, vm_ref, vn_ref, pt_ref, o_ref, lse_ref,
                 kc_ref, vc_ref, op_ref, lp_ref, *, nblk, ub, n_u, dilation, slopes):
    i = pl.program_id(1)
    tq = nblk * ub
    width = q_ref.shape[-1]
    sub = 2 * HALF
    span = sub + 2 * HALF
    row = lax.broadcasted_iota(jnp.int32, (sub, span), 0)
    col = lax.broadcasted_iota(jnp.int32, (sub, span), 1)
    adu = jnp.abs(col - HALF - row)
    band = adu <= HALF
    dist = (adu * dilation).astype(F32)
    lane = lax.broadcasted_iota(jnp.int32, (sub, LANES), 1)
    scale = HEAD_DIM ** -0.5
    blocks_per_sub = sub // ub

    def one_class(r):
        def gather(dst, prev, main, nxt):
            dst[0:HALF, :] = prev[:, r].reshape(HALF, width)
            dst[HALF:HALF + tq, :] = main[:, r].reshape(tq, width)
            dst[HALF + tq:, :] = nxt[:, r].reshape(HALF, width)

        gather(kc_ref, kp_ref, km_ref, kn_ref)
        gather(vc_ref, vp_ref, vm_ref, vn_ref)
        for sb in range(tq // sub):
            u_key = i * tq + (sb * sub - HALF) + col
            valid = band & (u_key >= 0) & (u_key < n_u)
            lse_tile = jnp.zeros((sub, LANES), F32)
            rows = slice(sb * sub, (sb + 1) * sub)
            for h in range(HEADS_PER_GROUP):
                cs = slice(h * HEAD_DIM, (h + 1) * HEAD_DIM)
                q = q_ref[sb * blocks_per_sub:(sb + 1) * blocks_per_sub, r, :, cs].reshape(sub, HEAD_DIM)
                k = kc_ref[sb * sub:sb * sub + span, cs]
                v = vc_ref[sb * sub:sb * sub + span, cs]
                s = lax.dot_general(q, k, (((1,), (1,)), ((), ())), preferred_element_type=F32) * scale
                s = s - slopes[h] * dist
                s = jnp.where(valid, s, MASK_VALUE)
                m = jnp.max(s, axis=-1, keepdims=True)
                p = jnp.exp(s - m)
                l = jnp.sum(p, axis=-1, keepdims=True)
                o = jnp.dot(p.astype(BF16), v, preferred_element_type=F32) * (1.0 / l)
                if dilation == 1:
                    o_ref[rows, cs] = o.astype(o_ref.dtype)
                else:
                    op_ref[r, rows, cs] = o.astype(op_ref.dtype)
                lse_tile = jnp.where(lane == h, m + jnp.log(l), lse_tile)
            if dilation == 1:
                lse_ref[rows, :] = lse_tile
            else:
                lp_ref[r, rows, :] = lse_tile

    if dilation == 1:
        one_class(0)
        return

    def class_step(r, carry):
        one_class(r)
        return carry

    lax.fori_loop(0, dilation, class_step, 0, unroll=4)
    pt = pt_ref[...]
    for blk in range(nblk):
        urows = slice(blk * ub, (blk + 1) * ub)
        nat = slice(blk * PERM_BLOCK, (blk + 1) * PERM_BLOCK)
        ob = jnp.concatenate([op_ref[rr, urows, :] for rr in range(dilation)], axis=0)
        o_ref[nat, :] = jnp.dot(pt, ob, preferred_element_type=F32).astype(o_ref.dtype)
        lb = jnp.concatenate([lp_ref[rr, urows, :] for rr in range(dilation)], axis=0)
        hi = lb.astype(BF16)
        rest = lb - hi.astype(F32)
        mid = rest.astype(BF16)
        lo = (rest - mid.astype(F32)).astype(BF16)
        lse_ref[nat, :] = (jnp.dot(pt, hi, preferred_element_type=F32)
                           + jnp.dot(pt, mid, preferred_element_type=F32)
                           + jnp.dot(pt, lo, preferred_element_type=F32))


def _group_attention(qkv, group, perms_t):
    b, s, width = qkv.shape
    _, dilation = ATTN_WINDOWS[group]
    n_u = s // dilation
    ub = HALF if dilation == 1 else PERM_BLOCK // dilation
    rb = ub * dilation
    tq = min({1: 512, 4: 256, 16: 128}[dilation], n_u)
    nblk = tq // ub
    hb = HALF // ub
    n_slabs = width // GROUP_WIDTH
    view = qkv.reshape(b, s // rb, dilation, ub, width)
    n_halo = (s // rb) // hb
    slopes = tuple(float(np.exp2(np.float32(-8.0 * (group * HEADS_PER_GROUP + h + 1) / N_ATTN_HEADS)))
                   for h in range(HEADS_PER_GROUP))

    def main(t):
        return pl.BlockSpec((None, nblk, dilation, ub, GROUP_WIDTH),
                            lambda bb, i: (bb, i, 0, 0, group * 3 + t))

    def halo_p(t):
        return pl.BlockSpec((None, hb, dilation, ub, GROUP_WIDTH),
                            lambda bb, i: (bb, jnp.maximum(i * (nblk // hb) - 1, 0), 0, 0, group * 3 + t))

    def halo_n(t):
        return pl.BlockSpec((None, hb, dilation, ub, GROUP_WIDTH),
                            lambda bb, i: (bb, jnp.minimum((i + 1) * (nblk // hb), n_halo - 1), 0, 0,
                                           group * 3 + t))

    assert nblk % hb == 0 and width == n_slabs * GROUP_WIDTH
    pt = perms_t[max(group - 1, 0)]
    kern = functools.partial(_attn_kernel, nblk=nblk, ub=ub, n_u=n_u, dilation=dilation, slopes=slopes)
    scratch = [pltpu.VMEM((tq + 2 * HALF, GROUP_WIDTH), BF16), pltpu.VMEM((tq + 2 * HALF, GROUP_WIDTH), BF16),
               pltpu.VMEM((dilation, tq, GROUP_WIDTH), BF16), pltpu.VMEM((dilation, tq, LANES), F32)]
    rows = nblk * rb
    return pl.pallas_call(
        kern,
        out_shape=(jax.ShapeDtypeStruct((b, s, GROUP_WIDTH), BF16),
                   jax.ShapeDtypeStruct((b, s, LANES), F32)),
        grid=(b, s // rows),
        in_specs=[main(0), halo_p(1), main(1), halo_n(1), halo_p(2), main(2), halo_n(2),
                  _const_spec(pt.shape, 2)],
        out_specs=(pl.BlockSpec((None, rows, GROUP_WIDTH), lambda bb, i: (bb, i, 0)),
                   pl.BlockSpec((None, rows, LANES), lambda bb, i: (bb, i, 0))),
        scratch_shapes=scratch,
        compiler_params=_params(2, 2),
        name=f"dilated_attention_g{group}",
    )(view, view, view, view, view, view, view, pt)


def _merge_proj_kernel(o0_ref, o1_ref, o2_ref, l0_ref, l1_ref, l2_ref, x_ref, gate_ref, w_ref,
                       out_ref, mix_ref):
    l0, l1, l2 = l0_ref[...], l1_ref[...], l2_ref[...]
    m = jnp.maximum(jnp.maximum(l0, l1), l2)
    e0, e1, e2 = jnp.exp(l0 - m), jnp.exp(l1 - m), jnp.exp(l2 - m)
    inv = 1.0 / (e0 + e1 + e2)
    for g, (o_ref, e) in enumerate(((o0_ref, e0), (o1_ref, e1), (o2_ref, e2))):
        alpha = e * inv
        for h in range(HEADS_PER_GROUP):
            src = slice(h * HEAD_DIM, (h + 1) * HEAD_DIM)
            dst = slice((g * HEADS_PER_GROUP + h) * HEAD_DIM, (g * HEADS_PER_GROUP + h + 1) * HEAD_DIM)
            mix_ref[:, dst] = (alpha[:, h:h + 1] * o_ref[:, src].astype(F32)).astype(mix_ref.dtype)
    y = jnp.dot(mix_ref[...], w_ref[...], preferred_element_type=F32)
    out_ref[...] = x_ref[...] + gate_ref[...] * y


def _merge_proj(outs, lses, x, mod5, w_o, layer, w_layer, tm):
    b, s, d = x.shape
    width = w_o.shape[1]
    tok = lambda c: pl.BlockSpec((None, tm, c), lambda bb, i: (bb, i, 0))
    return pl.pallas_call(
        _merge_proj_kernel,
        out_shape=jax.ShapeDtypeStruct((b, s, d), F32),
        grid=(b, s // tm),
        in_specs=[tok(GROUP_WIDTH)] * 3 + [tok(LANES)] * 3 + [
            tok(d),
            _mod_in(layer, 2, d, 2),
            pl.BlockSpec((None, width, d), lambda bb, i: (w_layer, 0, 0)),
        ],
        out_specs=tok(d),
        scratch_shapes=[pltpu.VMEM((tm, width), BF16)],
        compiler_params=_params(2, 2),
        name="attn_merge_out_proj",
    )(*outs, *lses, x, mod5, w_o)


K1_GROUP = 8
FUSED_STAGE1_ROWS = 256
WF_COLS = 512


def _dft_tables(s):
    n2 = FFT_N2
    n1 = s // n2
    c = FOURIER_GROUP_DIM
    ang_c = 2.0 * np.pi * np.outer(np.arange(c), np.arange(c)) / c
    cs_chan = np.concatenate([np.cos(ang_c), np.sin(ang_c)], axis=1) / np.sqrt(c)
    a1 = 2.0 * np.pi * np.outer(np.arange(n1), np.arange(n1)) / n1
    c1, s1 = np.cos(a1), np.sin(a1)
    w1 = np.kron(np.block([[c1, s1], [-s1, c1]]), np.eye(ROW_TILE))
    s2 = (np.arange(n2 // ROW_TILE)[:, None, None] * ROW_TILE + np.arange(ROW_TILE)[None, None, :])
    th = 2.0 * np.pi * np.arange(n1)[None, :, None] * s2 / s
    th = th.reshape(n2 // ROW_TILE, n1 * ROW_TILE, 1)
    a2 = 2.0 * np.pi * np.outer(np.arange(n2), np.arange(n2)) / n2
    cs2 = np.concatenate([np.cos(a2), np.sin(a2)], axis=1)
    return (jnp.asarray(cs_chan, BF16), jnp.asarray(w1, BF16),
            jnp.asarray(np.cos(th), F32), jnp.asarray(np.sin(th), F32), jnp.asarray(cs2, BF16))


def _chan_dft_kernel(x_ref, g_ref, sh_ref, sc_ref, cs_ref, zr_ref, zi_ref, h_ref, rs_ref, ab_ref, *, tm):
    _ada_norm_rows(x_ref, g_ref[...], sh_ref[...], sc_ref[...], h_ref, rs_ref, ab_ref, tm)
    c = FOURIER_GROUP_DIM
    for g in range(x_ref.shape[-1] // c):
        cols = slice(g * c, (g + 1) * c)
        r = jnp.dot(h_ref[:, cols], cs_ref[...], preferred_element_type=F32)
        zr_ref[:, cols] = r[:, :c].astype(zr_ref.dtype)
        zi_ref[:, cols] = (-r[:, c:]).astype(zi_ref.dtype)


def _chan_dft(x, mod5, gains, cs_chan, layer, tm):
    b, s, d = x.shape
    tok = pl.BlockSpec((None, tm, d), lambda bb, i: (bb, i, 0))
    kern = functools.partial(_chan_dft_kernel, tm=tm)
    return pl.pallas_call(
        kern,
        out_shape=(jax.ShapeDtypeStruct((b, s, d), BF16),) * 2,
        grid=(b, s // tm),
        in_specs=[tok, _row_in(layer, d, 2), _mod_in(layer, 0, d, 2), _mod_in(layer, 1, d, 2),
                  _const_spec(cs_chan.shape, 2)],
        out_specs=(tok, tok),
        scratch_shapes=[pltpu.VMEM((tm, d), BF16)] + _norm_scratch(tm, d),
        compiler_params=_params(2, 2),
        name="fourier_channel_dft",
    )(x, gains, mod5, mod5, cs_chan)


def _pos_dft1_kernel(zr_ref, zi_ref, w_ref, tc_ref, ts_ref, ur_ref, ui_ref, *, rows):
    tc = zr_ref.shape[-1]
    z = jnp.concatenate([zr_ref[...].reshape(rows, tc), zi_ref[...].reshape(rows, tc)], axis=0)
    t = jnp.dot(w_ref[...], z, preferred_element_type=F32)
    tr, ti = t[:rows], t[rows:]
    c, sn = tc_ref[...], ts_ref[...]
    ur_ref[...] = (tr * c + ti * sn).astype(ur_ref.dtype).reshape(ur_ref.shape)
    ui_ref[...] = (ti * c - tr * sn).astype(ui_ref.dtype).reshape(ui_ref.shape)


def _pos_dft1(zr, zi, w1, tw_c, tw_s):
    b, s, d = zr.shape
    n2 = FFT_N2
    n1 = s // n2
    rows = n1 * ROW_TILE
    tc = min(d, (1024 * 1024) // rows)
    view = lambda a: a.reshape(b, n1, n2 // ROW_TILE, ROW_TILE, d)
    blk = pl.BlockSpec((None, n1, None, ROW_TILE, tc), lambda bb, j, c: (bb, 0, j, 0, c))
    tw = pl.BlockSpec((None, rows, 1), lambda bb, j, c: (j, 0, 0))
    kern = functools.partial(_pos_dft1_kernel, rows=rows)
    ur, ui = pl.pallas_call(
        kern,
        out_shape=(jax.ShapeDtypeStruct((b, n1, n2 // ROW_TILE, ROW_TILE, d), BF16),) * 2,
        grid=(b, n2 // ROW_TILE, d // tc),
        in_specs=[blk, blk, _const_spec(w1.shape, 3), tw, tw],
        out_specs=(blk, blk),
        compiler_params=_params(3, 3),
        name="fourier_pos_dft_stage1",
    )(view(zr), view(zi), w1, tw_c, tw_s)
    return ur.reshape(b, s, d), ui.reshape(b, s, d)


def _chan_pos1_kernel(x_ref, g_ref, sh_ref, sc_ref, cs_ref, w_ref, tc_ref, ts_ref, ur_ref, ui_ref,
                      h_ref, z_ref, rs_ref, ab_ref, *, rows):
    groups = NORM_CHUNK // ROW_TILE
    load = lambda c, cols: x_ref[pl.ds(pl.multiple_of(c * groups, groups), groups), :, cols].reshape(
        NORM_CHUNK, LANES)
    _ada_norm_rows(x_ref, g_ref[...], sh_ref[...], sc_ref[...], h_ref, rs_ref, ab_ref, rows, load)
    c = FOURIER_GROUP_DIM
    for g in range(h_ref.shape[-1] // c):
        cols = slice(g * c, (g + 1) * c)
        r = jnp.dot(h_ref[:, cols], cs_ref[...], preferred_element_type=F32)
        z_ref[0:rows, cols] = r[:, :c].astype(z_ref.dtype)
        z_ref[rows:, cols] = (-r[:, c:]).astype(z_ref.dtype)
    t = jnp.dot(w_ref[...], z_ref[...], preferred_element_type=F32)
    tr, ti = t[:rows], t[rows:]
    cw, sw = tc_ref[...], ts_ref[...]
    ur_ref[...] = (tr * cw + ti * sw).astype(ur_ref.dtype).reshape(ur_ref.shape)
    ui_ref[...] = (ti * cw - tr * sw).astype(ui_ref.dtype).reshape(ui_ref.shape)


def _chan_pos1(x, mod5, gains, cs_chan, w1, tw_c, tw_s, layer):
    b, s, d = x.shape
    n2 = FFT_N2
    n1 = s // n2
    rows = n1 * ROW_TILE
    shape5 = (b, n1, n2 // ROW_TILE, ROW_TILE, d)
    blk = pl.BlockSpec((None, n1, None, ROW_TILE, d), lambda bb, j: (bb, 0, j, 0, 0))
    tw = pl.BlockSpec((None, rows, 1), lambda bb, j: (j, 0, 0))
    kern = functools.partial(_chan_pos1_kernel, rows=rows)
    ur, ui = pl.pallas_call(
        kern,
        out_shape=(jax.ShapeDtypeStruct(shape5, BF16),) * 2,
        grid=(b, n2 // ROW_TILE),
        in_specs=[blk, _row_in(layer, d, 2), _mod_in(layer, 0, d, 2), _mod_in(layer, 1, d, 2),
                  _const_spec(cs_chan.shape, 2), _const_spec(w1.shape, 2), tw, tw],
        out_specs=(blk, blk),
        scratch_shapes=[pltpu.VMEM((rows, d), BF16), pltpu.VMEM((2 * rows, d), BF16)] + _norm_scratch(rows, d),
        compiler_params=_params(2, 2),
        name="fourier_chan_pos_dft_stage1",
    )(x.reshape(shape5), gains, mod5, mod5, cs_chan, w1, tw_c, tw_s)
    return ur.reshape(b, s, d), ui.reshape(b, s, d)


def _pos_dft2_proj_kernel(ur_ref, ui_ref, cs2_ref, w_ref, b_ref, x_ref, gate_ref, out_ref, ys_ref, yb_ref, *,
                          inv_norm):
    n2 = cs2_ref.shape[0]
    d = ur_ref.shape[-1]
    slabs = d // LANES

    @pl.when(pl.program_id(2) == 0)
    def _():
        for k in range(K1_GROUP):
            rows = slice(k * n2, (k + 1) * n2)
            u = jnp.concatenate([ur_ref[rows, :], ui_ref[rows, :]], axis=0)
            y = jnp.dot(cs2_ref[...], u, preferred_element_type=F32) * inv_norm
            for sl in range(slabs):
                ys_ref[sl, pl.ds(k, n2, stride=K1_GROUP), :] = y[:, sl * LANES:(sl + 1) * LANES]
        for sl in range(slabs):
            yb_ref[:, sl * LANES:(sl + 1) * LANES] = ys_ref[sl].astype(yb_ref.dtype)

    proj = jnp.dot(yb_ref[...], w_ref[pl.program_id(2)], preferred_element_type=F32) + b_ref[...]
    out = x_ref[...].reshape(proj.shape) + gate_ref[...] * proj
    out_ref[...] = out.reshape(out_ref.shape)


def _pos_dft2_proj(ur, ui, cs2, x, mod5, w_f, b_f, layer, w_layer):
    b, s, d = x.shape
    n2 = FFT_N2
    n1 = s // n2
    tn = w_f.shape[-1]
    rows = K1_GROUP * n2
    n_a = n1 // K1_GROUP

    def u_blk(first_step):
        def imap(bb, a, c):
            nxt = jnp.minimum(bb * n_a + a + (c >= first_step).astype(jnp.int32), b * n_a - 1)
            return nxt // n_a, nxt % n_a, 0
        return pl.BlockSpec((None, rows, d), imap)

    x_view = x.reshape(b, n2, n1 // K1_GROUP, K1_GROUP, d)
    x_blk = pl.BlockSpec((None, n2, None, K1_GROUP, tn), lambda bb, a, c: (bb, 0, a, 0, c))
    kern = functools.partial(_pos_dft2_proj_kernel, inv_norm=float(1.0 / np.sqrt(s)))
    out = pl.pallas_call(
        kern,
        out_shape=jax.ShapeDtypeStruct(x_view.shape, F32),
        grid=(b, n1 // K1_GROUP, d // tn),
        in_specs=[u_blk(1), u_blk(2), _const_spec(cs2.shape, 3),
                  pl.BlockSpec((None, d // tn, d, tn), lambda bb, a, c: (w_layer, 0, 0, 0),
                               pipeline_mode=pl.Buffered(1)),
                  pl.BlockSpec((None, 1, tn), lambda bb, a, c: (w_layer, 0, c)),
                  x_blk,
                  pl.BlockSpec((None, None, None, 1, tn), lambda bb, a, c: (layer, bb, 2, 0, c))],
        out_specs=x_blk,
        scratch_shapes=[pltpu.VMEM((d // LANES, rows, LANES), F32), pltpu.VMEM((rows, d), BF16)],
        compiler_params=_params(3, 2),
        name="fourier_pos_dft_stage2_proj",
    )(ur, ui, cs2, w_f, b_f, x_view, mod5)
    return out.reshape(b, s, d)


def _mlp_kernel(x_ref, g_ref, sh_ref, sc_ref, gate_ref, w1_ref, b1_ref, w2_ref, b2_ref, fg_ref,
                out_ref, h_ref, acc_ref, rs_ref, ab_ref, *, tm, final_norm):
    j = pl.program_id(2)

    @pl.when(j == 0)
    def _():
        _ada_norm_rows(x_ref, g_ref[...], sh_ref[...], sc_ref[...], h_ref, rs_ref, ab_ref, tm)
        acc_ref[...] = jnp.zeros_like(acc_ref)

    u = jnp.dot(h_ref[...], w1_ref[...], preferred_element_type=F32) + b1_ref[...]
    u = jnp.maximum(u, 0.0)
    acc_ref[...] += jnp.dot((u * u).astype(BF16), w2_ref[...], preferred_element_type=F32)

    @pl.when(j == pl.num_programs(2) - 1)
    def _():
        y = x_ref[...] + gate_ref[...] * (acc_ref[...] + b2_ref[...])
        if final_norm:
            ms = jnp.mean(y * y, axis=-1, keepdims=True)
            y = y * lax.rsqrt(ms + RMS_EPS) * fg_ref[...]
        out_ref[...] = y


def _mlp(x, mod5, gains, w1, b1, w2, b2, final_g, layer, tm, tf, final_norm):
    b, s, d = x.shape
    dff = w1.shape[-1]
    kern = functools.partial(_mlp_kernel, tm=tm, final_norm=final_norm)
    tok = pl.BlockSpec((None, tm, d), lambda bb, i, j: (bb, i, 0))
    return pl.pallas_call(
        kern,
        out_shape=jax.ShapeDtypeStruct((b, s, d), F32),
        grid=(b, s // tm, dff // tf),
        in_specs=[
            tok,
            _row_in(layer, d, 3),
            _mod_in(layer, 3, d, 3),
            _mod_in(layer, 4, d, 3),
            _mod_in(layer, 5, d, 3),
            pl.BlockSpec((None, d, tf), lambda bb, i, j: (layer, 0, j)),
            pl.BlockSpec((None, 1, tf), lambda bb, i, j: (layer, 0, j)),
            pl.BlockSpec((None, tf, d), lambda bb, i, j: (layer, j, 0)),
            _row_in(layer, d, 3),
            pl.BlockSpec((1, d), lambda bb, i, j: (0, 0)),
        ],
        out_specs=tok,
        scratch_shapes=[pltpu.VMEM((tm, d), BF16), pltpu.VMEM((tm, d), F32)] + _norm_scratch(tm, d),
        compiler_params=_params(3, 2),
        name="sqrelu_mlp",
    )(x, gains, mod5, mod5, mod5, w1, b1, w2, b2, final_g)


def _trunk(x, mod5, p):
    depth = p["w1"].shape[0]
    s = x.shape[1]
    cs_chan, w1c, tw_c, tw_s, cs2 = _dft_tables(s)
    for i in range(depth):
        sub = i // 2
        if i % 2 == 0:
            qkv = _norm_proj(x, mod5, p["norm1_g"], p["w_qkv"], p["perms"], i, sub, tm=1024, tn=768)
            outs, lses = zip(*[_group_attention(qkv, g, p["perms_t"]) for g in range(len(ATTN_WINDOWS))])
            x = _merge_proj(outs, lses, x, mod5, p["w_o"], i, sub, tm=512)
        else:
            if (s // FFT_N2) * ROW_TILE <= FUSED_STAGE1_ROWS:
                ur, ui = _chan_pos1(x, mod5, p["norm1_g"], cs_chan, w1c, tw_c, tw_s, i)
            else:
                zr, zi = _chan_dft(x, mod5, p["norm1_g"], cs_chan, i, tm=512)
                ur, ui = _pos_dft1(zr, zi, w1c, tw_c, tw_s)
            x = _pos_dft2_proj(ur, ui, cs2, x, mod5, p["w_f"], p["b_f"], i, sub)
        x = _mlp(x, mod5, p["norm2_g"], p["w1"], p["b1"], p["w2"], p["b2"], p["final_g"], i,
                 tm=512, tf=1024, final_norm=(i == depth - 1))
    return x


def kernel(x_prompt, x_sample, c_prompt, c_sample, w_ada, b_ada, norm1_g, norm2_g, w_qkv, w_o, w_f, b_f,
           w1, b1, w2, b2, final_g):
    depth, d, _ = w_ada.shape
    n_p, n_s = c_prompt.shape[0], c_sample.shape[0]
    rows = -(-(n_p + n_s) // ROW_TILE) * ROW_TILE
    c_all = jnp.concatenate([c_prompt, c_sample, jnp.zeros((rows - n_p - n_s, d), F32)], axis=0)
    mod = _modulation(c_all, w_ada, b_ada)
    mod_p = mod[:, :n_p].reshape(depth, n_p, N_MOD, 1, d)
    mod_s = mod[:, n_p:n_p + n_s].reshape(depth, n_s, N_MOD, 1, d)

    row3 = lambda a: a.reshape(a.shape[0], 1, a.shape[-1])
    perms = _residue_perms()
    params = {
        "norm1_g": row3(norm1_g), "norm2_g": row3(norm2_g),
        "w_qkv": w_qkv.astype(BF16), "w_o": w_o.astype(BF16),
        "w_f": w_f.astype(BF16).reshape(w_f.shape[0], d, d // WF_COLS, WF_COLS).transpose(0, 2, 1, 3),
        "b_f": row3(b_f),
        "w1": w1.astype(BF16), "b1": row3(b1), "w2": w2.astype(BF16), "b2": row3(b2),
        "final_g": final_g.reshape(1, d),
        "perms": jnp.asarray(perms, BF16),
        "perms_t": jnp.asarray(np.transpose(perms, (0, 2, 1)), BF16),
    }
    return (_trunk(x_prompt, mod_p, params), _trunk(x_sample, mod_s, params))
```

```python
import functools

import numpy as np
import jax
import jax.numpy as jnp
from jax import lax
from jax.experimental import pallas as pl
from jax.experimental.pallas import tpu as pltpu

F32 = jnp.float32
BF16 = jnp.bfloat16

N_MOD = 6
RMS_EPS = 1e-6
MASK_VALUE = -1e30
ATTN_WINDOWS = ((128, 1), (512, 4), (2048, 16))
HEADS_PER_GROUP = 6
HEAD_DIM = 128
N_ATTN_HEADS = len(ATTN_WINDOWS) * HEADS_PER_GROUP
GROUP_WIDTH = HEADS_PER_GROUP * HEAD_DIM
FOURIER_GROUP_DIM = 256
HALF = 64
FFT_N2 = 128
LANES = 128
ROW_TILE = 16
PERM_BLOCK = 256
VMEM_LIMIT = 56 * 1024 * 1024


def _params(n_axes, n_parallel):
    sem = ("parallel",) * n_parallel + ("arbitrary",) * (n_axes - n_parallel)
    return pltpu.CompilerParams(dimension_semantics=sem, vmem_limit_bytes=VMEM_LIMIT)


def _const_spec(shape, n_grid):
    zeros = (0,) * len(shape)
    imap = (lambda a, b: zeros) if n_grid == 2 else (lambda a, b, c: zeros)
    return pl.BlockSpec(shape, imap, pipeline_mode=pl.Buffered(1))


NORM_CHUNK = 64


def _norm_scratch(tm, d):
    return [pltpu.VMEM((tm, LANES), F32), pltpu.VMEM((2, d), F32)]


def _ada_norm_rows(x_ref, gain, shift, scale, h_ref, rs_ref, ab_ref, rows, load=None):
    d = h_ref.shape[-1]
    slabs = d // LANES
    if load is None:
        load = lambda c, cols: x_ref[pl.ds(pl.multiple_of(c * NORM_CHUNK, NORM_CHUNK), NORM_CHUNK), cols]
    ab_ref[0:1, :] = gain * (1.0 + scale)
    ab_ref[1:2, :] = shift

    def stats(c, carry):
        r0 = pl.multiple_of(c * NORM_CHUNK, NORM_CHUNK)
        acc = jnp.zeros((NORM_CHUNK, LANES), F32)
        for t in range(slabs):
            xt = load(c, slice(t * LANES, (t + 1) * LANES))
            acc = acc + xt * xt
        ms = jnp.sum(acc, axis=-1, keepdims=True) * (1.0 / d)
        rs_ref[pl.ds(r0, NORM_CHUNK), :] = jnp.broadcast_to(lax.rsqrt(ms + RMS_EPS), (NORM_CHUNK, LANES))
        return carry

    lax.fori_loop(0, rows // NORM_CHUNK, stats, 0, unroll=4)

    def apply(c, carry):
        r0 = pl.multiple_of(c * NORM_CHUNK, NORM_CHUNK)
        rs = rs_ref[pl.ds(r0, NORM_CHUNK), :]
        for t in range(slabs):
            cols = slice(t * LANES, (t + 1) * LANES)
            h = load(c, cols) * rs * ab_ref[0:1, cols] + ab_ref[1:2, cols]
            h_ref[pl.ds(r0, NORM_CHUNK), cols] = h.astype(h_ref.dtype)
        return carry

    lax.fori_loop(0, rows // NORM_CHUNK, apply, 0)


def _ada_norm_chunks(x_ref, gain, shift, scale, h_ref, row0, rows):
    d = h_ref.shape[-1]
    slabs = d // LANES
    a = gain * (1.0 + scale)
    for c in range(rows // NORM_CHUNK):
        r0 = pl.multiple_of(row0 + c * NORM_CHUNK, NORM_CHUNK)
        acc = jnp.zeros((NORM_CHUNK, LANES), F32)
        for t in range(slabs):
            xt = x_ref[pl.ds(r0, NORM_CHUNK), t * LANES:(t + 1) * LANES]
            acc = acc + xt * xt
        ms = jnp.sum(acc, axis=-1, keepdims=True) * (1.0 / d)
        rs = jnp.broadcast_to(lax.rsqrt(ms + RMS_EPS), (NORM_CHUNK, LANES))
        for t in range(slabs):
            cols = slice(t * LANES, (t + 1) * LANES)
            h = x_ref[pl.ds(r0, NORM_CHUNK), cols] * rs * a[:, cols] + shift[:, cols]
            h_ref[pl.ds(r0, NORM_CHUNK), cols] = h.astype(h_ref.dtype)


def _mod_kernel(c_ref, w_ref, b_ref, o_ref):
    c = c_ref[...]
    act = (c * jax.nn.sigmoid(c)).astype(BF16)
    w = w_ref[...].astype(BF16)
    o_ref[...] = jnp.dot(act, w, preferred_element_type=F32) + b_ref[...]


def _modulation(c_all, w_ada, b_ada):
    depth, d, n = w_ada.shape
    rows = c_all.shape[0]
    tn = 1024
    return pl.pallas_call(
        _mod_kernel,
        out_shape=jax.ShapeDtypeStruct((depth, rows, n), F32),
        grid=(depth, n // tn),
        in_specs=[
            pl.BlockSpec((rows, d), lambda l, j: (0, 0)),
            pl.BlockSpec((None, d, tn), lambda l, j: (l, 0, j)),
            pl.BlockSpec((None, 1, tn), lambda l, j: (l, 0, j)),
        ],
        out_specs=pl.BlockSpec((None, rows, tn), lambda l, j: (l, 0, j)),
        compiler_params=_params(2, 2),
        name="adaln_modulation",
    )(c_all, w_ada, b_ada.reshape(depth, 1, n))


def _mod_in(layer, which, d, n_grid):
    if n_grid == 2:
        imap = lambda b, i: (layer, b, which, 0, 0)
    else:
        imap = lambda b, i, j: (layer, b, which, 0, 0)
    return pl.BlockSpec((None, None, None, 1, d), imap)


def _row_in(layer, d, n_grid):
    if n_grid == 2:
        imap = lambda b, i: (layer, 0, 0)
    else:
        imap = lambda b, i, j: (layer, 0, 0)
    return pl.BlockSpec((None, 1, d), imap)


def _residue_perms():
    mats = []
    for _, dil in ATTN_WINDOWS[1:]:
        ub = PERM_BLOCK // dil
        p = np.zeros((PERM_BLOCK, PERM_BLOCK), np.float32)
        nat = np.arange(PERM_BLOCK)
        p[(nat % dil) * ub + nat // dil, nat] = 1.0
        mats.append(p)
    return np.stack(mats)


def _norm_proj_kernel(xa_ref, xb_ref, g_ref, sh_ref, sc_ref, shn_ref, scn_ref, w_ref, p_ref, o_ref,
                      h0_ref, h1_ref, hp_ref, rs_ref, ab_ref, *, tm, steps_per_group):
    i, j = pl.program_id(1), pl.program_id(2)
    tile = pl.program_id(0) * pl.num_programs(1) + i
    half = tm // 2
    spg = steps_per_group
    gain = g_ref[...]
    rows_per_step = -(-half // (spg * NORM_CHUNK)) * NORM_CHUNK

    @pl.when((tile == 0) & (j == 0))
    def _():
        _ada_norm_rows(xa_ref, gain, sh_ref[...], sc_ref[...], h0_ref.at[0:half], rs_ref, ab_ref, half)
        _ada_norm_rows(xb_ref, gain, sh_ref[...], sc_ref[...], h0_ref.at[half:tm], rs_ref, ab_ref, half)

    def norm_ahead(x_ref, h_rows, step):
        row0 = jnp.minimum(step * rows_per_step, half - rows_per_step)
        _ada_norm_chunks(x_ref, gain, shn_ref[...], scn_ref[...], h_rows, row0, rows_per_step)

    def project(lhs_ref):
        o_ref[...] = jnp.dot(lhs_ref[...], w_ref[...], preferred_element_type=F32).astype(o_ref.dtype)

    def column_step(h_cur, h_next):
        @pl.when((j >= spg) & (j % spg == 0))
        def _():
            for blk in range(tm // PERM_BLOCK):
                rows = slice(blk * PERM_BLOCK, (blk + 1) * PERM_BLOCK)
                hp_ref[rows, :] = jnp.dot(p_ref[...], h_cur[rows, :],
                                          preferred_element_type=F32).astype(hp_ref.dtype)

        @pl.when(j < spg)
        def _():
            project(h_cur)

        @pl.when((j >= spg) & (j < 2 * spg))
        def _():
            norm_ahead(xa_ref, h_next.at[0:half], j - spg)
            project(hp_ref)

        @pl.when(j >= 2 * spg)
        def _():
            norm_ahead(xb_ref, h_next.at[half:tm], j - 2 * spg)
            project(hp_ref)

    @pl.when(tile % 2 == 0)
    def _():
        column_step(h0_ref, h1_ref)

    @pl.when(tile % 2 == 1)
    def _():
        column_step(h1_ref, h0_ref)


def _norm_proj(x, mod5, gains, w, perms, layer, w_layer, tm, tn):
    b, s, d = x.shape
    n = w.shape[-1]
    n_i = s // tm
    spg = (n // len(ATTN_WINDOWS)) // tn
    assert n // tn == len(ATTN_WINDOWS) * spg
    kern = functools.partial(_norm_proj_kernel, tm=tm, steps_per_group=spg)

    def tile_ahead(bb, i, ahead):
        t1 = jnp.minimum(bb * n_i + i + ahead, b * n_i - 1)
        return t1 // n_i, t1 % n_i

    def x_half(which, first_step):
        def imap(bb, i, j):
            b1, i1 = tile_ahead(bb, i, (j >= first_step).astype(jnp.int32))
            return b1, 2 * i1 + which, 0
        return pl.BlockSpec((None, tm // 2, d), imap)

    mod_next = lambda which: pl.BlockSpec((None, None, None, 1, d),
                                          lambda bb, i, j: (layer, tile_ahead(bb, i, 1)[0], which, 0, 0))
    return pl.pallas_call(
        kern,
        out_shape=jax.ShapeDtypeStruct((b, s, n), BF16),
        grid=(b, n_i, n // tn),
        in_specs=[
            x_half(0, spg),
            x_half(1, 2 * spg),
            _row_in(layer, d, 3),
            _mod_in(layer, 0, d, 3),
            _mod_in(layer, 1, d, 3),
            mod_next(0),
            mod_next(1),
            pl.BlockSpec((None, d, tn), lambda bb, i, j: (w_layer, 0, j)),
            pl.BlockSpec((None, PERM_BLOCK, PERM_BLOCK),
                         lambda bb, i, j: (jnp.maximum(j // spg, 1) - 1, 0, 0)),
        ],
        out_specs=pl.BlockSpec((None, tm, tn), lambda bb, i, j: (bb, i, j)),
        scratch_shapes=[pltpu.VMEM((tm, d), BF16)] * 3 + _norm_scratch(tm // 2, d),
        compiler_params=_params(3, 0),
        name="norm_qkv_proj",
    )(x, x, gains, mod5, mod5, mod5, mod5, w, perms)


def _attn_kernel(q_ref, kp_ref, km_ref, kn_ref, vp_ref, vm_ref, vn_ref, pt_ref, o_ref, lse_ref,
                 kc_ref, vc_ref, op_ref, lp_ref, *, nblk, ub, n_u, dilation, slopes):
    i = pl.program_id(1)
    tq = nblk * ub
    width = q_ref.shape[-1]
    sub = 2 * HALF
    span = sub + 2 * HALF
    row = lax.broadcasted_iota(jnp.int32, (sub, span), 0)
    col = lax.broadcasted_iota(jnp.int32, (sub, span), 1)
    adu = jnp.abs(col - HALF - row)
    band = adu <= HALF
    dist = (adu * dilation).astype(F32)
    lane = lax.broadcasted_iota(jnp.int32, (sub, LANES), 1)
    scale = HEAD_DIM ** -0.5
    blocks_per_sub = sub // ub

    def one_class(r):
        def gather(dst, prev, main, nxt):
            dst[0:HALF, :] = prev[:, r].reshape(HALF, width)
            dst[HALF:HALF + tq, :] = main[:, r].reshape(tq, width)
            dst[HALF + tq:, :] = nxt[:, r].reshape(HALF, width)

        gather(kc_ref, kp_ref, km_ref, kn_ref)
        gather(vc_ref, vp_ref, vm_ref, vn_ref)
        for sb in range(tq // sub):
            u_key = i * tq + (sb * sub - HALF) + col
            valid = band & (u_key >= 0) & (u_key < n_u)
            lse_tile = jnp.zeros((sub, LANES), F32)
            rows = slice(sb * sub, (sb + 1) * sub)
            for h in range(HEADS_PER_GROUP):
                cs = slice(h * HEAD_DIM, (h + 1) * HEAD_DIM)
                q = q_ref[sb * blocks_per_sub:(sb + 1) * blocks_per_sub, r, :, cs].reshape(sub, HEAD_DIM)
                k = kc_ref[sb * sub:sb * sub + span, cs]
                v = vc_ref[sb * sub:sb * sub + span, cs]
                s = lax.dot_general(q, k, (((1,), (1,)), ((), ())), preferred_element_type=F32) * scale
                s = s - slopes[h] * dist
                s = jnp.where(valid, s, MASK_VALUE)
                m = jnp.max(s, axis=-1, keepdims=True)
                p = jnp.exp(s - m)
                l = jnp.sum(p, axis=-1, keepdims=True)
                o = jnp.dot(p.astype(BF16), v, preferred_element_type=F32) * (1.0 / l)
                if dilation == 1:
                    o_ref[rows, cs] = o.astype(o_ref.dtype)
                else:
                    op_ref[r, rows, cs] = o.astype(op_ref.dtype)
                lse_tile = jnp.where(lane == h, m + jnp.log(l), lse_tile)
            if dilation == 1:
                lse_ref[rows, :] = lse_tile
            else:
                lp_ref[r, rows, :] = lse_tile

    if dilation == 1:
        one_class(0)
        return

    def class_step(r, carry):
        one_class(r)
        return carry

    lax.fori_loop(0, dilation, class_step, 0, unroll=4)
    pt = pt_ref[...]
    for blk in range(nblk):
        urows = slice(blk * ub, (blk + 1) * ub)
        nat = slice(blk * PERM_BLOCK, (blk + 1) * PERM_BLOCK)
        ob = jnp.concatenate([op_ref[rr, urows, :] for rr in range(dilation)], axis=0)
        o_ref[nat, :] = jnp.dot(pt, ob, preferred_element_type=F32).astype(o_ref.dtype)
        lb = jnp.concatenate([lp_ref[rr, urows, :] for rr in range(dilation)], axis=0)
        hi = lb.astype(BF16)
        rest = lb - hi.astype(F32)
        mid = rest.astype(BF16)
        lo = (rest - mid.astype(F32)).astype(BF16)
        lse_ref[nat, :] = (jnp.dot(pt, hi, preferred_element_type=F32)
                           + jnp.dot(pt, mid, preferred_element_type=F32)
                           + jnp.dot(pt, lo, preferred_element_type=F32))


def _group_attention(qkv, group, perms_t):
    b, s, width = qkv.shape
    _, dilation = ATTN_WINDOWS[group]
    n_u = s // dilation
    ub = HALF if dilation == 1 else PERM_BLOCK // dilation
    rb = ub * dilation
    tq = min({1: 512, 4: 256, 16: 128}[dilation], n_u)
    nblk = tq // ub
    hb = HALF // ub
    n_slabs = width // GROUP_WIDTH
    view = qkv.reshape(b, s // rb, dilation, ub, width)
    n_halo = (s // rb) // hb
    slopes = tuple(float(np.exp2(np.float32(-8.0 * (group * HEADS_PER_GROUP + h + 1) / N_ATTN_HEADS)))
                   for h in range(HEADS_PER_GROUP))

    def main(t):
        return pl.BlockSpec((None, nblk, dilation, ub, GROUP_WIDTH),
                            lambda bb, i: (bb, i, 0, 0, group * 3 + t))

    def halo_p(t):
        return pl.BlockSpec((None, hb, dilation, ub, GROUP_WIDTH),
                            lambda bb, i: (bb, jnp.maximum(i * (nblk // hb) - 1, 0), 0, 0, group * 3 + t))

    def halo_n(t):
        return pl.BlockSpec((None, hb, dilation, ub, GROUP_WIDTH),
                            lambda bb, i: (bb, jnp.minimum((i + 1) * (nblk // hb), n_halo - 1), 0, 0,
                                           group * 3 + t))

    assert nblk % hb == 0 and width == n_slabs * GROUP_WIDTH
    pt = perms_t[max(group - 1, 0)]
    kern = functools.partial(_attn_kernel, nblk=nblk, ub=ub, n_u=n_u, dilation=dilation, slopes=slopes)
    scratch = [pltpu.VMEM((tq + 2 * HALF, GROUP_WIDTH), BF16), pltpu.VMEM((tq + 2 * HALF, GROUP_WIDTH), BF16),
               pltpu.VMEM((dilation, tq, GROUP_WIDTH), BF16), pltpu.VMEM((dilation, tq, LANES), F32)]
    rows = nblk * rb
    return pl.pallas_call(
        kern,
        out_shape=(jax.ShapeDtypeStruct((b, s, GROUP_WIDTH), BF16),
                   jax.ShapeDtypeStruct((b, s, LANES), F32)),
        grid=(b, s // rows),
        in_specs=[main(0), halo_p(1), main(1), halo_n(1), halo_p(2), main(2), halo_n(2),
                  _const_spec(pt.shape, 2)],
        out_specs=(pl.BlockSpec((None, rows, GROUP_WIDTH), lambda bb, i: (bb, i, 0)),
                   pl.BlockSpec((None, rows, LANES), lambda bb, i: (bb, i, 0))),
        scratch_shapes=scratch,
        compiler_params=_params(2, 2),
        name=f"dilated_attention_g{group}",
    )(view, view, view, view, view, view, view, pt)


def _merge_proj_kernel(o0_ref, o1_ref, o2_ref, l0_ref, l1_ref, l2_ref, x_ref, gate_ref, w_ref,
                       out_ref, mix_ref):
    l0, l1, l2 = l0_ref[...], l1_ref[...], l2_ref[...]
    m = jnp.maximum(jnp.maximum(l0, l1), l2)
    e0, e1, e2 = jnp.exp(l0 - m), jnp.exp(l1 - m), jnp.exp(l2 - m)
    inv = 1.0 / (e0 + e1 + e2)
    for g, (o_ref, e) in enumerate(((o0_ref, e0), (o1_ref, e1), (o2_ref, e2))):
        alpha = e * inv
        for h in range(HEADS_PER_GROUP):
            src = slice(h * HEAD_DIM, (h + 1) * HEAD_DIM)
            dst = slice((g * HEADS_PER_GROUP + h) * HEAD_DIM, (g * HEADS_PER_GROUP + h + 1) * HEAD_DIM)
            mix_ref[:, dst] = (alpha[:, h:h + 1] * o_ref[:, src].astype(F32)).astype(mix_ref.dtype)
    y = jnp.dot(mix_ref[...], w_ref[...], preferred_element_type=F32)
    out_ref[...] = x_ref[...] + gate_ref[...] * y


def _merge_proj(outs, lses, x, mod5, w_o, layer, w_layer, tm):
    b, s, d = x.shape
    width = w_o.shape[1]
    tok = lambda c: pl.BlockSpec((None, tm, c), lambda bb, i: (bb, i, 0))
    return pl.pallas_call(
        _merge_proj_kernel,
        out_shape=jax.ShapeDtypeStruct((b, s, d), F32),
        grid=(b, s // tm),
        in_specs=[tok(GROUP_WIDTH)] * 3 + [tok(LANES)] * 3 + [
            tok(d),
            _mod_in(layer, 2, d, 2),
            pl.BlockSpec((None, width, d), lambda bb, i: (w_layer, 0, 0)),
        ],
        out_specs=tok(d),
        scratch_shapes=[pltpu.VMEM((tm, width), BF16)],
        compiler_params=_params(2, 2),
        name="attn_merge_out_proj",
    )(*outs, *lses, x, mod5, w_o)


K1_GROUP = 8
FUSED_STAGE1_ROWS = 256
STAGE2_CHUNK = 1024


def _dft_tables(s):
    n2 = FFT_N2
    n1 = s // n2
    c = FOURIER_GROUP_DIM
    ang_c = 2.0 * np.pi * np.outer(np.arange(c), np.arange(c)) / c
    cs_chan = np.concatenate([np.cos(ang_c), np.sin(ang_c)], axis=1) / np.sqrt(c)
    a1 = 2.0 * np.pi * np.outer(np.arange(n1), np.arange(n1)) / n1
    c1, s1 = np.cos(a1), np.sin(a1)
    w1 = np.kron(np.block([[c1, s1], [-s1, c1]]), np.eye(ROW_TILE))
    s2 = (np.arange(n2 // ROW_TILE)[:, None, None] * ROW_TILE + np.arange(ROW_TILE)[None, None, :])
    th = 2.0 * np.pi * np.arange(n1)[None, :, None] * s2 / s
    th = th.reshape(n2 // ROW_TILE, n1 * ROW_TILE, 1)
    a2 = 2.0 * np.pi * np.outer(np.arange(n2), np.arange(n2)) / n2
    cs2 = np.concatenate([np.cos(a2), np.sin(a2)], axis=1)
    return (jnp.asarray(cs_chan, BF16), jnp.asarray(w1, BF16),
            jnp.asarray(np.cos(th), F32), jnp.asarray(np.sin(th), F32), jnp.asarray(cs2, BF16))


def _chan_dft_kernel(x_ref, g_ref, sh_ref, sc_ref, cs_ref, zr_ref, zi_ref, h_ref, rs_ref, ab_ref, *, tm):
    _ada_norm_rows(x_ref, g_ref[...], sh_ref[...], sc_ref[...], h_ref, rs_ref, ab_ref, tm)
    c = FOURIER_GROUP_DIM
    for g in range(x_ref.shape[-1] // c):
        cols = slice(g * c, (g + 1) * c)
        r = jnp.dot(h_ref[:, cols], cs_ref[...], preferred_element_type=F32)
        zr_ref[:, cols] = r[:, :c].astype(zr_ref.dtype)
        zi_ref[:, cols] = (-r[:, c:]).astype(zi_ref.dtype)


def _chan_dft(x, mod5, gains, cs_chan, layer, tm):
    b, s, d = x.shape
    tok = pl.BlockSpec((None, tm, d), lambda bb, i: (bb, i, 0))
    kern = functools.partial(_chan_dft_kernel, tm=tm)
    return pl.pallas_call(
        kern,
        out_shape=(jax.ShapeDtypeStruct((b, s, d), BF16),) * 2,
        grid=(b, s // tm),
        in_specs=[tok, _row_in(layer, d, 2), _mod_in(layer, 0, d, 2), _mod_in(layer, 1, d, 2),
                  _const_spec(cs_chan.shape, 2)],
        out_specs=(tok, tok),
        scratch_shapes=[pltpu.VMEM((tm, d), BF16)] + _norm_scratch(tm, d),
        compiler_params=_params(2, 2),
        name="fourier_channel_dft",
    )(x, gains, mod5, mod5, cs_chan)


def _pos_dft1_kernel(zr_ref, zi_ref, w_ref, tc_ref, ts_ref, ur_ref, ui_ref, *, rows):
    tc = zr_ref.shape[-1]
    z = jnp.concatenate([zr_ref[...].reshape(rows, tc), zi_ref[...].reshape(rows, tc)], axis=0)
    t = jnp.dot(w_ref[...], z, preferred_element_type=F32)
    tr, ti = t[:rows], t[rows:]
    c, sn = tc_ref[...], ts_ref[...]
    ur_ref[...] = (tr * c + ti * sn).astype(ur_ref.dtype).reshape(ur_ref.shape)
    ui_ref[...] = (ti * c - tr * sn).astype(ui_ref.dtype).reshape(ui_ref.shape)


def _pos_dft1(zr, zi, w1, tw_c, tw_s):
    b, s, d = zr.shape
    n2 = FFT_N2
    n1 = s // n2
    rows = n1 * ROW_TILE
    tc = min(d, (1024 * 1024) // rows)
    view = lambda a: a.reshape(b, n1, n2 // ROW_TILE, ROW_TILE, d)
    blk = pl.BlockSpec((None, n1, None, ROW_TILE, tc), lambda bb, j, c: (bb, 0, j, 0, c))
    tw = pl.BlockSpec((None, rows, 1), lambda bb, j, c: (j, 0, 0))
    kern = functools.partial(_pos_dft1_kernel, rows=rows)
    ur, ui = pl.pallas_call(
        kern,
        out_shape=(jax.ShapeDtypeStruct((b, n1, n2 // ROW_TILE, ROW_TILE, d), BF16),) * 2,
        grid=(b, n2 // ROW_TILE, d // tc),
        in_specs=[blk, blk, _const_spec(w1.shape, 3), tw, tw],
        out_specs=(blk, blk),
        compiler_params=_params(3, 3),
        name="fourier_pos_dft_stage1",
    )(view(zr), view(zi), w1, tw_c, tw_s)
    return ur.reshape(b, s, d), ui.reshape(b, s, d)


def _fourier_short_kernel(xa_ref, xb_ref, g_ref, sh_ref, sc_ref, gate_ref, cs_ref, w1_ref, tc_ref, ts_ref,
                          cs2_ref, wf_ref, bf_ref, out_ref, ur_ref, ui_ref, yb_ref, *, n1, n_q, inv_norm):
    t = pl.program_id(1)
    n2 = cs2_ref.shape[0]
    d = yb_ref.shape[-1]
    rows = n1 * ROW_TILE
    n_tiles = n2 // ROW_TILE

    @pl.when(t < n_tiles)
    def _():
        def stage1(h_ref, z_ref, rs_ref, ab_ref):
            groups = NORM_CHUNK // ROW_TILE
            load = lambda c, cols: xa_ref[pl.ds(pl.multiple_of(c * groups, groups), groups), :, cols].reshape(
                NORM_CHUNK, LANES)
            _ada_norm_rows(xa_ref, g_ref[...], sh_ref[...], sc_ref[...], h_ref, rs_ref, ab_ref, rows, load)
            c = FOURIER_GROUP_DIM
            for g in range(d // c):
                cols = slice(g * c, (g + 1) * c)
                r = jnp.dot(h_ref[:, cols], cs_ref[...], preferred_element_type=F32)
                z_ref[0:rows, cols] = r[:, :c].astype(z_ref.dtype)
                z_ref[rows:, cols] = (-r[:, c:]).astype(z_ref.dtype)
            tt = jnp.dot(w1_ref[...], z_ref[...], preferred_element_type=F32)
            tr, ti = tt[:rows], tt[rows:]
            cw, sw = tc_ref[...], ts_ref[...]
            s2 = pl.ds(pl.multiple_of(t * ROW_TILE, ROW_TILE), ROW_TILE)
            ur_ref[:, s2, :] = (tr * cw + ti * sw).astype(ur_ref.dtype).reshape(n1, ROW_TILE, d)
            ui_ref[:, s2, :] = (ti * cw - tr * sw).astype(ui_ref.dtype).reshape(n1, ROW_TILE, d)

        pl.run_scoped(stage1, pltpu.VMEM((rows, d), BF16), pltpu.VMEM((2 * rows, d), BF16),
                      *_norm_scratch(rows, d))

    @pl.when(t >= n_tiles)
    def _():
        tb = t - n_tiles
        a, q = tb // n_q, tb % n_q

        @pl.when(q == 0)
        def _():
            def interleave(ys_ref):
                chunk_slabs = ys_ref.shape[0]
                chunk = chunk_slabs * LANES
                for cc in range(d // chunk):
                    cols = slice(cc * chunk, (cc + 1) * chunk)
                    for k in range(K1_GROUP):
                        k1 = a * K1_GROUP + k
                        u = jnp.concatenate([ur_ref[k1, :, cols], ui_ref[k1, :, cols]], axis=0)
                        y = jnp.dot(cs2_ref[...], u, preferred_element_type=F32) * inv_norm
                        for sl in range(chunk_slabs):
                            ys_ref[sl, pl.ds(k, n2, stride=K1_GROUP), :] = y[:, sl * LANES:(sl + 1) * LANES]
                    for sl in range(chunk_slabs):
                        c0 = cc * chunk + sl * LANES
                        yb_ref[:, c0:c0 + LANES] = ys_ref[sl].astype(yb_ref.dtype)

            pl.run_scoped(interleave, pltpu.VMEM((STAGE2_CHUNK // LANES, K1_GROUP * n2, LANES), F32))

        step_rows = xb_ref.shape[0] * K1_GROUP
        r0 = pl.multiple_of(q * step_rows, step_rows)
        proj = jnp.dot(yb_ref[pl.ds(r0, step_rows), :], wf_ref[...], preferred_element_type=F32) + bf_ref[...]
        out = xb_ref[...].reshape(proj.shape) + gate_ref[...] * proj
        out_ref[...] = out.reshape(out_ref.shape)


def _fourier_short(x, mod5, gains, tables, w_f, b_f, layer, w_layer):
    cs_chan, w1, tw_c, tw_s, cs2 = tables
    b, s, d = x.shape
    n2 = FFT_N2
    n1 = s // n2
    n_tiles = n2 // ROW_TILE
    n_a = n1 // K1_GROUP
    n_q = 4
    rows = n1 * ROW_TILE
    xa_view = x.reshape(b, n1, n_tiles, ROW_TILE, d)
    xb_view = x.reshape(b, n2, n_a, K1_GROUP, d)
    tile = lambda t: jnp.minimum(t, n_tiles - 1)
    late = lambda t: jnp.maximum(t - n_tiles, 0)
    xa_blk = pl.BlockSpec((None, n1, None, ROW_TILE, d), lambda bb, t: (bb, 0, tile(t), 0, 0))
    xb_blk = pl.BlockSpec((None, n2 // n_q, None, K1_GROUP, d),
                          lambda bb, t: (bb, late(t) % n_q, late(t) // n_q, 0, 0))
    tw = pl.BlockSpec((None, rows, 1), lambda bb, t: (tile(t), 0, 0))
    kern = functools.partial(_fourier_short_kernel, n1=n1, n_q=n_q, inv_norm=float(1.0 / np.sqrt(s)))
    out = pl.pallas_call(
        kern,
        out_shape=jax.ShapeDtypeStruct(xb_view.shape, F32),
        grid=(b, n_tiles + n_a * n_q),
        in_specs=[xa_blk, xb_blk, _row_in(layer, d, 2), _mod_in(layer, 0, d, 2), _mod_in(layer, 1, d, 2),
                  _mod_in(layer, 2, d, 2), _const_spec(cs_chan.shape, 2), _const_spec(w1.shape, 2), tw, tw,
                  _const_spec(cs2.shape, 2),
                  pl.BlockSpec((None, d, d), lambda bb, t: (w_layer, 0, 0), pipeline_mode=pl.Buffered(1)),
                  _row_in(w_layer, d, 2)],
        out_specs=xb_blk,
        scratch_shapes=[pltpu.VMEM((n1, n2, d), BF16), pltpu.VMEM((n1, n2, d), BF16),
                        pltpu.VMEM((K1_GROUP * n2, d), BF16)],
        compiler_params=_params(2, 1),
        name="fourier_mixer_short",
    )(xa_view, xb_view, gains, mod5, mod5, mod5, cs_chan, w1, tw_c, tw_s, cs2, w_f, b_f)
    return out.reshape(b, s, d)


def _pos_dft2_proj_kernel(ur_ref, ui_ref, cs2_ref, w_ref, b_ref, x_ref, gate_ref, out_ref, ys_ref, yb_ref, *,
                          inv_norm):
    n2 = cs2_ref.shape[0]
    d = ur_ref.shape[-1]
    chunk_slabs = ys_ref.shape[0]
    chunk = chunk_slabs * LANES

    @pl.when(pl.program_id(2) == 0)
    def _():
        for cc in range(d // chunk):
            for k in range(K1_GROUP):
                rows = slice(k * n2, (k + 1) * n2)
                cols = slice(cc * chunk, (cc + 1) * chunk)
                u = jnp.concatenate([ur_ref[rows, cols], ui_ref[rows, cols]], axis=0)
                y = jnp.dot(cs2_ref[...], u, preferred_element_type=F32) * inv_norm
                for sl in range(chunk_slabs):
                    ys_ref[sl, pl.ds(k, n2, stride=K1_GROUP), :] = y[:, sl * LANES:(sl + 1) * LANES]
            for sl in range(chunk_slabs):
                c0 = cc * chunk + sl * LANES
                yb_ref[:, c0:c0 + LANES] = ys_ref[sl].astype(yb_ref.dtype)

    step_rows = x_ref.shape[0] * K1_GROUP
    r0 = pl.multiple_of(pl.program_id(2) * step_rows, step_rows)
    proj = jnp.dot(yb_ref[pl.ds(r0, step_rows), :], w_ref[...], preferred_element_type=F32) + b_ref[...]
    out = x_ref[...].reshape(proj.shape) + gate_ref[...] * proj
    out_ref[...] = out.reshape(out_ref.shape)


def _pos_dft2_proj(ur, ui, cs2, x, mod5, w_f, b_f, layer, w_layer):
    b, s, d = x.shape
    n2 = FFT_N2
    n1 = s // n2
    n_q = 2
    rows = K1_GROUP * n2
    n_a = n1 // K1_GROUP

    def u_blk(first_step):
        def imap(bb, a, q):
            nxt = jnp.minimum(bb * n_a + a + (q >= first_step).astype(jnp.int32), b * n_a - 1)
            return nxt // n_a, nxt % n_a, 0
        return pl.BlockSpec((None, rows, d), imap)

    x_view = x.reshape(b, n2, n_a, K1_GROUP, d)
    x_blk = pl.BlockSpec((None, n2 // n_q, None, K1_GROUP, d), lambda bb, a, q: (bb, q, a, 0, 0))
    kern = functools.partial(_pos_dft2_proj_kernel, inv_norm=float(1.0 / np.sqrt(s)))
    out = pl.pallas_call(
        kern,
        out_shape=jax.ShapeDtypeStruct(x_view.shape, F32),
        grid=(b, n_a, n_q),
        in_specs=[u_blk(1), u_blk(1), _const_spec(cs2.shape, 3),
                  pl.BlockSpec((None, d, d), lambda bb, a, q: (w_layer, 0, 0), pipeline_mode=pl.Buffered(1)),
                  _row_in(w_layer, d, 3), x_blk, _mod_in(layer, 2, d, 3)],
        out_specs=x_blk,
        scratch_shapes=[pltpu.VMEM((STAGE2_CHUNK // LANES, rows, LANES), F32), pltpu.VMEM((rows, d), BF16)],
        compiler_params=_params(3, 2),
        name="fourier_pos_dft_stage2_proj",
    )(ur, ui, cs2, w_f, b_f, x_view, mod5)
    return out.reshape(b, s, d)


def _mlp_kernel(x_ref, g_ref, sh_ref, sc_ref, gate_ref, w1_ref, b1_ref, w2_ref, b2_ref, fg_ref,
                out_ref, h_ref, acc_ref, rs_ref, ab_ref, *, tm, final_norm):
    j = pl.program_id(2)

    @pl.when(j == 0)
    def _():
        _ada_norm_rows(x_ref, g_ref[...], sh_ref[...], sc_ref[...], h_ref, rs_ref, ab_ref, tm)
        acc_ref[...] = jnp.zeros_like(acc_ref)

    u = jnp.dot(h_ref[...], w1_ref[...], preferred_element_type=F32) + b1_ref[...]
    u = jnp.maximum(u, 0.0)
    acc_ref[...] += jnp.dot((u * u).astype(BF16), w2_ref[...], preferred_element_type=F32)

    @pl.when(j == pl.num_programs(2) - 1)
    def _():
        y = x_ref[...] + gate_ref[...] * (acc_ref[...] + b2_ref[...])
        if final_norm:
            ms = jnp.mean(y * y, axis=-1, keepdims=True)
            y = y * lax.rsqrt(ms + RMS_EPS) * fg_ref[...]
        out_ref[...] = y


def _mlp(x, mod5, gains, w1, b1, w2, b2, final_g, layer, tm, tf, final_norm):
    b, s, d = x.shape
    dff = w1.shape[-1]
    kern = functools.partial(_mlp_kernel, tm=tm, final_norm=final_norm)
    tok = pl.BlockSpec((None, tm, d), lambda bb, i, j: (bb, i, 0))
    return pl.pallas_call(
        kern,
        out_shape=jax.ShapeDtypeStruct((b, s, d), F32),
        grid=(b, s // tm, dff // tf),
        in_specs=[
            tok,
            _row_in(layer, d, 3),
            _mod_in(layer, 3, d, 3),
            _mod_in(layer, 4, d, 3),
            _mod_in(layer, 5, d, 3),
            pl.BlockSpec((None, d, tf), lambda bb, i, j: (layer, 0, j)),
            pl.BlockSpec((None, 1, tf), lambda bb, i, j: (layer, 0, j)),
            pl.BlockSpec((None, tf, d), lambda bb, i, j: (layer, j, 0)),
            _row_in(layer, d, 3),
            pl.BlockSpec((1, d), lambda bb, i, j: (0, 0)),
        ],
        out_specs=tok,
        scratch_shapes=[pltpu.VMEM((tm, d), BF16), pltpu.VMEM((tm, d), F32)] + _norm_scratch(tm, d),
        compiler_params=_params(3, 2),
        name="sqrelu_mlp",
    )(x, gains, mod5, mod5, mod5, w1, b1, w2, b2, final_g)


def _trunk(x, mod5, p):
    depth = p["w1"].shape[0]
    s = x.shape[1]
    cs_chan, w1c, tw_c, tw_s, cs2 = _dft_tables(s)
    for i in range(depth):
        sub = i // 2
        if i % 2 == 0:
            qkv = _norm_proj(x, mod5, p["norm1_g"], p["w_qkv"], p["perms"], i, sub, tm=1024, tn=768)
            outs, lses = zip(*[_group_attention(qkv, g, p["perms_t"]) for g in range(len(ATTN_WINDOWS))])
            x = _merge_proj(outs, lses, x, mod5, p["w_o"], i, sub, tm=512)
        else:
            if (s // FFT_N2) * ROW_TILE <= FUSED_STAGE1_ROWS:
                x = _fourier_short(x, mod5, p["norm1_g"], (cs_chan, w1c, tw_c, tw_s, cs2), p["w_f"], p["b_f"],
                                   i, sub)
            else:
                zr, zi = _chan_dft(x, mod5, p["norm1_g"], cs_chan, i, tm=512)
                ur, ui = _pos_dft1(zr, zi, w1c, tw_c, tw_s)
                x = _pos_dft2_proj(ur, ui, cs2, x, mod5, p["w_f"], p["b_f"], i, sub)
        x = _mlp(x, mod5, p["norm2_g"], p["w1"], p["b1"], p["w2"], p["b2"], p["final_g"], i,
                 tm=512, tf=1024, final_norm=(i == depth - 1))
    return x


def kernel(x_prompt, x_sample, c_prompt, c_sample, w_ada, b_ada, norm1_g, norm2_g, w_qkv, w_o, w_f, b_f,
           w1, b1, w2, b2, final_g):
    depth, d, _ = w_ada.shape
    n_p, n_s = c_prompt.shape[0], c_sample.shape[0]
    rows = -(-(n_p + n_s) // ROW_TILE) * ROW_TILE
    c_all = jnp.concatenate([c_prompt, c_sample, jnp.zeros((rows - n_p - n_s, d), F32)], axis=0)
    mod = _modulation(c_all, w_ada, b_ada)
    mod_p = mod[:, :n_p].reshape(depth, n_p, N_MOD, 1, d)
    mod_s = mod[:, n_p:n_p + n_s].reshape(depth, n_s, N_MOD, 1, d)

    row3 = lambda a: a.reshape(a.shape[0], 1, a.shape[-1])
    perms = _residue_perms()
    params = {
        "norm1_g": row3(norm1_g), "norm2_g": row3(norm2_g),
        "w_qkv": w_qkv.astype(BF16), "w_o": w_o.astype(BF16),
        "w_f": w_f.astype(BF16), "b_f": row3(b_f),
        "w1": w1.astype(BF16), "b1": row3(b1), "w2": w2.astype(BF16), "b2": row3(b2),
        "final_g": final_g.reshape(1, d),
        "perms": jnp.asarray(perms, BF16),
        "perms_t": jnp.asarray(np.transpose(perms, (0, 2, 1)), BF16),
    }
    return (_trunk(x_prompt, mod_p, params), _trunk(x_sample, mod_s, params))
```

```python
import functools

import numpy as np
import jax
import jax.numpy as jnp
from jax import lax
from jax.experimental import pallas as pl
from jax.experimental.pallas import tpu as pltpu

F32 = jnp.float32
BF16 = jnp.bfloat16

N_MOD = 6
RMS_EPS = 1e-6
MASK_VALUE = -1e30
ATTN_WINDOWS = ((128, 1), (512, 4), (2048, 16))
HEADS_PER_GROUP = 6
HEAD_DIM = 128
N_ATTN_HEADS = len(ATTN_WINDOWS) * HEADS_PER_GROUP
GROUP_WIDTH = HEADS_PER_GROUP * HEAD_DIM
FOURIER_GROUP_DIM = 256
HALF = 64
FFT_N2 = 128
LANES = 128
ROW_TILE = 16
PERM_BLOCK = 256
VMEM_LIMIT = 56 * 1024 * 1024


def _params(n_axes, n_parallel):
    sem = ("parallel",) * n_parallel + ("arbitrary",) * (n_axes - n_parallel)
    return pltpu.CompilerParams(dimension_semantics=sem, vmem_limit_bytes=VMEM_LIMIT)


def _const_spec(shape, n_grid):
    zeros = (0,) * len(shape)
    imap = (lambda a, b: zeros) if n_grid == 2 else (lambda a, b, c: zeros)
    return pl.BlockSpec(shape, imap, pipeline_mode=pl.Buffered(1))


NORM_CHUNK = 64


def _norm_scratch(tm, d):
    return [pltpu.VMEM((tm, LANES), F32), pltpu.VMEM((2, d), F32)]


def _ada_norm_rows(x_ref, gain, shift, scale, h_ref, rs_ref, ab_ref, rows, load=None):
    d = h_ref.shape[-1]
    slabs = d // LANES
    if load is None:
        load = lambda c, cols: x_ref[pl.ds(pl.multiple_of(c * NORM_CHUNK, NORM_CHUNK), NORM_CHUNK), cols]
    ab_ref[0:1, :] = gain * (1.0 + scale)
    ab_ref[1:2, :] = shift

    def stats(c, carry):
        r0 = pl.multiple_of(c * NORM_CHUNK, NORM_CHUNK)
        acc = jnp.zeros((NORM_CHUNK, LANES), F32)
        for t in range(slabs):
            xt = load(c, slice(t * LANES, (t + 1) * LANES))
            acc = acc + xt * xt
        ms = jnp.sum(acc, axis=-1, keepdims=True) * (1.0 / d)
        rs_ref[pl.ds(r0, NORM_CHUNK), :] = jnp.broadcast_to(lax.rsqrt(ms + RMS_EPS), (NORM_CHUNK, LANES))
        return carry

    lax.fori_loop(0, rows // NORM_CHUNK, stats, 0, unroll=4)

    def apply(c, carry):
        r0 = pl.multiple_of(c * NORM_CHUNK, NORM_CHUNK)
        rs = rs_ref[pl.ds(r0, NORM_CHUNK), :]
        for t in range(slabs):
            cols = slice(t * LANES, (t + 1) * LANES)
            h = load(c, cols) * rs * ab_ref[0:1, cols] + ab_ref[1:2, cols]
            h_ref[pl.ds(r0, NORM_CHUNK), cols] = h.astype(h_ref.dtype)
        return carry

    lax.fori_loop(0, rows // NORM_CHUNK, apply, 0)


def _mod_kernel(c_ref, w_ref, b_ref, o_ref):
    c = c_ref[...]
    act = (c * jax.nn.sigmoid(c)).astype(BF16)
    w = w_ref[...].astype(BF16)
    o_ref[...] = jnp.dot(act, w, preferred_element_type=F32) + b_ref[...]


def _modulation(c_all, w_ada, b_ada):
    depth, d, n = w_ada.shape
    rows = c_all.shape[0]
    tn = 1024
    return pl.pallas_call(
        _mod_kernel,
        out_shape=jax.ShapeDtypeStruct((depth, rows, n), F32),
        grid=(depth, n // tn),
        in_specs=[
            pl.BlockSpec((rows, d), lambda l, j: (0, 0)),
            pl.BlockSpec((None, d, tn), lambda l, j: (l, 0, j)),
            pl.BlockSpec((None, 1, tn), lambda l, j: (l, 0, j)),
        ],
        out_specs=pl.BlockSpec((None, rows, tn), lambda l, j: (l, 0, j)),
        compiler_params=_params(2, 2),
        name="adaln_modulation",
    )(c_all, w_ada, b_ada.reshape(depth, 1, n))


def _mod_in(layer, which, d, n_grid):
    if n_grid == 2:
        imap = lambda b, i: (layer, b, which, 0, 0)
    else:
        imap = lambda b, i, j: (layer, b, which, 0, 0)
    return pl.BlockSpec((None, None, None, 1, d), imap)


def _row_in(layer, d, n_grid):
    if n_grid == 2:
        imap = lambda b, i: (layer, 0, 0)
    else:
        imap = lambda b, i, j: (layer, 0, 0)
    return pl.BlockSpec((None, 1, d), imap)


def _residue_perms():
    mats = []
    for _, dil in ATTN_WINDOWS[1:]:
        ub = PERM_BLOCK // dil
        p = np.zeros((PERM_BLOCK, PERM_BLOCK), np.float32)
        nat = np.arange(PERM_BLOCK)
        p[(nat % dil) * ub + nat // dil, nat] = 1.0
        mats.append(p)
    return np.stack(mats)


def _norm_proj_kernel(xa_ref, xb_ref, g_ref, sh_ref, sc_ref, w_ref, p_ref, o_ref, h_ref, hp_ref, rs_ref, ab_ref, *,
                      tm, steps_per_group):
    j = pl.program_id(2)
    half = tm // 2
    spg = steps_per_group

    @pl.when(j == 0)
    def _():
        gain, shift, scale = g_ref[...], sh_ref[...], sc_ref[...]
        _ada_norm_rows(xa_ref, gain, shift, scale, h_ref.at[0:half], rs_ref, ab_ref, half)
        _ada_norm_rows(xb_ref, gain, shift, scale, h_ref.at[half:tm], rs_ref, ab_ref, half)

    @pl.when((j >= spg) & (j % spg == 0))
    def _():
        for blk in range(tm // PERM_BLOCK):
            rows = slice(blk * PERM_BLOCK, (blk + 1) * PERM_BLOCK)
            hp_ref[rows, :] = jnp.dot(p_ref[...], h_ref[rows, :], preferred_element_type=F32).astype(hp_ref.dtype)

    @pl.when(j < spg)
    def _():
        o_ref[...] = jnp.dot(h_ref[...], w_ref[...], preferred_element_type=F32).astype(o_ref.dtype)

    @pl.when(j >= spg)
    def _():
        o_ref[...] = jnp.dot(hp_ref[...], w_ref[...], preferred_element_type=F32).astype(o_ref.dtype)


def _norm_proj(x, mod5, gains, w, perms, layer, w_layer, tm, tn):
    b, s, d = x.shape
    n = w.shape[-1]
    n_i = s // tm
    spg = (n // len(ATTN_WINDOWS)) // tn
    assert n // tn == len(ATTN_WINDOWS) * spg
    kern = functools.partial(_norm_proj_kernel, tm=tm, steps_per_group=spg)

    def x_half(which, first_step):
        def imap(bb, i, j):
            t1 = jnp.minimum(bb * n_i + i + (j >= first_step).astype(jnp.int32), b * n_i - 1)
            return t1 // n_i, 2 * (t1 % n_i) + which, 0
        return pl.BlockSpec((None, tm // 2, d), imap)

    return pl.pallas_call(
        kern,
        out_shape=jax.ShapeDtypeStruct((b, s, n), BF16),
        grid=(b, n_i, n // tn),
        in_specs=[
            x_half(0, spg),
            x_half(1, 2 * spg),
            _row_in(layer, d, 3),
            _mod_in(layer, 0, d, 3),
            _mod_in(layer, 1, d, 3),
            pl.BlockSpec((None, d, tn), lambda bb, i, j: (w_layer, 0, j)),
            pl.BlockSpec((None, PERM_BLOCK, PERM_BLOCK),
                         lambda bb, i, j: (jnp.maximum(j // spg, 1) - 1, 0, 0)),
        ],
        out_specs=pl.BlockSpec((None, tm, tn), lambda bb, i, j: (bb, i, j)),
        scratch_shapes=[pltpu.VMEM((tm, d), BF16)] * 2 + _norm_scratch(tm // 2, d),
        compiler_params=_params(3, 0),
        name="norm_qkv_proj",
    )(x, x, gains, mod5, mod5, w, perms)


def _attn_kernel(q_ref, kp_ref, km_ref, kn_ref, vp_ref, vm_ref, vn_ref, pt_ref, o_ref, lse_ref,
                 kc_ref, vc_ref, op_ref, lp_ref, *, nblk, ub, n_u, dilation, slopes):
    i = pl.program_id(1)
    tq = nblk * ub
    width = q_ref.shape[-1]
    sub = 2 * HALF
    span = sub + 2 * HALF
    row = lax.broadcasted_iota(jnp.int32, (sub, span), 0)
    col = lax.broadcasted_iota(jnp.int32, (sub, span), 1)
    adu = jnp.abs(col - HALF - row)
    band = adu <= HALF
    dist = (adu * dilation).astype(F32)
    lane = lax.broadcasted_iota(jnp.int32, (sub, LANES), 1)
    scale = HEAD_DIM ** -0.5
    blocks_per_sub = sub // ub

    def one_class(r):
        def gather(dst, prev, main, nxt):
            dst[0:HALF, :] = prev[:, r].reshape(HALF, width)
            dst[HALF:HALF + tq, :] = main[:, r].reshape(tq, width)
            dst[HALF + tq:, :] = nxt[:, r].reshape(HALF, width)

        gather(kc_ref, kp_ref, km_ref, kn_ref)
        gather(vc_ref, vp_ref, vm_ref, vn_ref)
        for sb in range(tq // sub):
            u_key = i * tq + (sb * sub - HALF) + col
            valid = band & (u_key >= 0) & (u_key < n_u)
            lse_tile = jnp.zeros((sub, LANES), F32)
            rows = slice(sb * sub, (sb + 1) * sub)
            for h in range(HEADS_PER_GROUP):
                cs = slice(h * HEAD_DIM, (h + 1) * HEAD_DIM)
                q = q_ref[sb * blocks_per_sub:(sb + 1) * blocks_per_sub, r, :, cs].reshape(sub, HEAD_DIM)
                k = kc_ref[sb * sub:sb * sub + span, cs]
                v = vc_ref[sb * sub:sb * sub + span, cs]
                s = lax.dot_general(q, k, (((1,), (1,)), ((), ())), preferred_element_type=F32) * scale
                s = s - slopes[h] * dist
                s = jnp.where(valid, s, MASK_VALUE)
                m = jnp.max(s, axis=-1, keepdims=True)
                p = jnp.exp(s - m)
                l = jnp.sum(p, axis=-1, keepdims=True)
                o = jnp.dot(p.astype(BF16), v, preferred_element_type=F32) * (1.0 / l)
                if dilation == 1:
                    o_ref[rows, cs] = o.astype(o_ref.dtype)
                else:
                    op_ref[r, rows, cs] = o.astype(op_ref.dtype)
                lse_tile = jnp.where(lane == h, m + jnp.log(l), lse_tile)
            if dilation == 1:
                lse_ref[rows, :] = lse_tile
            else:
                lp_ref[r, rows, :] = lse_tile

    if dilation == 1:
        one_class(0)
        return

    def class_step(r, carry):
        one_class(r)
        return carry

    lax.fori_loop(0, dilation, class_step, 0, unroll=4)
    pt = pt_ref[...]
    for blk in range(nblk):
        urows = slice(blk * ub, (blk + 1) * ub)
        nat = slice(blk * PERM_BLOCK, (blk + 1) * PERM_BLOCK)
        ob = jnp.concatenate([op_ref[rr, urows, :] for rr in range(dilation)], axis=0)
        o_ref[nat, :] = jnp.dot(pt, ob, preferred_element_type=F32).astype(o_ref.dtype)
        lb = jnp.concatenate([lp_ref[rr, urows, :] for rr in range(dilation)], axis=0)
        hi = lb.astype(BF16)
        rest = lb - hi.astype(F32)
        mid = rest.astype(BF16)
        lo = (rest - mid.astype(F32)).astype(BF16)
        lse_ref[nat, :] = (jnp.dot(pt, hi, preferred_element_type=F32)
                           + jnp.dot(pt, mid, preferred_element_type=F32)
                           + jnp.dot(pt, lo, preferred_element_type=F32))


def _group_attention(qkv, group, perms_t):
    b, s, width = qkv.shape
    _, dilation = ATTN_WINDOWS[group]
    n_u = s // dilation
    ub = HALF if dilation == 1 else PERM_BLOCK // dilation
    rb = ub * dilation
    tq = min({1: 512, 4: 256, 16: 128}[dilation], n_u)
    nblk = tq // ub
    hb = HALF // ub
    n_slabs = width // GROUP_WIDTH
    view = qkv.reshape(b, s // rb, dilation, ub, width)
    n_halo = (s // rb) // hb
    slopes = tuple(float(np.exp2(np.float32(-8.0 * (group * HEADS_PER_GROUP + h + 1) / N_ATTN_HEADS)))
                   for h in range(HEADS_PER_GROUP))

    def main(t):
        return pl.BlockSpec((None, nblk, dilation, ub, GROUP_WIDTH),
                            lambda bb, i: (bb, i, 0, 0, group * 3 + t))

    def halo_p(t):
        return pl.BlockSpec((None, hb, dilation, ub, GROUP_WIDTH),
                            lambda bb, i: (bb, jnp.maximum(i * (nblk // hb) - 1, 0), 0, 0, group * 3 + t))

    def halo_n(t):
        return pl.BlockSpec((None, hb, dilation, ub, GROUP_WIDTH),
                            lambda bb, i: (bb, jnp.minimum((i + 1) * (nblk // hb), n_halo - 1), 0, 0,
                                           group * 3 + t))

    assert nblk % hb == 0 and width == n_slabs * GROUP_WIDTH
    pt = perms_t[max(group - 1, 0)]
    kern = functools.partial(_attn_kernel, nblk=nblk, ub=ub, n_u=n_u, dilation=dilation, slopes=slopes)
    scratch = [pltpu.VMEM((tq + 2 * HALF, GROUP_WIDTH), BF16), pltpu.VMEM((tq + 2 * HALF, GROUP_WIDTH), BF16),
               pltpu.VMEM((dilation, tq, GROUP_WIDTH), BF16), pltpu.VMEM((dilation, tq, LANES), F32)]
    rows = nblk * rb
    return pl.pallas_call(
        kern,
        out_shape=(jax.ShapeDtypeStruct((b, s, GROUP_WIDTH), BF16),
                   jax.ShapeDtypeStruct((b, s, LANES), F32)),
        grid=(b, s // rows),
        in_specs=[main(0), halo_p(1), main(1), halo_n(1), halo_p(2), main(2), halo_n(2),
                  _const_spec(pt.shape, 2)],
        out_specs=(pl.BlockSpec((None, rows, GROUP_WIDTH), lambda bb, i: (bb, i, 0)),
                   pl.BlockSpec((None, rows, LANES), lambda bb, i: (bb, i, 0))),
        scratch_shapes=scratch,
        compiler_params=_params(2, 2),
        name=f"dilated_attention_g{group}",
    )(view, view, view, view, view, view, view, pt)


def _merge_proj_kernel(o0_ref, o1_ref, o2_ref, l0_ref, l1_ref, l2_ref, x_ref, gate_ref, w_ref,
                       out_ref, mix_ref):
    l0, l1, l2 = l0_ref[...], l1_ref[...], l2_ref[...]
    m = jnp.maximum(jnp.maximum(l0, l1), l2)
    e0, e1, e2 = jnp.exp(l0 - m), jnp.exp(l1 - m), jnp.exp(l2 - m)
    inv = 1.0 / (e0 + e1 + e2)
    for g, (o_ref, e) in enumerate(((o0_ref, e0), (o1_ref, e1), (o2_ref, e2))):
        alpha = e * inv
        for h in range(HEADS_PER_GROUP):
            src = slice(h * HEAD_DIM, (h + 1) * HEAD_DIM)
            dst = slice((g * HEADS_PER_GROUP + h) * HEAD_DIM, (g * HEADS_PER_GROUP + h + 1) * HEAD_DIM)
            mix_ref[:, dst] = (alpha[:, h:h + 1] * o_ref[:, src].astype(F32)).astype(mix_ref.dtype)
    y = jnp.dot(mix_ref[...], w_ref[...], preferred_element_type=F32)
    out_ref[...] = x_ref[...] + gate_ref[...] * y


def _merge_proj(outs, lses, x, mod5, w_o, layer, w_layer, tm):
    b, s, d = x.shape
    width = w_o.shape[1]
    tok = lambda c: pl.BlockSpec((None, tm, c), lambda bb, i: (bb, i, 0))
    return pl.pallas_call(
        _merge_proj_kernel,
        out_shape=jax.ShapeDtypeStruct((b, s, d), F32),
        grid=(b, s // tm),
        in_specs=[tok(GROUP_WIDTH)] * 3 + [tok(LANES)] * 3 + [
            tok(d),
            _mod_in(layer, 2, d, 2),
            pl.BlockSpec((None, width, d), lambda bb, i: (w_layer, 0, 0)),
        ],
        out_specs=tok(d),
        scratch_shapes=[pltpu.VMEM((tm, width), BF16)],
        compiler_params=_params(2, 2),
        name="attn_merge_out_proj",
    )(*outs, *lses, x, mod5, w_o)


K1_GROUP = 8
FUSED_STAGE1_ROWS = 256
STAGE2_CHUNK = 1024


def _dft_tables(s):
    n2 = FFT_N2
    n1 = s // n2
    c = FOURIER_GROUP_DIM
    ang_c = 2.0 * np.pi * np.outer(np.arange(c), np.arange(c)) / c
    cs_chan = np.concatenate([np.cos(ang_c), np.sin(ang_c)], axis=1) / np.sqrt(c)
    a1 = 2.0 * np.pi * np.outer(np.arange(n1), np.arange(n1)) / n1
    c1, s1 = np.cos(a1), np.sin(a1)
    w1 = np.kron(np.block([[c1, s1], [-s1, c1]]), np.eye(ROW_TILE))
    s2 = (np.arange(n2 // ROW_TILE)[:, None, None] * ROW_TILE + np.arange(ROW_TILE)[None, None, :])
    th = 2.0 * np.pi * np.arange(n1)[None, :, None] * s2 / s
    th = th.reshape(n2 // ROW_TILE, n1 * ROW_TILE, 1)
    a2 = 2.0 * np.pi * np.outer(np.arange(n2), np.arange(n2)) / n2
    cs2 = np.concatenate([np.cos(a2), np.sin(a2)], axis=1)
    return (jnp.asarray(cs_chan, BF16), jnp.asarray(w1, BF16),
            jnp.asarray(np.cos(th), F32), jnp.asarray(np.sin(th), F32), jnp.asarray(cs2, BF16))


def _chan_dft_kernel(x_ref, g_ref, sh_ref, sc_ref, cs_ref, zr_ref, zi_ref, h_ref, rs_ref, ab_ref, *, tm):
    _ada_norm_rows(x_ref, g_ref[...], sh_ref[...], sc_ref[...], h_ref, rs_ref, ab_ref, tm)
    c = FOURIER_GROUP_DIM
    for g in range(x_ref.shape[-1] // c):
        cols = slice(g * c, (g + 1) * c)
        r = jnp.dot(h_ref[:, cols], cs_ref[...], preferred_element_type=F32)
        zr_ref[:, cols] = r[:, :c].astype(zr_ref.dtype)
        zi_ref[:, cols] = (-r[:, c:]).astype(zi_ref.dtype)


def _chan_dft(x, mod5, gains, cs_chan, layer, tm):
    b, s, d = x.shape
    tok = pl.BlockSpec((None, tm, d), lambda bb, i: (bb, i, 0))
    kern = functools.partial(_chan_dft_kernel, tm=tm)
    return pl.pallas_call(
        kern,
        out_shape=(jax.ShapeDtypeStruct((b, s, d), BF16),) * 2,
        grid=(b, s // tm),
        in_specs=[tok, _row_in(layer, d, 2), _mod_in(layer, 0, d, 2), _mod_in(layer, 1, d, 2),
                  _const_spec(cs_chan.shape, 2)],
        out_specs=(tok, tok),
        scratch_shapes=[pltpu.VMEM((tm, d), BF16)] + _norm_scratch(tm, d),
        compiler_params=_params(2, 2),
        name="fourier_channel_dft",
    )(x, gains, mod5, mod5, cs_chan)


def _pos_dft1_kernel(zr_ref, zi_ref, w_ref, tc_ref, ts_ref, ur_ref, ui_ref, *, rows):
    tc = zr_ref.shape[-1]
    z = jnp.concatenate([zr_ref[...].reshape(rows, tc), zi_ref[...].reshape(rows, tc)], axis=0)
    t = jnp.dot(w_ref[...], z, preferred_element_type=F32)
    tr, ti = t[:rows], t[rows:]
    c, sn = tc_ref[...], ts_ref[...]
    ur_ref[...] = (tr * c + ti * sn).astype(ur_ref.dtype).reshape(ur_ref.shape)
    ui_ref[...] = (ti * c - tr * sn).astype(ui_ref.dtype).reshape(ui_ref.shape)


def _pos_dft1(zr, zi, w1, tw_c, tw_s):
    b, s, d = zr.shape
    n2 = FFT_N2
    n1 = s // n2
    rows = n1 * ROW_TILE
    tc = min(d, (1024 * 1024) // rows)
    view = lambda a: a.reshape(b, n1, n2 // ROW_TILE, ROW_TILE, d)
    blk = pl.BlockSpec((None, n1, None, ROW_TILE, tc), lambda bb, j, c: (bb, 0, j, 0, c))
    tw = pl.BlockSpec((None, rows, 1), lambda bb, j, c: (j, 0, 0))
    kern = functools.partial(_pos_dft1_kernel, rows=rows)
    ur, ui = pl.pallas_call(
        kern,
        out_shape=(jax.ShapeDtypeStruct((b, n1, n2 // ROW_TILE, ROW_TILE, d), BF16),) * 2,
        grid=(b, n2 // ROW_TILE, d // tc),
        in_specs=[blk, blk, _const_spec(w1.shape, 3), tw, tw],
        out_specs=(blk, blk),
        compiler_params=_params(3, 3),
        name="fourier_pos_dft_stage1",
    )(view(zr), view(zi), w1, tw_c, tw_s)
    return ur.reshape(b, s, d), ui.reshape(b, s, d)


def _fourier_short_kernel(xa_ref, xb_ref, g_ref, sh_ref, sc_ref, gate_ref, cs_ref, w1_ref, tc_ref, ts_ref,
                          cs2_ref, wf_ref, bf_ref, out_ref, ur_ref, ui_ref, yb_ref, *, n1, n_q, inv_norm):
    t = pl.program_id(1)
    n2 = cs2_ref.shape[0]
    d = yb_ref.shape[-1]
    rows = n1 * ROW_TILE
    n_tiles = n2 // ROW_TILE

    @pl.when(t < n_tiles)
    def _():
        def stage1(h_ref, z_ref, rs_ref, ab_ref):
            groups = NORM_CHUNK // ROW_TILE
            load = lambda c, cols: xa_ref[pl.ds(pl.multiple_of(c * groups, groups), groups), :, cols].reshape(
                NORM_CHUNK, LANES)
            _ada_norm_rows(xa_ref, g_ref[...], sh_ref[...], sc_ref[...], h_ref, rs_ref, ab_ref, rows, load)
            c = FOURIER_GROUP_DIM
            for g in range(d // c):
                cols = slice(g * c, (g + 1) * c)
                r = jnp.dot(h_ref[:, cols], cs_ref[...], preferred_element_type=F32)
                z_ref[0:rows, cols] = r[:, :c].astype(z_ref.dtype)
                z_ref[rows:, cols] = (-r[:, c:]).astype(z_ref.dtype)
            tt = jnp.dot(w1_ref[...], z_ref[...], preferred_element_type=F32)
            tr, ti = tt[:rows], tt[rows:]
            cw, sw = tc_ref[...], ts_ref[...]
            s2 = pl.ds(pl.multiple_of(t * ROW_TILE, ROW_TILE), ROW_TILE)
            ur_ref[:, s2, :] = (tr * cw + ti * sw).astype(ur_ref.dtype).reshape(n1, ROW_TILE, d)
            ui_ref[:, s2, :] = (ti * cw - tr * sw).astype(ui_ref.dtype).reshape(n1, ROW_TILE, d)

        pl.run_scoped(stage1, pltpu.VMEM((rows, d), BF16), pltpu.VMEM((2 * rows, d), BF16),
                      *_norm_scratch(rows, d))

    @pl.when(t >= n_tiles)
    def _():
        tb = t - n_tiles
        a, q = tb // n_q, tb % n_q

        @pl.when(q == 0)
        def _():
            def interleave(ys_ref):
                chunk_slabs = ys_ref.shape[0]
                chunk = chunk_slabs * LANES
                for cc in range(d // chunk):
                    cols = slice(cc * chunk, (cc + 1) * chunk)
                    for k in range(K1_GROUP):
                        k1 = a * K1_GROUP + k
                        u = jnp.concatenate([ur_ref[k1, :, cols], ui_ref[k1, :, cols]], axis=0)
                        y = jnp.dot(cs2_ref[...], u, preferred_element_type=F32) * inv_norm
                        for sl in range(chunk_slabs):
                            ys_ref[sl, pl.ds(k, n2, stride=K1_GROUP), :] = y[:, sl * LANES:(sl + 1) * LANES]
                    for sl in range(chunk_slabs):
                        c0 = cc * chunk + sl * LANES
                        yb_ref[:, c0:c0 + LANES] = ys_ref[sl].astype(yb_ref.dtype)

            pl.run_scoped(interleave, pltpu.VMEM((STAGE2_CHUNK // LANES, K1_GROUP * n2, LANES), F32))

        step_rows = xb_ref.shape[0] * K1_GROUP
        r0 = pl.multiple_of(q * step_rows, step_rows)
        proj = jnp.dot(yb_ref[pl.ds(r0, step_rows), :], wf_ref[...], preferred_element_type=F32) + bf_ref[...]
        out = xb_ref[...].reshape(proj.shape) + gate_ref[...] * proj
        out_ref[...] = out.reshape(out_ref.shape)


def _fourier_short(x, mod5, gains, tables, w_f, b_f, layer, w_layer):
    cs_chan, w1, tw_c, tw_s, cs2 = tables
    b, s, d = x.shape
    n2 = FFT_N2
    n1 = s // n2
    n_tiles = n2 // ROW_TILE
    n_a = n1 // K1_GROUP
    n_q = 4
    rows = n1 * ROW_TILE
    xa_view = x.reshape(b, n1, n_tiles, ROW_TILE, d)
    xb_view = x.reshape(b, n2, n_a, K1_GROUP, d)
    tile = lambda t: jnp.minimum(t, n_tiles - 1)
    late = lambda t: jnp.maximum(t - n_tiles, 0)
    xa_blk = pl.BlockSpec((None, n1, None, ROW_TILE, d), lambda bb, t: (bb, 0, tile(t), 0, 0))
    xb_blk = pl.BlockSpec((None, n2 // n_q, None, K1_GROUP, d),
                          lambda bb, t: (bb, late(t) % n_q, late(t) // n_q, 0, 0))
    tw = pl.BlockSpec((None, rows, 1), lambda bb, t: (tile(t), 0, 0))
    kern = functools.partial(_fourier_short_kernel, n1=n1, n_q=n_q, inv_norm=float(1.0 / np.sqrt(s)))
    out = pl.pallas_call(
        kern,
        out_shape=jax.ShapeDtypeStruct(xb_view.shape, F32),
        grid=(b, n_tiles + n_a * n_q),
        in_specs=[xa_blk, xb_blk, _row_in(layer, d, 2), _mod_in(layer, 0, d, 2), _mod_in(layer, 1, d, 2),
                  _mod_in(layer, 2, d, 2), _const_spec(cs_chan.shape, 2), _const_spec(w1.shape, 2), tw, tw,
                  _const_spec(cs2.shape, 2),
                  pl.BlockSpec((None, d, d), lambda bb, t: (w_layer, 0, 0), pipeline_mode=pl.Buffered(1)),
                  _row_in(w_layer, d, 2)],
        out_specs=xb_blk,
        scratch_shapes=[pltpu.VMEM((n1, n2, d), BF16), pltpu.VMEM((n1, n2, d), BF16),
                        pltpu.VMEM((K1_GROUP * n2, d), BF16)],
        compiler_params=_params(2, 1),
        name="fourier_mixer_short",
    )(xa_view, xb_view, gains, mod5, mod5, mod5, cs_chan, w1, tw_c, tw_s, cs2, w_f, b_f)
    return out.reshape(b, s, d)


def _pos_dft2_proj_kernel(ur_ref, ui_ref, cs2_ref, w_ref, b_ref, x_ref, gate_ref, out_ref, ys_ref, yb_ref, *,
                          inv_norm):
    n2 = cs2_ref.shape[0]
    d = ur_ref.shape[-1]
    chunk_slabs = ys_ref.shape[0]
    chunk = chunk_slabs * LANES

    @pl.when(pl.program_id(2) == 0)
    def _():
        for cc in range(d // chunk):
            for k in range(K1_GROUP):
                rows = slice(k * n2, (k + 1) * n2)
                cols = slice(cc * chunk, (cc + 1) * chunk)
                u = jnp.concatenate([ur_ref[rows, cols], ui_ref[rows, cols]], axis=0)
                y = jnp.dot(cs2_ref[...], u, preferred_element_type=F32) * inv_norm
                for sl in range(chunk_slabs):
                    ys_ref[sl, pl.ds(k, n2, stride=K1_GROUP), :] = y[:, sl * LANES:(sl + 1) * LANES]
            for sl in range(chunk_slabs):
                c0 = cc * chunk + sl * LANES
                yb_ref[:, c0:c0 + LANES] = ys_ref[sl].astype(yb_ref.dtype)

    step_rows = x_ref.shape[0] * K1_GROUP
    r0 = pl.multiple_of(pl.program_id(2) * step_rows, step_rows)
    proj = jnp.dot(yb_ref[pl.ds(r0, step_rows), :], w_ref[...], preferred_element_type=F32) + b_ref[...]
    out = x_ref[...].reshape(proj.shape) + gate_ref[...] * proj
    out_ref[...] = out.reshape(out_ref.shape)


def _pos_dft2_proj(ur, ui, cs2, x, mod5, w_f, b_f, layer, w_layer):
    b, s, d = x.shape
    n2 = FFT_N2
    n1 = s // n2
    n_q = 2
    rows = K1_GROUP * n2
    n_a = n1 // K1_GROUP

    def u_blk(first_step):
        def imap(bb, a, q):
            nxt = jnp.minimum(bb * n_a + a + (q >= first_step).astype(jnp.int32), b * n_a - 1)
            return nxt // n_a, nxt % n_a, 0
        return pl.BlockSpec((None, rows, d), imap)

    x_view = x.reshape(b, n2, n_a, K1_GROUP, d)
    x_blk = pl.BlockSpec((None, n2 // n_q, None, K1_GROUP, d), lambda bb, a, q: (bb, q, a, 0, 0))
    kern = functools.partial(_pos_dft2_proj_kernel, inv_norm=float(1.0 / np.sqrt(s)))
    out = pl.pallas_call(
        kern,
        out_shape=jax.ShapeDtypeStruct(x_view.shape, F32),
        grid=(b, n_a, n_q),
        in_specs=[u_blk(1), u_blk(1), _const_spec(cs2.shape, 3),
                  pl.BlockSpec((None, d, d), lambda bb, a, q: (w_layer, 0, 0), pipeline_mode=pl.Buffered(1)),
                  _row_in(w_layer, d, 3), x_blk, _mod_in(layer, 2, d, 3)],
        out_specs=x_blk,
        scratch_shapes=[pltpu.VMEM((STAGE2_CHUNK // LANES, rows, LANES), F32), pltpu.VMEM((rows, d), BF16)],
        compiler_params=_params(3, 2),
        name="fourier_pos_dft_stage2_proj",
    )(ur, ui, cs2, w_f, b_f, x_view, mod5)
    return out.reshape(b, s, d)


def _mlp_kernel(x_ref, g_ref, sh_ref, sc_ref, gate_ref, w1_ref, b1_ref, w2_ref, b2_ref, fg_ref,
                out_ref, h_ref, acc_ref, rs_ref, ab_ref, *, tm, final_norm):
    j = pl.program_id(2)

    @pl.when(j == 0)
    def _():
        _ada_norm_rows(x_ref, g_ref[...], sh_ref[...], sc_ref[...], h_ref, rs_ref, ab_ref, tm)
        acc_ref[...] = jnp.zeros_like(acc_ref)

    u = jnp.dot(h_ref[...], w1_ref[...], preferred_element_type=F32) + b1_ref[...]
    u = jnp.maximum(u, 0.0)
    acc_ref[...] += jnp.dot((u * u).astype(BF16), w2_ref[...], preferred_element_type=F32)

    @pl.when(j == pl.num_programs(2) - 1)
    def _():
        y = x_ref[...] + gate_ref[...] * (acc_ref[...] + b2_ref[...])
        if final_norm:
            ms = jnp.mean(y * y, axis=-1, keepdims=True)
            y = y * lax.rsqrt(ms + RMS_EPS) * fg_ref[...]
        out_ref[...] = y


def _mlp(x, mod5, gains, w1, b1, w2, b2, final_g, layer, tm, tf, final_norm):
    b, s, d = x.shape
    dff = w1.shape[-1]
    kern = functools.partial(_mlp_kernel, tm=tm, final_norm=final_norm)
    tok = pl.BlockSpec((None, tm, d), lambda bb, i, j: (bb, i, 0))
    return pl.pallas_call(
        kern,
        out_shape=jax.ShapeDtypeStruct((b, s, d), F32),
        grid=(b, s // tm, dff // tf),
        in_specs=[
            tok,
            _row_in(layer, d, 3),
            _mod_in(layer, 3, d, 3),
            _mod_in(layer, 4, d, 3),
            _mod_in(layer, 5, d, 3),
            pl.BlockSpec((None, d, tf), lambda bb, i, j: (layer, 0, j)),
            pl.BlockSpec((None, 1, tf), lambda bb, i, j: (layer, 0, j)),
            pl.BlockSpec((None, tf, d), lambda bb, i, j: (layer, j, 0)),
            _row_in(layer, d, 3),
            pl.BlockSpec((1, d), lambda bb, i, j: (0, 0)),
        ],
        out_specs=tok,
        scratch_shapes=[pltpu.VMEM((tm, d), BF16), pltpu.VMEM((tm, d), F32)] + _norm_scratch(tm, d),
        compiler_params=_params(3, 2),
        name="sqrelu_mlp",
    )(x, gains, mod5, mod5, mod5, w1, b1, w2, b2, final_g)


def _trunk(x, mod5, p):
    depth = p["w1"].shape[0]
    s = x.shape[1]
    cs_chan, w1c, tw_c, tw_s, cs2 = _dft_tables(s)
    for i in range(depth):
        sub = i // 2
        if i % 2 == 0:
            qkv = _norm_proj(x, mod5, p["norm1_g"], p["w_qkv"], p["perms"], i, sub, tm=1024, tn=768)
            outs, lses = zip(*[_group_attention(qkv, g, p["perms_t"]) for g in range(len(ATTN_WINDOWS))])
            x = _merge_proj(outs, lses, x, mod5, p["w_o"], i, sub, tm=512)
        else:
            if (s // FFT_N2) * ROW_TILE <= FUSED_STAGE1_ROWS:
                x = _fourier_short(x, mod5, p["norm1_g"], (cs_chan, w1c, tw_c, tw_s, cs2), p["w_f"], p["b_f"],
                                   i, sub)
            else:
                zr, zi = _chan_dft(x, mod5, p["norm1_g"], cs_chan, i, tm=512)
                ur, ui = _pos_dft1(zr, zi, w1c, tw_c, tw_s)
                x = _pos_dft2_proj(ur, ui, cs2, x, mod5, p["w_f"], p["b_f"], i, sub)
        x = _mlp(x, mod5, p["norm2_g"], p["w1"], p["b1"], p["w2"], p["b2"], p["final_g"], i,
                 tm=512, tf=1024, final_norm=(i == depth - 1))
    return x


def kernel(x_prompt, x_sample, c_prompt, c_sample, w_ada, b_ada, norm1_g, norm2_g, w_qkv, w_o, w_f, b_f,
           w1, b1, w2, b2, final_g):
    depth, d, _ = w_ada.shape
    n_p, n_s = c_prompt.shape[0], c_sample.shape[0]
    rows = -(-(n_p + n_s) // ROW_TILE) * ROW_TILE
    c_all = jnp.concatenate([c_prompt, c_sample, jnp.zeros((rows - n_p - n_s, d), F32)], axis=0)
    mod = _modulation(c_all, w_ada, b_ada)
    mod_p = mod[:, :n_p].reshape(depth, n_p, N_MOD, 1, d)
    mod_s = mod[:, n_p:n_p + n_s].reshape(depth, n_s, N_MOD, 1, d)

    row3 = lambda a: a.reshape(a.shape[0], 1, a.shape[-1])
    perms = _residue_perms()
    params = {
        "norm1_g": row3(norm1_g), "norm2_g": row3(norm2_g),
        "w_qkv": w_qkv.astype(BF16), "w_o": w_o.astype(BF16),
        "w_f": w_f.astype(BF16), "b_f": row3(b_f),
        "w1": w1.astype(BF16), "b1": row3(b1), "w2": w2.astype(BF16), "b2": row3(b2),
        "final_g": final_g.reshape(1, d),
        "perms": jnp.asarray(perms, BF16),
        "perms_t": jnp.asarray(np.transpose(perms, (0, 2, 1)), BF16),
    }
    return (_trunk(x_prompt, mod_p, params), _trunk(x_sample, mod_s, params))
```

```python
import functools

import numpy as np
import jax
import jax.numpy as jnp
from jax import lax
from jax.experimental import pallas as pl
from jax.experimental.pallas import tpu as pltpu

F32 = jnp.float32
BF16 = jnp.bfloat16

N_MOD = 6
RMS_EPS = 1e-6
MASK_VALUE = -1e30
ATTN_WINDOWS = ((128, 1), (512, 4), (2048, 16))
HEADS_PER_GROUP = 6
HEAD_DIM = 128
N_ATTN_HEADS = len(ATTN_WINDOWS) * HEADS_PER_GROUP
GROUP_WIDTH = HEADS_PER_GROUP * HEAD_DIM
FOURIER_GROUP_DIM = 256
HALF = 64
FFT_N2 = 128
LANES = 128
ROW_TILE = 16
PERM_BLOCK = 256
VMEM_LIMIT = 56 * 1024 * 1024


def _params(n_axes, n_parallel):
    sem = ("parallel",) * n_parallel + ("arbitrary",) * (n_axes - n_parallel)
    return pltpu.CompilerParams(dimension_semantics=sem, vmem_limit_bytes=VMEM_LIMIT)


def _const_spec(shape, n_grid):
    zeros = (0,) * len(shape)
    imap = (lambda a, b: zeros) if n_grid == 2 else (lambda a, b, c: zeros)
    return pl.BlockSpec(shape, imap, pipeline_mode=pl.Buffered(1))


NORM_CHUNK = 64


def _norm_scratch(tm, d):
    return [pltpu.VMEM((tm, LANES), F32), pltpu.VMEM((2, d), F32)]


def _ada_norm_rows(x_ref, gain, shift, scale, h_ref, rs_ref, ab_ref, rows, load=None):
    d = h_ref.shape[-1]
    slabs = d // LANES
    if load is None:
        load = lambda c, cols: x_ref[pl.ds(pl.multiple_of(c * NORM_CHUNK, NORM_CHUNK), NORM_CHUNK), cols]
    ab_ref[0:1, :] = gain * (1.0 + scale)
    ab_ref[1:2, :] = shift

    def stats(c, carry):
        r0 = pl.multiple_of(c * NORM_CHUNK, NORM_CHUNK)
        acc = jnp.zeros((NORM_CHUNK, LANES), F32)
        for t in range(slabs):
            xt = load(c, slice(t * LANES, (t + 1) * LANES))
            acc = acc + xt * xt
        ms = jnp.sum(acc, axis=-1, keepdims=True) * (1.0 / d)
        rs_ref[pl.ds(r0, NORM_CHUNK), :] = jnp.broadcast_to(lax.rsqrt(ms + RMS_EPS), (NORM_CHUNK, LANES))
        return carry

    lax.fori_loop(0, rows // NORM_CHUNK, stats, 0, unroll=4)

    def apply(c, carry):
        r0 = pl.multiple_of(c * NORM_CHUNK, NORM_CHUNK)
        rs = rs_ref[pl.ds(r0, NORM_CHUNK), :]
        for t in range(slabs):
            cols = slice(t * LANES, (t + 1) * LANES)
            h = load(c, cols) * rs * ab_ref[0:1, cols] + ab_ref[1:2, cols]
            h_ref[pl.ds(r0, NORM_CHUNK), cols] = h.astype(h_ref.dtype)
        return carry

    lax.fori_loop(0, rows // NORM_CHUNK, apply, 0)


def _ada_norm_chunks(x_ref, gain, shift, scale, h_ref, row0, rows, keep_ref=None):
    d = h_ref.shape[-1]
    slabs = d // LANES
    a = gain * (1.0 + scale)
    for c in range(rows // NORM_CHUNK):
        r0 = pl.multiple_of(row0 + c * NORM_CHUNK, NORM_CHUNK)
        acc = jnp.zeros((NORM_CHUNK, LANES), F32)
        for t in range(slabs):
            xt = x_ref[pl.ds(r0, NORM_CHUNK), t * LANES:(t + 1) * LANES]
            acc = acc + xt * xt
        ms = jnp.sum(acc, axis=-1, keepdims=True) * (1.0 / d)
        rs = jnp.broadcast_to(lax.rsqrt(ms + RMS_EPS), (NORM_CHUNK, LANES))
        for t in range(slabs):
            cols = slice(t * LANES, (t + 1) * LANES)
            xt = x_ref[pl.ds(r0, NORM_CHUNK), cols]
            if keep_ref is not None:
                keep_ref[pl.ds(r0, NORM_CHUNK), cols] = xt
            h_ref[pl.ds(r0, NORM_CHUNK), cols] = (xt * rs * a[:, cols] + shift[:, cols]).astype(h_ref.dtype)


def _mod_kernel(c_ref, w_ref, b_ref, o_ref):
    c = c_ref[...]
    act = (c * jax.nn.sigmoid(c)).astype(BF16)
    w = w_ref[...].astype(BF16)
    o_ref[...] = jnp.dot(act, w, preferred_element_type=F32) + b_ref[...]


def _modulation(c_all, w_ada, b_ada):
    depth, d, n = w_ada.shape
    rows = c_all.shape[0]
    tn = 1024
    return pl.pallas_call(
        _mod_kernel,
        out_shape=jax.ShapeDtypeStruct((depth, rows, n), F32),
        grid=(depth, n // tn),
        in_specs=[
            pl.BlockSpec((rows, d), lambda l, j: (0, 0)),
            pl.BlockSpec((None, d, tn), lambda l, j: (l, 0, j)),
            pl.BlockSpec((None, 1, tn), lambda l, j: (l, 0, j)),
        ],
        out_specs=pl.BlockSpec((None, rows, tn), lambda l, j: (l, 0, j)),
        compiler_params=_params(2, 2),
        name="adaln_modulation",
    )(c_all, w_ada, b_ada.reshape(depth, 1, n))


def _mod_in(layer, which, d, n_grid):
    if n_grid == 2:
        imap = lambda b, i: (layer, b, which, 0, 0)
    else:
        imap = lambda b, i, j: (layer, b, which, 0, 0)
    return pl.BlockSpec((None, None, None, 1, d), imap)


def _row_in(layer, d, n_grid):
    if n_grid == 2:
        imap = lambda b, i: (layer, 0, 0)
    else:
        imap = lambda b, i, j: (layer, 0, 0)
    return pl.BlockSpec((None, 1, d), imap)


def _residue_perms():
    mats = []
    for _, dil in ATTN_WINDOWS[1:]:
        ub = PERM_BLOCK // dil
        p = np.zeros((PERM_BLOCK, PERM_BLOCK), np.float32)
        nat = np.arange(PERM_BLOCK)
        p[(nat % dil) * ub + nat // dil, nat] = 1.0
        mats.append(p)
    return np.stack(mats)


def _norm_proj_kernel(xa_ref, xb_ref, g_ref, sh_ref, sc_ref, shn_ref, scn_ref, w_ref, p_ref, o_ref,
                      h0_ref, h1_ref, hp_ref, rs_ref, ab_ref, *, tm, steps_per_group):
    i, j = pl.program_id(1), pl.program_id(2)
    tile = pl.program_id(0) * pl.num_programs(1) + i
    half = tm // 2
    spg = steps_per_group
    gain = g_ref[...]
    rows_per_step = -(-half // (spg * NORM_CHUNK)) * NORM_CHUNK

    @pl.when((tile == 0) & (j == 0))
    def _():
        _ada_norm_rows(xa_ref, gain, sh_ref[...], sc_ref[...], h0_ref.at[0:half], rs_ref, ab_ref, half)
        _ada_norm_rows(xb_ref, gain, sh_ref[...], sc_ref[...], h0_ref.at[half:tm], rs_ref, ab_ref, half)

    def norm_ahead(x_ref, h_rows, step):
        row0 = jnp.minimum(step * rows_per_step, half - rows_per_step)
        _ada_norm_chunks(x_ref, gain, shn_ref[...], scn_ref[...], h_rows, row0, rows_per_step)

    def project(lhs_ref):
        o_ref[...] = jnp.dot(lhs_ref[...], w_ref[...], preferred_element_type=F32).astype(o_ref.dtype)

    def column_step(h_cur, h_next):
        @pl.when((j >= spg) & (j % spg == 0))
        def _():
            for blk in range(tm // PERM_BLOCK):
                rows = slice(blk * PERM_BLOCK, (blk + 1) * PERM_BLOCK)
                hp_ref[rows, :] = jnp.dot(p_ref[...], h_cur[rows, :],
                                          preferred_element_type=F32).astype(hp_ref.dtype)

        @pl.when(j < spg)
        def _():
            project(h_cur)

        @pl.when((j >= spg) & (j < 2 * spg))
        def _():
            norm_ahead(xa_ref, h_next.at[0:half], j - spg)
            project(hp_ref)

        @pl.when(j >= 2 * spg)
        def _():
            norm_ahead(xb_ref, h_next.at[half:tm], j - 2 * spg)
            project(hp_ref)

    @pl.when(tile % 2 == 0)
    def _():
        column_step(h0_ref, h1_ref)

    @pl.when(tile % 2 == 1)
    def _():
        column_step(h1_ref, h0_ref)


def _norm_proj(x, mod5, gains, w, perms, layer, w_layer, tm, tn):
    b, s, d = x.shape
    n = w.shape[-1]
    n_i = s // tm
    spg = (n // len(ATTN_WINDOWS)) // tn
    assert n // tn == len(ATTN_WINDOWS) * spg
    kern = functools.partial(_norm_proj_kernel, tm=tm, steps_per_group=spg)

    def tile_ahead(bb, i, ahead):
        t1 = jnp.minimum(bb * n_i + i + ahead, b * n_i - 1)
        return t1 // n_i, t1 % n_i

    def x_half(which, first_step):
        def imap(bb, i, j):
            b1, i1 = tile_ahead(bb, i, (j >= first_step).astype(jnp.int32))
            return b1, 2 * i1 + which, 0
        return pl.BlockSpec((None, tm // 2, d), imap)

    mod_next = lambda which: pl.BlockSpec((None, None, None, 1, d),
                                          lambda bb, i, j: (layer, tile_ahead(bb, i, 1)[0], which, 0, 0))
    return pl.pallas_call(
        kern,
        out_shape=jax.ShapeDtypeStruct((b, s, n), BF16),
        grid=(b, n_i, n // tn),
        in_specs=[
            x_half(0, spg),
            x_half(1, 2 * spg),
            _row_in(layer, d, 3),
            _mod_in(layer, 0, d, 3),
            _mod_in(layer, 1, d, 3),
            mod_next(0),
            mod_next(1),
            pl.BlockSpec((None, d, tn), lambda bb, i, j: (w_layer, 0, j)),
            pl.BlockSpec((None, PERM_BLOCK, PERM_BLOCK),
                         lambda bb, i, j: (jnp.maximum(j // spg, 1) - 1, 0, 0)),
        ],
        out_specs=pl.BlockSpec((None, tm, tn), lambda bb, i, j: (bb, i, j)),
        scratch_shapes=[pltpu.VMEM((tm, d), BF16)] * 3 + _norm_scratch(tm // 2, d),
        compiler_params=_params(3, 0),
        name="norm_qkv_proj",
    )(x, x, gains, mod5, mod5, mod5, mod5, w, perms)


def _attn_kernel(q_ref, kp_ref, km_ref, kn_ref, vp_ref, vm_ref, vn_ref, pt_ref, o_ref, lse_ref,
                 kc_ref, vc_ref, op_ref, lp_ref, *, nblk, ub, n_u, dilation, slopes):
    i = pl.program_id(1)
    tq = nblk * ub
    width = q_ref.shape[-1]
    sub = 2 * HALF
    span = sub + 2 * HALF
    row = lax.broadcasted_iota(jnp.int32, (sub, span), 0)
    col = lax.broadcasted_iota(jnp.int32, (sub, span), 1)
    adu = jnp.abs(col - HALF - row)
    band = adu <= HALF
    dist = (adu * dilation).astype(F32)
    lane = lax.broadcasted_iota(jnp.int32, (sub, LANES), 1)
    scale = HEAD_DIM ** -0.5
    blocks_per_sub = sub // ub

    def one_class(r):
        def gather(dst, prev, main, nxt):
            dst[0:HALF, :] = prev[:, r].reshape(HALF, width)
            dst[HALF:HALF + tq, :] = main[:, r].reshape(tq, width)
            dst[HALF + tq:, :] = nxt[:, r].reshape(HALF, width)

        gather(kc_ref, kp_ref, km_ref, kn_ref)
        gather(vc_ref, vp_ref, vm_ref, vn_ref)
        for sb in range(tq // sub):
            u_key = i * tq + (sb * sub - HALF) + col
            valid = band & (u_key >= 0) & (u_key < n_u)
            lse_tile = jnp.zeros((sub, LANES), F32)
            rows = slice(sb * sub, (sb + 1) * sub)
            for h in range(HEADS_PER_GROUP):
                cs = slice(h * HEAD_DIM, (h + 1) * HEAD_DIM)
                q = q_ref[sb * blocks_per_sub:(sb + 1) * blocks_per_sub, r, :, cs].reshape(sub, HEAD_DIM)
                k = kc_ref[sb * sub:sb * sub + span, cs]
                v = vc_ref[sb * sub:sb * sub + span, cs]
                s = lax.dot_general(q, k, (((1,), (1,)), ((), ())), preferred_element_type=F32) * scale
                s = s - slopes[h] * dist
                s = jnp.where(valid, s, MASK_VALUE)
                m = jnp.max(s, axis=-1, keepdims=True)
                p = jnp.exp(s - m)
                l = jnp.sum(p, axis=-1, keepdims=True)
                o = jnp.dot(p.astype(BF16), v, preferred_element_type=F32) * (1.0 / l)
                if dilation == 1:
                    o_ref[rows, cs] = o.astype(o_ref.dtype)
                else:
                    op_ref[r, rows, cs] = o.astype(op_ref.dtype)
                lse_tile = jnp.where(lane == h, m + jnp.log(l), lse_tile)
            if dilation == 1:
                lse_ref[rows, :] = lse_tile
            else:
                lp_ref[r, rows, :] = lse_tile

    if dilation == 1:
        one_class(0)
        return

    def class_step(r, carry):
        one_class(r)
        return carry

    lax.fori_loop(0, dilation, class_step, 0, unroll=4)
    pt = pt_ref[...]
    for blk in range(nblk):
        urows = slice(blk * ub, (blk + 1) * ub)
        nat = slice(blk * PERM_BLOCK, (blk + 1) * PERM_BLOCK)
        ob = jnp.concatenate([op_ref[rr, urows, :] for rr in range(dilation)], axis=0)
        o_ref[nat, :] = jnp.dot(pt, ob, preferred_element_type=F32).astype(o_ref.dtype)
        lb = jnp.concatenate([lp_ref[rr, urows, :] for rr in range(dilation)], axis=0)
        hi = lb.astype(BF16)
        rest = lb - hi.astype(F32)
        mid = rest.astype(BF16)
        lo = (rest - mid.astype(F32)).astype(BF16)
        lse_ref[nat, :] = (jnp.dot(pt, hi, preferred_element_type=F32)
                           + jnp.dot(pt, mid, preferred_element_type=F32)
                           + jnp.dot(pt, lo, preferred_element_type=F32))


def _group_attention(qkv, group, perms_t):
    b, s, width = qkv.shape
    _, dilation = ATTN_WINDOWS[group]
    n_u = s // dilation
    ub = HALF if dilation == 1 else PERM_BLOCK // dilation
    rb = ub * dilation
    tq = min({1: 512, 4: 256, 16: 128}[dilation], n_u)
    nblk = tq // ub
    hb = HALF // ub
    n_slabs = width // GROUP_WIDTH
    view = qkv.reshape(b, s // rb, dilation, ub, width)
    n_halo = (s // rb) // hb
    slopes = tuple(float(np.exp2(np.float32(-8.0 * (group * HEADS_PER_GROUP + h + 1) / N_ATTN_HEADS)))
                   for h in range(HEADS_PER_GROUP))

    def main(t):
        return pl.BlockSpec((None, nblk, dilation, ub, GROUP_WIDTH),
                            lambda bb, i: (bb, i, 0, 0, group * 3 + t))

    def halo_p(t):
        return pl.BlockSpec((None, hb, dilation, ub, GROUP_WIDTH),
                            lambda bb, i: (bb, jnp.maximum(i * (nblk // hb) - 1, 0), 0, 0, group * 3 + t))

    def halo_n(t):
        return pl.BlockSpec((None, hb, dilation, ub, GROUP_WIDTH),
                            lambda bb, i: (bb, jnp.minimum((i + 1) * (nblk // hb), n_halo - 1), 0, 0,
                                           group * 3 + t))

    assert nblk % hb == 0 and width == n_slabs * GROUP_WIDTH
    pt = perms_t[max(group - 1, 0)]
    kern = functools.partial(_attn_kernel, nblk=nblk, ub=ub, n_u=n_u, dilation=dilation, slopes=slopes)
    scratch = [pltpu.VMEM((tq + 2 * HALF, GROUP_WIDTH), BF16), pltpu.VMEM((tq + 2 * HALF, GROUP_WIDTH), BF16),
               pltpu.VMEM((dilation, tq, GROUP_WIDTH), BF16), pltpu.VMEM((dilation, tq, LANES), F32)]
    rows = nblk * rb
    return pl.pallas_call(
        kern,
        out_shape=(jax.ShapeDtypeStruct((b, s, GROUP_WIDTH), BF16),
                   jax.ShapeDtypeStruct((b, s, LANES), F32)),
        grid=(b, s // rows),
        in_specs=[main(0), halo_p(1), main(1), halo_n(1), halo_p(2), main(2), halo_n(2),
                  _const_spec(pt.shape, 2)],
        out_specs=(pl.BlockSpec((None, rows, GROUP_WIDTH), lambda bb, i: (bb, i, 0)),
                   pl.BlockSpec((None, rows, LANES), lambda bb, i: (bb, i, 0))),
        scratch_shapes=scratch,
        compiler_params=_params(2, 2),
        name=f"dilated_attention_g{group}",
    )(view, view, view, view, view, view, view, pt)


def _merge_proj_kernel(o0_ref, o1_ref, o2_ref, l0_ref, l1_ref, l2_ref, x_ref, gate_ref, w_ref,
                       out_ref, mix_ref):
    l0, l1, l2 = l0_ref[...], l1_ref[...], l2_ref[...]
    m = jnp.maximum(jnp.maximum(l0, l1), l2)
    e0, e1, e2 = jnp.exp(l0 - m), jnp.exp(l1 - m), jnp.exp(l2 - m)
    inv = 1.0 / (e0 + e1 + e2)
    for g, (o_ref, e) in enumerate(((o0_ref, e0), (o1_ref, e1), (o2_ref, e2))):
        alpha = e * inv
        for h in range(HEADS_PER_GROUP):
            src = slice(h * HEAD_DIM, (h + 1) * HEAD_DIM)
            dst = slice((g * HEADS_PER_GROUP + h) * HEAD_DIM, (g * HEADS_PER_GROUP + h + 1) * HEAD_DIM)
            mix_ref[:, dst] = (alpha[:, h:h + 1] * o_ref[:, src].astype(F32)).astype(mix_ref.dtype)
    y = jnp.dot(mix_ref[...], w_ref[...], preferred_element_type=F32)
    out_ref[...] = x_ref[...] + gate_ref[...] * y


def _merge_proj(outs, lses, x, mod5, w_o, layer, w_layer, tm):
    b, s, d = x.shape
    width = w_o.shape[1]
    tok = lambda c: pl.BlockSpec((None, tm, c), lambda bb, i: (bb, i, 0))
    return pl.pallas_call(
        _merge_proj_kernel,
        out_shape=jax.ShapeDtypeStruct((b, s, d), F32),
        grid=(b, s // tm),
        in_specs=[tok(GROUP_WIDTH)] * 3 + [tok(LANES)] * 3 + [
            tok(d),
            _mod_in(layer, 2, d, 2),
            pl.BlockSpec((None, width, d), lambda bb, i: (w_layer, 0, 0)),
        ],
        out_specs=tok(d),
        scratch_shapes=[pltpu.VMEM((tm, width), BF16)],
        compiler_params=_params(2, 2),
        name="attn_merge_out_proj",
    )(*outs, *lses, x, mod5, w_o)


K1_GROUP = 8
FUSED_STAGE1_ROWS = 256
STAGE2_CHUNK = 1024


def _dft_tables(s):
    n2 = FFT_N2
    n1 = s // n2
    c = FOURIER_GROUP_DIM
    ang_c = 2.0 * np.pi * np.outer(np.arange(c), np.arange(c)) / c
    cs_chan = np.concatenate([np.cos(ang_c), np.sin(ang_c)], axis=1) / np.sqrt(c)
    a1 = 2.0 * np.pi * np.outer(np.arange(n1), np.arange(n1)) / n1
    c1, s1 = np.cos(a1), np.sin(a1)
    w1 = np.kron(np.block([[c1, s1], [-s1, c1]]), np.eye(ROW_TILE))
    s2 = (np.arange(n2 // ROW_TILE)[:, None, None] * ROW_TILE + np.arange(ROW_TILE)[None, None, :])
    th = 2.0 * np.pi * np.arange(n1)[None, :, None] * s2 / s
    th = th.reshape(n2 // ROW_TILE, n1 * ROW_TILE, 1)
    a2 = 2.0 * np.pi * np.outer(np.arange(n2), np.arange(n2)) / n2
    cs2 = np.concatenate([np.cos(a2), np.sin(a2)], axis=1)
    return (jnp.asarray(cs_chan, BF16), jnp.asarray(w1, BF16),
            jnp.asarray(np.cos(th), F32), jnp.asarray(np.sin(th), F32), jnp.asarray(cs2, BF16))


def _chan_dft_kernel(x_ref, g_ref, sh_ref, sc_ref, cs_ref, zr_ref, zi_ref, h_ref, rs_ref, ab_ref, *, tm):
    _ada_norm_rows(x_ref, g_ref[...], sh_ref[...], sc_ref[...], h_ref, rs_ref, ab_ref, tm)
    c = FOURIER_GROUP_DIM
    for g in range(x_ref.shape[-1] // c):
        cols = slice(g * c, (g + 1) * c)
        r = jnp.dot(h_ref[:, cols], cs_ref[...], preferred_element_type=F32)
        zr_ref[:, cols] = r[:, :c].astype(zr_ref.dtype)
        zi_ref[:, cols] = (-r[:, c:]).astype(zi_ref.dtype)


def _chan_dft(x, mod5, gains, cs_chan, layer, tm):
    b, s, d = x.shape
    tok = pl.BlockSpec((None, tm, d), lambda bb, i: (bb, i, 0))
    kern = functools.partial(_chan_dft_kernel, tm=tm)
    return pl.pallas_call(
        kern,
        out_shape=(jax.ShapeDtypeStruct((b, s, d), BF16),) * 2,
        grid=(b, s // tm),
        in_specs=[tok, _row_in(layer, d, 2), _mod_in(layer, 0, d, 2), _mod_in(layer, 1, d, 2),
                  _const_spec(cs_chan.shape, 2)],
        out_specs=(tok, tok),
        scratch_shapes=[pltpu.VMEM((tm, d), BF16)] + _norm_scratch(tm, d),
        compiler_params=_params(2, 2),
        name="fourier_channel_dft",
    )(x, gains, mod5, mod5, cs_chan)


def _pos_dft1_kernel(zr_ref, zi_ref, w_ref, tc_ref, ts_ref, ur_ref, ui_ref, *, rows):
    tc = zr_ref.shape[-1]
    z = jnp.concatenate([zr_ref[...].reshape(rows, tc), zi_ref[...].reshape(rows, tc)], axis=0)
    t = jnp.dot(w_ref[...], z, preferred_element_type=F32)
    tr, ti = t[:rows], t[rows:]
    c, sn = tc_ref[...], ts_ref[...]
    ur_ref[...] = (tr * c + ti * sn).astype(ur_ref.dtype).reshape(ur_ref.shape)
    ui_ref[...] = (ti * c - tr * sn).astype(ui_ref.dtype).reshape(ui_ref.shape)


def _pos_dft1(zr, zi, w1, tw_c, tw_s):
    b, s, d = zr.shape
    n2 = FFT_N2
    n1 = s // n2
    rows = n1 * ROW_TILE
    tc = min(d, (1024 * 1024) // rows)
    view = lambda a: a.reshape(b, n1, n2 // ROW_TILE, ROW_TILE, d)
    blk = pl.BlockSpec((None, n1, None, ROW_TILE, tc), lambda bb, j, c: (bb, 0, j, 0, c))
    tw = pl.BlockSpec((None, rows, 1), lambda bb, j, c: (j, 0, 0))
    kern = functools.partial(_pos_dft1_kernel, rows=rows)
    ur, ui = pl.pallas_call(
        kern,
        out_shape=(jax.ShapeDtypeStruct((b, n1, n2 // ROW_TILE, ROW_TILE, d), BF16),) * 2,
        grid=(b, n2 // ROW_TILE, d // tc),
        in_specs=[blk, blk, _const_spec(w1.shape, 3), tw, tw],
        out_specs=(blk, blk),
        compiler_params=_params(3, 3),
        name="fourier_pos_dft_stage1",
    )(view(zr), view(zi), w1, tw_c, tw_s)
    return ur.reshape(b, s, d), ui.reshape(b, s, d)


def _fourier_short_kernel(xa_ref, xb_ref, g_ref, sh_ref, sc_ref, gate_ref, cs_ref, w1_ref, tc_ref, ts_ref,
                          cs2_ref, wf_ref, bf_ref, out_ref, ur_ref, ui_ref, yb_ref, *, n1, n_q, inv_norm):
    t = pl.program_id(1)
    n2 = cs2_ref.shape[0]
    d = yb_ref.shape[-1]
    rows = n1 * ROW_TILE
    n_tiles = n2 // ROW_TILE

    @pl.when(t < n_tiles)
    def _():
        def stage1(h_ref, z_ref, rs_ref, ab_ref):
            groups = NORM_CHUNK // ROW_TILE
            load = lambda c, cols: xa_ref[pl.ds(pl.multiple_of(c * groups, groups), groups), :, cols].reshape(
                NORM_CHUNK, LANES)
            _ada_norm_rows(xa_ref, g_ref[...], sh_ref[...], sc_ref[...], h_ref, rs_ref, ab_ref, rows, load)
            c = FOURIER_GROUP_DIM
            for g in range(d // c):
                cols = slice(g * c, (g + 1) * c)
                r = jnp.dot(h_ref[:, cols], cs_ref[...], preferred_element_type=F32)
                z_ref[0:rows, cols] = r[:, :c].astype(z_ref.dtype)
                z_ref[rows:, cols] = (-r[:, c:]).astype(z_ref.dtype)
            tt = jnp.dot(w1_ref[...], z_ref[...], preferred_element_type=F32)
            tr, ti = tt[:rows], tt[rows:]
            cw, sw = tc_ref[...], ts_ref[...]
            s2 = pl.ds(pl.multiple_of(t * ROW_TILE, ROW_TILE), ROW_TILE)
            ur_ref[:, s2, :] = (tr * cw + ti * sw).astype(ur_ref.dtype).reshape(n1, ROW_TILE, d)
            ui_ref[:, s2, :] = (ti * cw - tr * sw).astype(ui_ref.dtype).reshape(n1, ROW_TILE, d)

        pl.run_scoped(stage1, pltpu.VMEM((rows, d), BF16), pltpu.VMEM((2 * rows, d), BF16),
                      *_norm_scratch(rows, d))

    @pl.when(t >= n_tiles)
    def _():
        tb = t - n_tiles
        a, q = tb // n_q, tb % n_q

        @pl.when(q == 0)
        def _():
            def interleave(ys_ref):
                chunk_slabs = ys_ref.shape[0]
                chunk = chunk_slabs * LANES
                for cc in range(d // chunk):
                    cols = slice(cc * chunk, (cc + 1) * chunk)
                    for k in range(K1_GROUP):
                        k1 = a * K1_GROUP + k
                        u = jnp.concatenate([ur_ref[k1, :, cols], ui_ref[k1, :, cols]], axis=0)
                        y = jnp.dot(cs2_ref[...], u, preferred_element_type=F32) * inv_norm
                        for sl in range(chunk_slabs):
                            ys_ref[sl, pl.ds(k, n2, stride=K1_GROUP), :] = y[:, sl * LANES:(sl + 1) * LANES]
                    for sl in range(chunk_slabs):
                        c0 = cc * chunk + sl * LANES
                        yb_ref[:, c0:c0 + LANES] = ys_ref[sl].astype(yb_ref.dtype)

            pl.run_scoped(interleave, pltpu.VMEM((STAGE2_CHUNK // LANES, K1_GROUP * n2, LANES), F32))

        step_rows = xb_ref.shape[0] * K1_GROUP
        r0 = pl.multiple_of(q * step_rows, step_rows)
        proj = jnp.dot(yb_ref[pl.ds(r0, step_rows), :], wf_ref[...], preferred_element_type=F32) + bf_ref[...]
        out = xb_ref[...].reshape(proj.shape) + gate_ref[...] * proj
        out_ref[...] = out.reshape(out_ref.shape)


def _fourier_short(x, mod5, gains, tables, w_f, b_f, layer, w_layer):
    cs_chan, w1, tw_c, tw_s, cs2 = tables
    b, s, d = x.shape
    n2 = FFT_N2
    n1 = s // n2
    n_tiles = n2 // ROW_TILE
    n_a = n1 // K1_GROUP
    n_q = 4
    rows = n1 * ROW_TILE
    xa_view = x.reshape(b, n1, n_tiles, ROW_TILE, d)
    xb_view = x.reshape(b, n2, n_a, K1_GROUP, d)
    tile = lambda t: jnp.minimum(t, n_tiles - 1)
    late = lambda t: jnp.maximum(t - n_tiles, 0)
    xa_blk = pl.BlockSpec((None, n1, None, ROW_TILE, d), lambda bb, t: (bb, 0, tile(t), 0, 0))
    xb_blk = pl.BlockSpec((None, n2 // n_q, None, K1_GROUP, d),
                          lambda bb, t: (bb, late(t) % n_q, late(t) // n_q, 0, 0))
    tw = pl.BlockSpec((None, rows, 1), lambda bb, t: (tile(t), 0, 0))
    kern = functools.partial(_fourier_short_kernel, n1=n1, n_q=n_q, inv_norm=float(1.0 / np.sqrt(s)))
    out = pl.pallas_call(
        kern,
        out_shape=jax.ShapeDtypeStruct(xb_view.shape, F32),
        grid=(b, n_tiles + n_a * n_q),
        in_specs=[xa_blk, xb_blk, _row_in(layer, d, 2), _mod_in(layer, 0, d, 2), _mod_in(layer, 1, d, 2),
                  _mod_in(layer, 2, d, 2), _const_spec(cs_chan.shape, 2), _const_spec(w1.shape, 2), tw, tw,
                  _const_spec(cs2.shape, 2),
                  pl.BlockSpec((None, d, d), lambda bb, t: (w_layer, 0, 0), pipeline_mode=pl.Buffered(1)),
                  _row_in(w_layer, d, 2)],
        out_specs=xb_blk,
        scratch_shapes=[pltpu.VMEM((n1, n2, d), BF16), pltpu.VMEM((n1, n2, d), BF16),
                        pltpu.VMEM((K1_GROUP * n2, d), BF16)],
        compiler_params=_params(2, 1),
        name="fourier_mixer_short",
    )(xa_view, xb_view, gains, mod5, mod5, mod5, cs_chan, w1, tw_c, tw_s, cs2, w_f, b_f)
    return out.reshape(b, s, d)


def _pos_dft2_proj_kernel(ur_ref, ui_ref, cs2_ref, w_ref, b_ref, x_ref, gate_ref, out_ref, ys_ref, yb_ref, *,
                          inv_norm):
    n2 = cs2_ref.shape[0]
    d = ur_ref.shape[-1]
    chunk_slabs = ys_ref.shape[0]
    chunk = chunk_slabs * LANES

    @pl.when(pl.program_id(2) == 0)
    def _():
        for cc in range(d // chunk):
            for k in range(K1_GROUP):
                rows = slice(k * n2, (k + 1) * n2)
                cols = slice(cc * chunk, (cc + 1) * chunk)
                u = jnp.concatenate([ur_ref[rows, cols], ui_ref[rows, cols]], axis=0)
                y = jnp.dot(cs2_ref[...], u, preferred_element_type=F32) * inv_norm
                for sl in range(chunk_slabs):
                    ys_ref[sl, pl.ds(k, n2, stride=K1_GROUP), :] = y[:, sl * LANES:(sl + 1) * LANES]
            for sl in range(chunk_slabs):
                c0 = cc * chunk + sl * LANES
                yb_ref[:, c0:c0 + LANES] = ys_ref[sl].astype(yb_ref.dtype)

    step_rows = x_ref.shape[0] * K1_GROUP
    r0 = pl.multiple_of(pl.program_id(2) * step_rows, step_rows)
    proj = jnp.dot(yb_ref[pl.ds(r0, step_rows), :], w_ref[...], preferred_element_type=F32) + b_ref[...]
    out = x_ref[...].reshape(proj.shape) + gate_ref[...] * proj
    out_ref[...] = out.reshape(out_ref.shape)


def _pos_dft2_proj(ur, ui, cs2, x, mod5, w_f, b_f, layer, w_layer):
    b, s, d = x.shape
    n2 = FFT_N2
    n1 = s // n2
    n_q = 2
    rows = K1_GROUP * n2
    n_a = n1 // K1_GROUP

    def u_blk(first_step):
        def imap(bb, a, q):
            nxt = jnp.minimum(bb * n_a + a + (q >= first_step).astype(jnp.int32), b * n_a - 1)
            return nxt // n_a, nxt % n_a, 0
        return pl.BlockSpec((None, rows, d), imap)

    x_view = x.reshape(b, n2, n_a, K1_GROUP, d)
    x_blk = pl.BlockSpec((None, n2 // n_q, None, K1_GROUP, d), lambda bb, a, q: (bb, q, a, 0, 0))
    kern = functools.partial(_pos_dft2_proj_kernel, inv_norm=float(1.0 / np.sqrt(s)))
    out = pl.pallas_call(
        kern,
        out_shape=jax.ShapeDtypeStruct(x_view.shape, F32),
        grid=(b, n_a, n_q),
        in_specs=[u_blk(1), u_blk(1), _const_spec(cs2.shape, 3),
                  pl.BlockSpec((None, d, d), lambda bb, a, q: (w_layer, 0, 0), pipeline_mode=pl.Buffered(1)),
                  _row_in(w_layer, d, 3), x_blk, _mod_in(layer, 2, d, 3)],
        out_specs=x_blk,
        scratch_shapes=[pltpu.VMEM((STAGE2_CHUNK // LANES, rows, LANES), F32), pltpu.VMEM((rows, d), BF16)],
        compiler_params=_params(3, 2),
        name="fourier_pos_dft_stage2_proj",
    )(ur, ui, cs2, w_f, b_f, x_view, mod5)
    return out.reshape(b, s, d)


def _mlp_kernel(xn_ref, g_ref, sh_ref, sc_ref, shn_ref, scn_ref, gate_ref, w1_ref, b1_ref, w2_ref, b2_ref, fg_ref,
                out_ref, h0_ref, h1_ref, xk0_ref, xk1_ref, *, tm, ahead_steps, final_norm):
    i, j = pl.program_id(1), pl.program_id(2)
    n_j = pl.num_programs(2)
    tile = pl.program_id(0) * pl.num_programs(1) + i
    gain = g_ref[...]
    rows_per_step = tm // ahead_steps

    @pl.when((tile == 0) & (j == 0))
    def _():
        _ada_norm_chunks(xn_ref, gain, sh_ref[...], sc_ref[...], h0_ref, 0, tm, keep_ref=xk0_ref)

    @pl.when(j == 0)
    def _():
        out_ref[...] = jnp.zeros_like(out_ref)

    def chunk_step(h_cur, h_next, xk_next, ahead):
        if ahead:
            _ada_norm_chunks(xn_ref, gain, shn_ref[...], scn_ref[...], h_next, (j - 1) * rows_per_step,
                             rows_per_step, keep_ref=xk_next)
        u = jnp.dot(h_cur[...], w1_ref[...], preferred_element_type=F32) + b1_ref[...]
        u = jnp.maximum(u, 0.0)
        out_ref[...] += jnp.dot((u * u).astype(BF16), w2_ref[...], preferred_element_type=F32)

    is_ahead = (j >= 1) & (j <= ahead_steps)
    for parity, (h_cur, h_next, xk_next) in enumerate(((h0_ref, h1_ref, xk1_ref), (h1_ref, h0_ref, xk0_ref))):
        for ahead in (True, False):
            @pl.when((tile % 2 == parity) & (is_ahead if ahead else jnp.logical_not(is_ahead)))
            def _(h_cur=h_cur, h_next=h_next, xk_next=xk_next, ahead=ahead):
                chunk_step(h_cur, h_next, xk_next, ahead)

    @pl.when(j == n_j - 1)
    def _():
        def finish(xk_ref):
            y = xk_ref[...] + gate_ref[...] * (out_ref[...] + b2_ref[...])
            if final_norm:
                ms = jnp.mean(y * y, axis=-1, keepdims=True)
                y = y * lax.rsqrt(ms + RMS_EPS) * fg_ref[...]
            out_ref[...] = y

        @pl.when(tile % 2 == 0)
        def _():
            finish(xk0_ref)

        @pl.when(tile % 2 == 1)
        def _():
            finish(xk1_ref)


def _mlp(x, mod5, gains, w1, b1, w2, b2, final_g, layer, tm, tf, final_norm):
    b, s, d = x.shape
    dff = w1.shape[-1]
    n_i, n_j = s // tm, dff // tf
    ahead_steps = 4
    assert n_j > ahead_steps and (tm // ahead_steps) % NORM_CHUNK == 0
    kern = functools.partial(_mlp_kernel, tm=tm, ahead_steps=ahead_steps, final_norm=final_norm)

    def tile_ahead(bb, i, ahead):
        t1 = jnp.minimum(bb * n_i + i + ahead, b * n_i - 1)
        return t1 // n_i, t1 % n_i

    tok_next = pl.BlockSpec((None, tm, d), lambda bb, i, j: (*tile_ahead(bb, i, jnp.minimum(j, 1)), 0))
    mod_next = lambda which: pl.BlockSpec((None, None, None, 1, d),
                                          lambda bb, i, j: (layer, tile_ahead(bb, i, 1)[0], which, 0, 0))
    return pl.pallas_call(
        kern,
        out_shape=jax.ShapeDtypeStruct((b, s, d), F32),
        grid=(b, n_i, n_j),
        in_specs=[
            tok_next,
            _row_in(layer, d, 3),
            _mod_in(layer, 3, d, 3),
            _mod_in(layer, 4, d, 3),
            mod_next(3),
            mod_next(4),
            _mod_in(layer, 5, d, 3),
            pl.BlockSpec((None, d, tf), lambda bb, i, j: (layer, 0, j)),
            pl.BlockSpec((None, 1, tf), lambda bb, i, j: (layer, 0, j)),
            pl.BlockSpec((None, tf, d), lambda bb, i, j: (layer, j, 0)),
            _row_in(layer, d, 3),
            pl.BlockSpec((1, d), lambda bb, i, j: (0, 0)),
        ],
        out_specs=pl.BlockSpec((None, tm, d), lambda bb, i, j: (bb, i, 0)),
        scratch_shapes=[pltpu.VMEM((tm, d), BF16)] * 2 + [pltpu.VMEM((tm, d), F32)] * 2,
        compiler_params=_params(3, 0),
        name="sqrelu_mlp",
    )(x, gains, mod5, mod5, mod5, mod5, mod5, w1, b1, w2, b2, final_g)


def _trunk(x, mod5, p):
    depth = p["w1"].shape[0]
    s = x.shape[1]
    cs_chan, w1c, tw_c, tw_s, cs2 = _dft_tables(s)
    for i in range(depth):
        sub = i // 2
        if i % 2 == 0:
            qkv = _norm_proj(x, mod5, p["norm1_g"], p["w_qkv"], p["perms"], i, sub, tm=1024, tn=768)
            outs, lses = zip(*[_group_attention(qkv, g, p["perms_t"]) for g in range(len(ATTN_WINDOWS))])
            x = _merge_proj(outs, lses, x, mod5, p["w_o"], i, sub, tm=512)
        else:
            if (s // FFT_N2) * ROW_TILE <= FUSED_STAGE1_ROWS:
                x = _fourier_short(x, mod5, p["norm1_g"], (cs_chan, w1c, tw_c, tw_s, cs2), p["w_f"], p["b_f"],
                                   i, sub)
            else:
                zr, zi = _chan_dft(x, mod5, p["norm1_g"], cs_chan, i, tm=512)
                ur, ui = _pos_dft1(zr, zi, w1c, tw_c, tw_s)
                x = _pos_dft2_proj(ur, ui, cs2, x, mod5, p["w_f"], p["b_f"], i, sub)
        x = _mlp(x, mod5, p["norm2_g"], p["w1"], p["b1"], p["w2"], p["b2"], p["final_g"], i,
                 tm=512, tf=1024, final_norm=(i == depth - 1))
    return x


def kernel(x_prompt, x_sample, c_prompt, c_sample, w_ada, b_ada, norm1_g, norm2_g, w_qkv, w_o, w_f, b_f,
           w1, b1, w2, b2, final_g):
    depth, d, _ = w_ada.shape
    n_p, n_s = c_prompt.shape[0], c_sample.shape[0]
    rows = -(-(n_p + n_s) // ROW_TILE) * ROW_TILE
    c_all = jnp.concatenate([c_prompt, c_sample, jnp.zeros((rows - n_p - n_s, d), F32)], axis=0)
    mod = _modulation(c_all, w_ada, b_ada)
    mod_p = mod[:, :n_p].reshape(depth, n_p, N_MOD, 1, d)
    mod_s = mod[:, n_p:n_p + n_s].reshape(depth, n_s, N_MOD, 1, d)

    row3 = lambda a: a.reshape(a.shape[0], 1, a.shape[-1])
    perms = _residue_perms()
    params = {
        "norm1_g": row3(norm1_g), "norm2_g": row3(norm2_g),
        "w_qkv": w_qkv.astype(BF16), "w_o": w_o.astype(BF16),
        "w_f": w_f.astype(BF16), "b_f": row3(b_f),
        "w1": w1.astype(BF16), "b1": row3(b1), "w2": w2.astype(BF16), "b2": row3(b2),
        "final_g": final_g.reshape(1, d),
        "perms": jnp.asarray(perms, BF16),
        "perms_t": jnp.asarray(np.transpose(perms, (0, 2, 1)), BF16),
    }
    return (_trunk(x_prompt, mod_p, params), _trunk(x_sample, mod_s, params))
```

```python
import functools

import numpy as np
import jax
import jax.numpy as jnp
from jax import lax
from jax.experimental import pallas as pl
from jax.experimental.pallas import tpu as pltpu

F32 = jnp.float32
BF16 = jnp.bfloat16

N_MOD = 6
RMS_EPS = 1e-6
MASK_VALUE = -1e30
ATTN_WINDOWS = ((128, 1), (512, 4), (2048, 16))
HEADS_PER_GROUP = 6
HEAD_DIM = 128
N_ATTN_HEADS = len(ATTN_WINDOWS) * HEADS_PER_GROUP
GROUP_WIDTH = HEADS_PER_GROUP * HEAD_DIM
FOURIER_GROUP_DIM = 256
HALF = 64
FFT_N2 = 128
LANES = 128
ROW_TILE = 16
PERM_BLOCK = 256
VMEM_LIMIT = 56 * 1024 * 1024


def _params(n_axes, n_parallel):
    sem = ("parallel",) * n_parallel + ("arbitrary",) * (n_axes - n_parallel)
    return pltpu.CompilerParams(dimension_semantics=sem, vmem_limit_bytes=VMEM_LIMIT)


def _const_spec(shape, n_grid):
    zeros = (0,) * len(shape)
    imap = (lambda a, b: zeros) if n_grid == 2 else (lambda a, b, c: zeros)
    return pl.BlockSpec(shape, imap, pipeline_mode=pl.Buffered(1))


NORM_CHUNK = 64


def _norm_scratch(tm, d):
    return [pltpu.VMEM((tm, LANES), F32), pltpu.VMEM((2, d), F32)]


def _ada_norm_rows(x_ref, gain, shift, scale, h_ref, rs_ref, ab_ref, rows, load=None, zero_ref=None):
    d = h_ref.shape[-1]
    slabs = d // LANES
    if load is None:
        load = lambda c, cols: x_ref[pl.ds(pl.multiple_of(c * NORM_CHUNK, NORM_CHUNK), NORM_CHUNK), cols]
    ab_ref[0:1, :] = gain * (1.0 + scale)
    ab_ref[1:2, :] = shift

    def stats(c, carry):
        r0 = pl.multiple_of(c * NORM_CHUNK, NORM_CHUNK)
        acc = jnp.zeros((NORM_CHUNK, LANES), F32)
        for t in range(slabs):
            xt = load(c, slice(t * LANES, (t + 1) * LANES))
            acc = acc + xt * xt
        ms = jnp.sum(acc, axis=-1, keepdims=True) * (1.0 / d)
        rs_ref[pl.ds(r0, NORM_CHUNK), :] = jnp.broadcast_to(lax.rsqrt(ms + RMS_EPS), (NORM_CHUNK, LANES))
        return carry

    lax.fori_loop(0, rows // NORM_CHUNK, stats, 0, unroll=4)

    def apply(c, carry):
        r0 = pl.multiple_of(c * NORM_CHUNK, NORM_CHUNK)
        rs = rs_ref[pl.ds(r0, NORM_CHUNK), :]
        for t in range(slabs):
            cols = slice(t * LANES, (t + 1) * LANES)
            h = load(c, cols) * rs * ab_ref[0:1, cols] + ab_ref[1:2, cols]
            h_ref[pl.ds(r0, NORM_CHUNK), cols] = h.astype(h_ref.dtype)
            if zero_ref is not None:
                zero_ref[pl.ds(r0, NORM_CHUNK), cols] = jnp.zeros((NORM_CHUNK, LANES), zero_ref.dtype)
        return carry

    lax.fori_loop(0, rows // NORM_CHUNK, apply, 0)


def _ada_norm_chunks(x_ref, gain, shift, scale, h_ref, row0, rows):
    d = h_ref.shape[-1]
    slabs = d // LANES
    a = gain * (1.0 + scale)
    for c in range(rows // NORM_CHUNK):
        r0 = pl.multiple_of(row0 + c * NORM_CHUNK, NORM_CHUNK)
        acc = jnp.zeros((NORM_CHUNK, LANES), F32)
        for t in range(slabs):
            xt = x_ref[pl.ds(r0, NORM_CHUNK), t * LANES:(t + 1) * LANES]
            acc = acc + xt * xt
        ms = jnp.sum(acc, axis=-1, keepdims=True) * (1.0 / d)
        rs = jnp.broadcast_to(lax.rsqrt(ms + RMS_EPS), (NORM_CHUNK, LANES))
        for t in range(slabs):
            cols = slice(t * LANES, (t + 1) * LANES)
            h = x_ref[pl.ds(r0, NORM_CHUNK), cols] * rs * a[:, cols] + shift[:, cols]
            h_ref[pl.ds(r0, NORM_CHUNK), cols] = h.astype(h_ref.dtype)


def _mod_kernel(c_ref, w_ref, b_ref, o_ref):
    c = c_ref[...]
    act = (c * jax.nn.sigmoid(c)).astype(BF16)
    w = w_ref[...].astype(BF16)
    o_ref[...] = jnp.dot(act, w, preferred_element_type=F32) + b_ref[...]


def _modulation(c_all, w_ada, b_ada):
    depth, d, n = w_ada.shape
    rows = c_all.shape[0]
    tn = 1024
    return pl.pallas_call(
        _mod_kernel,
        out_shape=jax.ShapeDtypeStruct((depth, rows, n), F32),
        grid=(depth, n // tn),
        in_specs=[
            pl.BlockSpec((rows, d), lambda l, j: (0, 0)),
            pl.BlockSpec((None, d, tn), lambda l, j: (l, 0, j)),
            pl.BlockSpec((None, 1, tn), lambda l, j: (l, 0, j)),
        ],
        out_specs=pl.BlockSpec((None, rows, tn), lambda l, j: (l, 0, j)),
        compiler_params=_params(2, 2),
        name="adaln_modulation",
    )(c_all, w_ada, b_ada.reshape(depth, 1, n))


def _mod_in(layer, which, d, n_grid):
    if n_grid == 2:
        imap = lambda b, i: (layer, b, which, 0, 0)
    else:
        imap = lambda b, i, j: (layer, b, which, 0, 0)
    return pl.BlockSpec((None, None, None, 1, d), imap)


def _row_in(layer, d, n_grid):
    if n_grid == 2:
        imap = lambda b, i: (layer, 0, 0)
    else:
        imap = lambda b, i, j: (layer, 0, 0)
    return pl.BlockSpec((None, 1, d), imap)


def _residue_perms():
    mats = []
    for _, dil in ATTN_WINDOWS[1:]:
        ub = PERM_BLOCK // dil
        p = np.zeros((PERM_BLOCK, PERM_BLOCK), np.float32)
        nat = np.arange(PERM_BLOCK)
        p[(nat % dil) * ub + nat // dil, nat] = 1.0
        mats.append(p)
    return np.stack(mats)


def _norm_proj_kernel(xa_ref, xb_ref, g_ref, sh_ref, sc_ref, shn_ref, scn_ref, w_ref, p_ref, o_ref,
                      h0_ref, h1_ref, hp_ref, rs_ref, ab_ref, *, tm, steps_per_group):
    i, j = pl.program_id(1), pl.program_id(2)
    tile = pl.program_id(0) * pl.num_programs(1) + i
    half = tm // 2
    spg = steps_per_group
    gain = g_ref[...]
    rows_per_step = -(-half // (spg * NORM_CHUNK)) * NORM_CHUNK

    @pl.when((tile == 0) & (j == 0))
    def _():
        _ada_norm_rows(xa_ref, gain, sh_ref[...], sc_ref[...], h0_ref.at[0:half], rs_ref, ab_ref, half)
        _ada_norm_rows(xb_ref, gain, sh_ref[...], sc_ref[...], h0_ref.at[half:tm], rs_ref, ab_ref, half)

    def norm_ahead(x_ref, h_rows, step):
        row0 = jnp.minimum(step * rows_per_step, half - rows_per_step)
        _ada_norm_chunks(x_ref, gain, shn_ref[...], scn_ref[...], h_rows, row0, rows_per_step)

    def project(lhs_ref):
        o_ref[...] = jnp.dot(lhs_ref[...], w_ref[...], preferred_element_type=F32).astype(o_ref.dtype)

    def column_step(h_cur, h_next):
        @pl.when((j >= spg) & (j % spg == 0))
        def _():
            for blk in range(tm // PERM_BLOCK):
                rows = slice(blk * PERM_BLOCK, (blk + 1) * PERM_BLOCK)
                hp_ref[rows, :] = jnp.dot(p_ref[...], h_cur[rows, :],
                                          preferred_element_type=F32).astype(hp_ref.dtype)

        @pl.when(j < spg)
        def _():
            project(h_cur)

        @pl.when((j >= spg) & (j < 2 * spg))
        def _():
            norm_ahead(xa_ref, h_next.at[0:half], j - spg)
            project(hp_ref)

        @pl.when(j >= 2 * spg)
        def _():
            norm_ahead(xb_ref, h_next.at[half:tm], j - 2 * spg)
            project(hp_ref)

    @pl.when(tile % 2 == 0)
    def _():
        column_step(h0_ref, h1_ref)

    @pl.when(tile % 2 == 1)
    def _():
        column_step(h1_ref, h0_ref)


def _norm_proj(x, mod5, gains, w, perms, layer, w_layer, tm, tn):
    b, s, d = x.shape
    n = w.shape[-1]
    n_i = s // tm
    spg = (n // len(ATTN_WINDOWS)) // tn
    assert n // tn == len(ATTN_WINDOWS) * spg
    kern = functools.partial(_norm_proj_kernel, tm=tm, steps_per_group=spg)

    def tile_ahead(bb, i, ahead):
        t1 = jnp.minimum(bb * n_i + i + ahead, b * n_i - 1)
        return t1 // n_i, t1 % n_i

    def x_half(which, first_step):
        def imap(bb, i, j):
            b1, i1 = tile_ahead(bb, i, (j >= first_step).astype(jnp.int32))
            return b1, 2 * i1 + which, 0
        return pl.BlockSpec((None, tm // 2, d), imap)

    mod_next = lambda which: pl.BlockSpec((None, None, None, 1, d),
                                          lambda bb, i, j: (layer, tile_ahead(bb, i, 1)[0], which, 0, 0))
    return pl.pallas_call(
        kern,
        out_shape=jax.ShapeDtypeStruct((b, s, n), BF16),
        grid=(b, n_i, n // tn),
        in_specs=[
            x_half(0, spg),
            x_half(1, 2 * spg),
            _row_in(layer, d, 3),
            _mod_in(layer, 0, d, 3),
            _mod_in(layer, 1, d, 3),
            mod_next(0),
            mod_next(1),
            pl.BlockSpec((None, d, tn), lambda bb, i, j: (w_layer, 0, j)),
            pl.BlockSpec((None, PERM_BLOCK, PERM_BLOCK),
                         lambda bb, i, j: (jnp.maximum(j // spg, 1) - 1, 0, 0)),
        ],
        out_specs=pl.BlockSpec((None, tm, tn), lambda bb, i, j: (bb, i, j)),
        scratch_shapes=[pltpu.VMEM((tm, d), BF16)] * 3 + _norm_scratch(tm // 2, d),
        compiler_params=_params(3, 0),
        name="norm_qkv_proj",
    )(x, x, gains, mod5, mod5, mod5, mod5, w, perms)


def _attn_kernel(q_ref, kp_ref, km_ref, kn_ref, vp_ref, vm_ref, vn_ref, pt_ref, o_ref, lse_ref,
                 kc_ref, vc_ref, op_ref, lp_ref, bias_ref=None, *, nblk, ub, n_u, dilation, slopes):
    i = pl.program_id(1)
    tq = nblk * ub
    width = q_ref.shape[-1]
    sub = 2 * HALF
    span = sub + 2 * HALF
    row = lax.broadcasted_iota(jnp.int32, (sub, span), 0)
    col = lax.broadcasted_iota(jnp.int32, (sub, span), 1)
    adu = jnp.abs(col - HALF - row)
    band = adu <= HALF
    dist = (adu * dilation).astype(F32)
    lane = lax.broadcasted_iota(jnp.int32, (sub, LANES), 1)
    scale = HEAD_DIM ** -0.5
    blocks_per_sub = sub // ub

    def key_valid(sb):
        u_key = i * tq + (sb * sub - HALF) + col
        return band & (u_key >= 0) & (u_key < n_u)

    def one_class(r):
        def gather(dst, prev, main, nxt):
            dst[0:HALF, :] = prev[:, r].reshape(HALF, width)
            dst[HALF:HALF + tq, :] = main[:, r].reshape(tq, width)
            dst[HALF + tq:, :] = nxt[:, r].reshape(HALF, width)

        gather(kc_ref, kp_ref, km_ref, kn_ref)
        gather(vc_ref, vp_ref, vm_ref, vn_ref)
        for sb in range(tq // sub):
            if bias_ref is None:
                valid = key_valid(sb)
            lse_tile = jnp.zeros((sub, LANES), F32)
            rows = slice(sb * sub, (sb + 1) * sub)
            for h in range(HEADS_PER_GROUP):
                cs = slice(h * HEAD_DIM, (h + 1) * HEAD_DIM)
                q = q_ref[sb * blocks_per_sub:(sb + 1) * blocks_per_sub, r, :, cs].reshape(sub, HEAD_DIM)
                k = kc_ref[sb * sub:sb * sub + span, cs]
                v = vc_ref[sb * sub:sb * sub + span, cs]
                s = lax.dot_general(q, k, (((1,), (1,)), ((), ())), preferred_element_type=F32) * scale
                if bias_ref is None:
                    s = jnp.where(valid, s - slopes[h] * dist, MASK_VALUE)
                else:
                    s = s + bias_ref[sb, h]
                m = jnp.max(s, axis=-1, keepdims=True)
                p = jnp.exp(s - m)
                l = jnp.sum(p, axis=-1, keepdims=True)
                o = jnp.dot(p.astype(BF16), v, preferred_element_type=F32) * (1.0 / l)
                if dilation == 1:
                    o_ref[rows, cs] = o.astype(o_ref.dtype)
                else:
                    op_ref[r, rows, cs] = o.astype(op_ref.dtype)
                lse_tile = jnp.where(lane == h, m + jnp.log(l), lse_tile)
            if dilation == 1:
                lse_ref[rows, :] = lse_tile
            else:
                lp_ref[r, rows, :] = lse_tile

    if dilation == 1:
        one_class(0)
        return

    for sb in range(tq // sub):
        valid = key_valid(sb)
        for h in range(HEADS_PER_GROUP):
            bias_ref[sb, h] = jnp.where(valid, -slopes[h] * dist, MASK_VALUE)

    def class_step(r, carry):
        one_class(r)
        return carry

    lax.fori_loop(0, dilation, class_step, 0, unroll=4)
    pt = pt_ref[...]
    for blk in range(nblk):
        urows = slice(blk * ub, (blk + 1) * ub)
        nat = slice(blk * PERM_BLOCK, (blk + 1) * PERM_BLOCK)
        ob = jnp.concatenate([op_ref[rr, urows, :] for rr in range(dilation)], axis=0)
        o_ref[nat, :] = jnp.dot(pt, ob, preferred_element_type=F32).astype(o_ref.dtype)
        lb = jnp.concatenate([lp_ref[rr, urows, :] for rr in range(dilation)], axis=0)
        hi = lb.astype(BF16)
        rest = lb - hi.astype(F32)
        mid = rest.astype(BF16)
        lo = (rest - mid.astype(F32)).astype(BF16)
        lse_ref[nat, :] = (jnp.dot(pt, hi, preferred_element_type=F32)
                           + jnp.dot(pt, mid, preferred_element_type=F32)
                           + jnp.dot(pt, lo, preferred_element_type=F32))


def _group_attention(qkv, group, perms_t):
    b, s, width = qkv.shape
    _, dilation = ATTN_WINDOWS[group]
    n_u = s // dilation
    ub = HALF if dilation == 1 else PERM_BLOCK // dilation
    rb = ub * dilation
    tq = min({1: 512, 4: 256, 16: 128}[dilation], n_u)
    nblk = tq // ub
    hb = HALF // ub
    n_slabs = width // GROUP_WIDTH
    view = qkv.reshape(b, s // rb, dilation, ub, width)
    n_halo = (s // rb) // hb
    slopes = tuple(float(np.exp2(np.float32(-8.0 * (group * HEADS_PER_GROUP + h + 1) / N_ATTN_HEADS)))
                   for h in range(HEADS_PER_GROUP))

    def main(t):
        return pl.BlockSpec((None, nblk, dilation, ub, GROUP_WIDTH),
                            lambda bb, i: (bb, i, 0, 0, group * 3 + t))

    def halo_p(t):
        return pl.BlockSpec((None, hb, dilation, ub, GROUP_WIDTH),
                            lambda bb, i: (bb, jnp.maximum(i * (nblk // hb) - 1, 0), 0, 0, group * 3 + t))

    def halo_n(t):
        return pl.BlockSpec((None, hb, dilation, ub, GROUP_WIDTH),
                            lambda bb, i: (bb, jnp.minimum((i + 1) * (nblk // hb), n_halo - 1), 0, 0,
                                           group * 3 + t))

    assert nblk % hb == 0 and width == n_slabs * GROUP_WIDTH
    pt = perms_t[max(group - 1, 0)]
    kern = functools.partial(_attn_kernel, nblk=nblk, ub=ub, n_u=n_u, dilation=dilation, slopes=slopes)
    scratch = [pltpu.VMEM((tq + 2 * HALF, GROUP_WIDTH), BF16), pltpu.VMEM((tq + 2 * HALF, GROUP_WIDTH), BF16),
               pltpu.VMEM((dilation, tq, GROUP_WIDTH), BF16), pltpu.VMEM((dilation, tq, LANES), F32)]
    if dilation > 1:
        scratch.append(pltpu.VMEM((tq // (2 * HALF), HEADS_PER_GROUP, 2 * HALF, 4 * HALF), F32))
    rows = nblk * rb
    return pl.pallas_call(
        kern,
        out_shape=(jax.ShapeDtypeStruct((b, s, GROUP_WIDTH), BF16),
                   jax.ShapeDtypeStruct((b, s, LANES), F32)),
        grid=(b, s // rows),
        in_specs=[main(0), halo_p(1), main(1), halo_n(1), halo_p(2), main(2), halo_n(2),
                  _const_spec(pt.shape, 2)],
        out_specs=(pl.BlockSpec((None, rows, GROUP_WIDTH), lambda bb, i: (bb, i, 0)),
                   pl.BlockSpec((None, rows, LANES), lambda bb, i: (bb, i, 0))),
        scratch_shapes=scratch,
        compiler_params=_params(2, 2),
        name=f"dilated_attention_g{group}",
    )(view, view, view, view, view, view, view, pt)


def _merge_proj_kernel(o0_ref, o1_ref, o2_ref, l0_ref, l1_ref, l2_ref, x_ref, gate_ref, w_ref,
                       out_ref, mix_ref):
    l0, l1, l2 = l0_ref[...], l1_ref[...], l2_ref[...]
    m = jnp.maximum(jnp.maximum(l0, l1), l2)
    e0, e1, e2 = jnp.exp(l0 - m), jnp.exp(l1 - m), jnp.exp(l2 - m)
    inv = 1.0 / (e0 + e1 + e2)
    for g, (o_ref, e) in enumerate(((o0_ref, e0), (o1_ref, e1), (o2_ref, e2))):
        alpha = e * inv
        for h in range(HEADS_PER_GROUP):
            src = slice(h * HEAD_DIM, (h + 1) * HEAD_DIM)
            dst = slice((g * HEADS_PER_GROUP + h) * HEAD_DIM, (g * HEADS_PER_GROUP + h + 1) * HEAD_DIM)
            mix_ref[:, dst] = (alpha[:, h:h + 1] * o_ref[:, src].astype(F32)).astype(mix_ref.dtype)
    y = jnp.dot(mix_ref[...], w_ref[...], preferred_element_type=F32)
    out_ref[...] = x_ref[...] + gate_ref[...] * y


def _merge_proj(outs, lses, x, mod5, w_o, layer, w_layer, tm):
    b, s, d = x.shape
    width = w_o.shape[1]
    tok = lambda c: pl.BlockSpec((None, tm, c), lambda bb, i: (bb, i, 0))
    return pl.pallas_call(
        _merge_proj_kernel,
        out_shape=jax.ShapeDtypeStruct((b, s, d), F32),
        grid=(b, s // tm),
        in_specs=[tok(GROUP_WIDTH)] * 3 + [tok(LANES)] * 3 + [
            tok(d),
            _mod_in(layer, 2, d, 2),
            pl.BlockSpec((None, width, d), lambda bb, i: (w_layer, 0, 0)),
        ],
        out_specs=tok(d),
        scratch_shapes=[pltpu.VMEM((tm, width), BF16)],
        compiler_params=_params(2, 2),
        name="attn_merge_out_proj",
    )(*outs, *lses, x, mod5, w_o)


K1_GROUP = 8
FUSED_STAGE1_ROWS = 256
STAGE2_CHUNK = 1024


def _dft_tables(s):
    n2 = FFT_N2
    n1 = s // n2
    c = FOURIER_GROUP_DIM
    ang_c = 2.0 * np.pi * np.outer(np.arange(c), np.arange(c)) / c
    cs_chan = np.concatenate([np.cos(ang_c), np.sin(ang_c)], axis=1) / np.sqrt(c)
    a1 = 2.0 * np.pi * np.outer(np.arange(n1), np.arange(n1)) / n1
    c1, s1 = np.cos(a1), np.sin(a1)
    w1 = np.kron(np.block([[c1, s1], [-s1, c1]]), np.eye(ROW_TILE))
    s2 = (np.arange(n2 // ROW_TILE)[:, None, None] * ROW_TILE + np.arange(ROW_TILE)[None, None, :])
    th = 2.0 * np.pi * np.arange(n1)[None, :, None] * s2 / s
    th = th.reshape(n2 // ROW_TILE, n1 * ROW_TILE, 1)
    a2 = 2.0 * np.pi * np.outer(np.arange(n2), np.arange(n2)) / n2
    cs2 = np.concatenate([np.cos(a2), np.sin(a2)], axis=1)
    return (jnp.asarray(cs_chan, BF16), jnp.asarray(w1, BF16),
            jnp.asarray(np.cos(th), F32), jnp.asarray(np.sin(th), F32), jnp.asarray(cs2, BF16))


def _chan_dft_kernel(x_ref, g_ref, sh_ref, sc_ref, cs_ref, zr_ref, zi_ref, h_ref, rs_ref, ab_ref, *, tm):
    _ada_norm_rows(x_ref, g_ref[...], sh_ref[...], sc_ref[...], h_ref, rs_ref, ab_ref, tm)
    c = FOURIER_GROUP_DIM
    for g in range(x_ref.shape[-1] // c):
        cols = slice(g * c, (g + 1) * c)
        r = jnp.dot(h_ref[:, cols], cs_ref[...], preferred_element_type=F32)
        zr_ref[:, cols] = r[:, :c].astype(zr_ref.dtype)
        zi_ref[:, cols] = (-r[:, c:]).astype(zi_ref.dtype)


def _chan_dft(x, mod5, gains, cs_chan, layer, tm):
    b, s, d = x.shape
    tok = pl.BlockSpec((None, tm, d), lambda bb, i: (bb, i, 0))
    kern = functools.partial(_chan_dft_kernel, tm=tm)
    return pl.pallas_call(
        kern,
        out_shape=(jax.ShapeDtypeStruct((b, s, d), BF16),) * 2,
        grid=(b, s // tm),
        in_specs=[tok, _row_in(layer, d, 2), _mod_in(layer, 0, d, 2), _mod_in(layer, 1, d, 2),
                  _const_spec(cs_chan.shape, 2)],
        out_specs=(tok, tok),
        scratch_shapes=[pltpu.VMEM((tm, d), BF16)] + _norm_scratch(tm, d),
        compiler_params=_params(2, 2),
        name="fourier_channel_dft",
    )(x, gains, mod5, mod5, cs_chan)


def _pos_dft1_kernel(zr_ref, zi_ref, w_ref, tc_ref, ts_ref, ur_ref, ui_ref, *, rows):
    tc = zr_ref.shape[-1]
    z = jnp.concatenate([zr_ref[...].reshape(rows, tc), zi_ref[...].reshape(rows, tc)], axis=0)
    t = jnp.dot(w_ref[...], z, preferred_element_type=F32)
    tr, ti = t[:rows], t[rows:]
    c, sn = tc_ref[...], ts_ref[...]
    ur_ref[...] = (tr * c + ti * sn).astype(ur_ref.dtype).reshape(ur_ref.shape)
    ui_ref[...] = (ti * c - tr * sn).astype(ui_ref.dtype).reshape(ui_ref.shape)


def _pos_dft1(zr, zi, w1, tw_c, tw_s):
    b, s, d = zr.shape
    n2 = FFT_N2
    n1 = s // n2
    rows = n1 * ROW_TILE
    tc = min(d, (1024 * 1024) // rows)
    view = lambda a: a.reshape(b, n1, n2 // ROW_TILE, ROW_TILE, d)
    blk = pl.BlockSpec((None, n1, None, ROW_TILE, tc), lambda bb, j, c: (bb, 0, j, 0, c))
    tw = pl.BlockSpec((None, rows, 1), lambda bb, j, c: (j, 0, 0))
    kern = functools.partial(_pos_dft1_kernel, rows=rows)
    ur, ui = pl.pallas_call(
        kern,
        out_shape=(jax.ShapeDtypeStruct((b, n1, n2 // ROW_TILE, ROW_TILE, d), BF16),) * 2,
        grid=(b, n2 // ROW_TILE, d // tc),
        in_specs=[blk, blk, _const_spec(w1.shape, 3), tw, tw],
        out_specs=(blk, blk),
        compiler_params=_params(3, 3),
        name="fourier_pos_dft_stage1",
    )(view(zr), view(zi), w1, tw_c, tw_s)
    return ur.reshape(b, s, d), ui.reshape(b, s, d)


def _fourier_short_kernel(xa_ref, xb_ref, g_ref, sh_ref, sc_ref, gate_ref, cs_ref, w1_ref, tc_ref, ts_ref,
                          cs2_ref, wf_ref, bf_ref, out_ref, ur_ref, ui_ref, yb_ref, *, n1, n_q, inv_norm):
    t = pl.program_id(1)
    n2 = cs2_ref.shape[0]
    d = yb_ref.shape[-1]
    rows = n1 * ROW_TILE
    n_tiles = n2 // ROW_TILE

    @pl.when(t < n_tiles)
    def _():
        def stage1(h_ref, z_ref, rs_ref, ab_ref):
            groups = NORM_CHUNK // ROW_TILE
            load = lambda c, cols: xa_ref[pl.ds(pl.multiple_of(c * groups, groups), groups), :, cols].reshape(
                NORM_CHUNK, LANES)
            _ada_norm_rows(xa_ref, g_ref[...], sh_ref[...], sc_ref[...], h_ref, rs_ref, ab_ref, rows, load)
            c = FOURIER_GROUP_DIM
            for g in range(d // c):
                cols = slice(g * c, (g + 1) * c)
                r = jnp.dot(h_ref[:, cols], cs_ref[...], preferred_element_type=F32)
                z_ref[0:rows, cols] = r[:, :c].astype(z_ref.dtype)
                z_ref[rows:, cols] = (-r[:, c:]).astype(z_ref.dtype)
            tt = jnp.dot(w1_ref[...], z_ref[...], preferred_element_type=F32)
            tr, ti = tt[:rows], tt[rows:]
            cw, sw = tc_ref[...], ts_ref[...]
            s2 = pl.ds(pl.multiple_of(t * ROW_TILE, ROW_TILE), ROW_TILE)
            ur_ref[:, s2, :] = (tr * cw + ti * sw).astype(ur_ref.dtype).reshape(n1, ROW_TILE, d)
            ui_ref[:, s2, :] = (ti * cw - tr * sw).astype(ui_ref.dtype).reshape(n1, ROW_TILE, d)

        pl.run_scoped(stage1, pltpu.VMEM((rows, d), BF16), pltpu.VMEM((2 * rows, d), BF16),
                      *_norm_scratch(rows, d))

    @pl.when(t >= n_tiles)
    def _():
        tb = t - n_tiles
        a, q = tb // n_q, tb % n_q

        @pl.when(q == 0)
        def _():
            def interleave(ys_ref):
                chunk_slabs = ys_ref.shape[0]
                chunk = chunk_slabs * LANES
                for cc in range(d // chunk):
                    cols = slice(cc * chunk, (cc + 1) * chunk)
                    for k in range(K1_GROUP):
                        k1 = a * K1_GROUP + k
                        u = jnp.concatenate([ur_ref[k1, :, cols], ui_ref[k1, :, cols]], axis=0)
                        y = jnp.dot(cs2_ref[...], u, preferred_element_type=F32) * inv_norm
                        for sl in range(chunk_slabs):
                            ys_ref[sl, pl.ds(k, n2, stride=K1_GROUP), :] = y[:, sl * LANES:(sl + 1) * LANES]
                    for sl in range(chunk_slabs):
                        c0 = cc * chunk + sl * LANES
                        yb_ref[:, c0:c0 + LANES] = ys_ref[sl].astype(yb_ref.dtype)

            pl.run_scoped(interleave, pltpu.VMEM((STAGE2_CHUNK // LANES, K1_GROUP * n2, LANES), F32))

        step_rows = xb_ref.shape[0] * K1_GROUP
        r0 = pl.multiple_of(q * step_rows, step_rows)
        proj = jnp.dot(yb_ref[pl.ds(r0, step_rows), :], wf_ref[...], preferred_element_type=F32) + bf_ref[...]
        out = xb_ref[...].reshape(proj.shape) + gate_ref[...] * proj
        out_ref[...] = out.reshape(out_ref.shape)


def _fourier_short(x, mod5, gains, tables, w_f, b_f, layer, w_layer):
    cs_chan, w1, tw_c, tw_s, cs2 = tables
    b, s, d = x.shape
    n2 = FFT_N2
    n1 = s // n2
    n_tiles = n2 // ROW_TILE
    n_a = n1 // K1_GROUP
    n_q = 4
    rows = n1 * ROW_TILE
    xa_view = x.reshape(b, n1, n_tiles, ROW_TILE, d)
    xb_view = x.reshape(b, n2, n_a, K1_GROUP, d)
    tile = lambda t: jnp.minimum(t, n_tiles - 1)
    late = lambda t: jnp.maximum(t - n_tiles, 0)
    xa_blk = pl.BlockSpec((None, n1, None, ROW_TILE, d), lambda bb, t: (bb, 0, tile(t), 0, 0))
    xb_blk = pl.BlockSpec((None, n2 // n_q, None, K1_GROUP, d),
                          lambda bb, t: (bb, late(t) % n_q, late(t) // n_q, 0, 0))
    tw = pl.BlockSpec((None, rows, 1), lambda bb, t: (tile(t), 0, 0))
    kern = functools.partial(_fourier_short_kernel, n1=n1, n_q=n_q, inv_norm=float(1.0 / np.sqrt(s)))
    out = pl.pallas_call(
        kern,
        out_shape=jax.ShapeDtypeStruct(xb_view.shape, F32),
        grid=(b, n_tiles + n_a * n_q),
        in_specs=[xa_blk, xb_blk, _row_in(layer, d, 2), _mod_in(layer, 0, d, 2), _mod_in(layer, 1, d, 2),
                  _mod_in(layer, 2, d, 2), _const_spec(cs_chan.shape, 2), _const_spec(w1.shape, 2), tw, tw,
                  _const_spec(cs2.shape, 2),
                  pl.BlockSpec((None, d, d), lambda bb, t: (w_layer, 0, 0), pipeline_mode=pl.Buffered(1)),
                  _row_in(w_layer, d, 2)],
        out_specs=xb_blk,
        scratch_shapes=[pltpu.VMEM((n1, n2, d), BF16), pltpu.VMEM((n1, n2, d), BF16),
                        pltpu.VMEM((K1_GROUP * n2, d), BF16)],
        compiler_params=_params(2, 1),
        name="fourier_mixer_short",
    )(xa_view, xb_view, gains, mod5, mod5, mod5, cs_chan, w1, tw_c, tw_s, cs2, w_f, b_f)
    return out.reshape(b, s, d)


def _pos_dft2_proj_kernel(ur_ref, ui_ref, cs2_ref, w_ref, b_ref, x_ref, gate_ref, out_ref, ys_ref, yb_ref, *,
                          inv_norm):
    n2 = cs2_ref.shape[0]
    d = ur_ref.shape[-1]
    chunk_slabs = ys_ref.shape[0]
    chunk = chunk_slabs * LANES

    @pl.when(pl.program_id(2) == 0)
    def _():
        for cc in range(d // chunk):
            for k in range(K1_GROUP):
                rows = slice(k * n2, (k + 1) * n2)
                cols = slice(cc * chunk, (cc + 1) * chunk)
                u = jnp.concatenate([ur_ref[rows, cols], ui_ref[rows, cols]], axis=0)
                y = jnp.dot(cs2_ref[...], u, preferred_element_type=F32) * inv_norm
                for sl in range(chunk_slabs):
                    ys_ref[sl, pl.ds(k, n2, stride=K1_GROUP), :] = y[:, sl * LANES:(sl + 1) * LANES]
            for sl in range(chunk_slabs):
                c0 = cc * chunk + sl * LANES
                yb_ref[:, c0:c0 + LANES] = ys_ref[sl].astype(yb_ref.dtype)

    step_rows = x_ref.shape[0] * K1_GROUP
    r0 = pl.multiple_of(pl.program_id(2) * step_rows, step_rows)
    proj = jnp.dot(yb_ref[pl.ds(r0, step_rows), :], w_ref[...], preferred_element_type=F32) + b_ref[...]
    out = x_ref[...].reshape(proj.shape) + gate_ref[...] * proj
    out_ref[...] = out.reshape(out_ref.shape)


def _pos_dft2_proj(ur, ui, cs2, x, mod5, w_f, b_f, layer, w_layer):
    b, s, d = x.shape
    n2 = FFT_N2
    n1 = s // n2
    n_q = 2
    rows = K1_GROUP * n2
    n_a = n1 // K1_GROUP

    def u_blk(first_step):
        def imap(bb, a, q):
            nxt = jnp.minimum(bb * n_a + a + (q >= first_step).astype(jnp.int32), b * n_a - 1)
            return nxt // n_a, nxt % n_a, 0
        return pl.BlockSpec((None, rows, d), imap)

    x_view = x.reshape(b, n2, n_a, K1_GROUP, d)
    x_blk = pl.BlockSpec((None, n2 // n_q, None, K1_GROUP, d), lambda bb, a, q: (bb, q, a, 0, 0))
    kern = functools.partial(_pos_dft2_proj_kernel, inv_norm=float(1.0 / np.sqrt(s)))
    out = pl.pallas_call(
        kern,
        out_shape=jax.ShapeDtypeStruct(x_view.shape, F32),
        grid=(b, n_a, n_q),
        in_specs=[u_blk(1), u_blk(1), _const_spec(cs2.shape, 3),
                  pl.BlockSpec((None, d, d), lambda bb, a, q: (w_layer, 0, 0), pipeline_mode=pl.Buffered(1)),
                  _row_in(w_layer, d, 3), x_blk, _mod_in(layer, 2, d, 3)],
        out_specs=x_blk,
        scratch_shapes=[pltpu.VMEM((STAGE2_CHUNK // LANES, rows, LANES), F32), pltpu.VMEM((rows, d), BF16)],
        compiler_params=_params(3, 2),
        name="fourier_pos_dft_stage2_proj",
    )(ur, ui, cs2, w_f, b_f, x_view, mod5)
    return out.reshape(b, s, d)


def _mlp_kernel(x_ref, g_ref, sh_ref, sc_ref, gate_ref, w1_ref, b1_ref, w2_ref, b2_ref, fg_ref,
                out_ref, h_ref, acc_ref, rs_ref, ab_ref, *, tm, final_norm):
    j = pl.program_id(2)

    @pl.when(j == 0)
    def _():
        _ada_norm_rows(x_ref, g_ref[...], sh_ref[...], sc_ref[...], h_ref, rs_ref, ab_ref, tm, zero_ref=acc_ref)

    u = jnp.dot(h_ref[...], w1_ref[...], preferred_element_type=F32) + b1_ref[...]
    u = jnp.maximum(u, 0.0)
    acc_ref[...] += jnp.dot((u * u).astype(BF16), w2_ref[...], preferred_element_type=F32)

    @pl.when(j == pl.num_programs(2) - 1)
    def _():
        y = x_ref[...] + gate_ref[...] * (acc_ref[...] + b2_ref[...])
        if final_norm:
            ms = jnp.mean(y * y, axis=-1, keepdims=True)
            y = y * lax.rsqrt(ms + RMS_EPS) * fg_ref[...]
        out_ref[...] = y


def _mlp(x, mod5, gains, w1, b1, w2, b2, final_g, layer, tm, tf, final_norm):
    b, s, d = x.shape
    dff = w1.shape[-1]
    kern = functools.partial(_mlp_kernel, tm=tm, final_norm=final_norm)
    tok = pl.BlockSpec((None, tm, d), lambda bb, i, j: (bb, i, 0))
    return pl.pallas_call(
        kern,
        out_shape=jax.ShapeDtypeStruct((b, s, d), F32),
        grid=(b, s // tm, dff // tf),
        in_specs=[
            tok,
            _row_in(layer, d, 3),
            _mod_in(layer, 3, d, 3),
            _mod_in(layer, 4, d, 3),
            _mod_in(layer, 5, d, 3),
            pl.BlockSpec((None, d, tf), lambda bb, i, j: (layer, 0, j)),
            pl.BlockSpec((None, 1, tf), lambda bb, i, j: (layer, 0, j)),
            pl.BlockSpec((None, tf, d), lambda bb, i, j: (layer, j, 0)),
            _row_in(layer, d, 3),
            pl.BlockSpec((1, d), lambda bb, i, j: (0, 0)),
        ],
        out_specs=tok,
        scratch_shapes=[pltpu.VMEM((tm, d), BF16), pltpu.VMEM((tm, d), F32)] + _norm_scratch(tm, d),
        compiler_params=_params(3, 2),
        name="sqrelu_mlp",
    )(x, gains, mod5, mod5, mod5, w1, b1, w2, b2, final_g)


def _trunk(x, mod5, p):
    depth = p["w1"].shape[0]
    s = x.shape[1]
    cs_chan, w1c, tw_c, tw_s, cs2 = _dft_tables(s)
    for i in range(depth):
        sub = i // 2
        if i % 2 == 0:
            qkv = _norm_proj(x, mod5, p["norm1_g"], p["w_qkv"], p["perms"], i, sub, tm=1024, tn=768)
            outs, lses = zip(*[_group_attention(qkv, g, p["perms_t"]) for g in range(len(ATTN_WINDOWS))])
            x = _merge_proj(outs, lses, x, mod5, p["w_o"], i, sub, tm=512)
        else:
            if (s // FFT_N2) * ROW_TILE <= FUSED_STAGE1_ROWS:
                x = _fourier_short(x, mod5, p["norm1_g"], (cs_chan, w1c, tw_c, tw_s, cs2), p["w_f"], p["b_f"],
                                   i, sub)
            else:
                zr, zi = _chan_dft(x, mod5, p["norm1_g"], cs_chan, i, tm=512)
                ur, ui = _pos_dft1(zr, zi, w1c, tw_c, tw_s)
                x = _pos_dft2_proj(ur, ui, cs2, x, mod5, p["w_f"], p["b_f"], i, sub)
        x = _mlp(x, mod5, p["norm2_g"], p["w1"], p["b1"], p["w2"], p["b2"], p["final_g"], i,
                 tm=512, tf=1024, final_norm=(i == depth - 1))
    return x


def kernel(x_prompt, x_sample, c_prompt, c_sample, w_ada, b_ada, norm1_g, norm2_g, w_qkv, w_o, w_f, b_f,
           w1, b1, w2, b2, final_g):
    depth, d, _ = w_ada.shape
    n_p, n_s = c_prompt.shape[0], c_sample.shape[0]
    rows = -(-(n_p + n_s) // ROW_TILE) * ROW_TILE
    c_all = jnp.concatenate([c_prompt, c_sample, jnp.zeros((rows - n_p - n_s, d), F32)], axis=0)
    mod = _modulation(c_all, w_ada, b_ada)
    mod_p = mod[:, :n_p].reshape(depth, n_p, N_MOD, 1, d)
    mod_s = mod[:, n_p:n_p + n_s].reshape(depth, n_s, N_MOD, 1, d)

    row3 = lambda a: a.reshape(a.shape[0], 1, a.shape[-1])
    perms = _residue_perms()
    params = {
        "norm1_g": row3(norm1_g), "norm2_g": row3(norm2_g),
        "w_qkv": w_qkv.astype(BF16), "w_o": w_o.astype(BF16),
        "w_f": w_f.astype(BF16), "b_f": row3(b_f),
        "w1": w1.astype(BF16), "b1": row3(b1), "w2": w2.astype(BF16), "b2": row3(b2),
        "final_g": final_g.reshape(1, d),
        "perms": jnp.asarray(perms, BF16),
        "perms_t": jnp.asarray(np.transpose(perms, (0, 2, 1)), BF16),
    }
    return (_trunk(x_prompt, mod_p, params), _trunk(x_sample, mod_s, params))
```

```python
import functools

import numpy as np
import jax
import jax.numpy as jnp
from jax import lax
from jax.experimental import pallas as pl
from jax.experimental.pallas import tpu as pltpu

F32 = jnp.float32
BF16 = jnp.bfloat16

N_MOD = 6
RMS_EPS = 1e-6
MASK_VALUE = -1e30
ATTN_WINDOWS = ((128, 1), (512, 4), (2048, 16))
HEADS_PER_GROUP = 6
HEAD_DIM = 128
N_ATTN_HEADS = len(ATTN_WINDOWS) * HEADS_PER_GROUP
GROUP_WIDTH = HEADS_PER_GROUP * HEAD_DIM
FOURIER_GROUP_DIM = 256
HALF = 64
FFT_N2 = 128
LANES = 128
ROW_TILE = 16
PERM_BLOCK = 256
VMEM_LIMIT = 56 * 1024 * 1024


def _params(n_axes, n_parallel):
    sem = ("parallel",) * n_parallel + ("arbitrary",) * (n_axes - n_parallel)
    return pltpu.CompilerParams(dimension_semantics=sem, vmem_limit_bytes=VMEM_LIMIT)


def _const_spec(shape, n_grid):
    zeros = (0,) * len(shape)
    imap = (lambda a, b: zeros) if n_grid == 2 else (lambda a, b, c: zeros)
    return pl.BlockSpec(shape, imap, pipeline_mode=pl.Buffered(1))


NORM_CHUNK = 64


def _norm_scratch(tm, d):
    return [pltpu.VMEM((tm, LANES), F32), pltpu.VMEM((2, d), F32)]


def _ada_norm_rows(x_ref, gain, shift, scale, h_ref, rs_ref, ab_ref, rows, load=None, zero_ref=None):
    d = h_ref.shape[-1]
    slabs = d // LANES
    if load is None:
        load = lambda c, cols: x_ref[pl.ds(pl.multiple_of(c * NORM_CHUNK, NORM_CHUNK), NORM_CHUNK), cols]
    ab_ref[0:1, :] = gain * (1.0 + scale)
    ab_ref[1:2, :] = shift

    def stats(c, carry):
        r0 = pl.multiple_of(c * NORM_CHUNK, NORM_CHUNK)
        acc = jnp.zeros((NORM_CHUNK, LANES), F32)
        for t in range(slabs):
            xt = load(c, slice(t * LANES, (t + 1) * LANES))
            acc = acc + xt * xt
        ms = jnp.sum(acc, axis=-1, keepdims=True) * (1.0 / d)
        rs_ref[pl.ds(r0, NORM_CHUNK), :] = jnp.broadcast_to(lax.rsqrt(ms + RMS_EPS), (NORM_CHUNK, LANES))
        return carry

    lax.fori_loop(0, rows // NORM_CHUNK, stats, 0, unroll=4)

    def apply(c, carry):
        r0 = pl.multiple_of(c * NORM_CHUNK, NORM_CHUNK)
        rs = rs_ref[pl.ds(r0, NORM_CHUNK), :]
        for t in range(slabs):
            cols = slice(t * LANES, (t + 1) * LANES)
            h = load(c, cols) * rs * ab_ref[0:1, cols] + ab_ref[1:2, cols]
            h_ref[pl.ds(r0, NORM_CHUNK), cols] = h.astype(h_ref.dtype)
            if zero_ref is not None:
                zero_ref[pl.ds(r0, NORM_CHUNK), cols] = jnp.zeros((NORM_CHUNK, LANES), zero_ref.dtype)
        return carry

    lax.fori_loop(0, rows // NORM_CHUNK, apply, 0)


def _ada_norm_chunks(x_ref, gain, shift, scale, h_ref, row0, rows):
    d = h_ref.shape[-1]
    slabs = d // LANES
    a = gain * (1.0 + scale)
    for c in range(rows // NORM_CHUNK):
        r0 = pl.multiple_of(row0 + c * NORM_CHUNK, NORM_CHUNK)
        acc = jnp.zeros((NORM_CHUNK, LANES), F32)
        for t in range(slabs):
            xt = x_ref[pl.ds(r0, NORM_CHUNK), t * LANES:(t + 1) * LANES]
            acc = acc + xt * xt
        ms = jnp.sum(acc, axis=-1, keepdims=True) * (1.0 / d)
        rs = jnp.broadcast_to(lax.rsqrt(ms + RMS_EPS), (NORM_CHUNK, LANES))
        for t in range(slabs):
            cols = slice(t * LANES, (t + 1) * LANES)
            h = x_ref[pl.ds(r0, NORM_CHUNK), cols] * rs * a[:, cols] + shift[:, cols]
            h_ref[pl.ds(r0, NORM_CHUNK), cols] = h.astype(h_ref.dtype)


def _mod_kernel(c_ref, w_ref, b_ref, o_ref):
    c = c_ref[...]
    act = (c * jax.nn.sigmoid(c)).astype(BF16)
    w = w_ref[...].astype(BF16)
    o_ref[...] = jnp.dot(act, w, preferred_element_type=F32) + b_ref[...]


def _modulation(c_all, w_ada, b_ada):
    depth, d, n = w_ada.shape
    rows = c_all.shape[0]
    tn = 1024
    return pl.pallas_call(
        _mod_kernel,
        out_shape=jax.ShapeDtypeStruct((depth, rows, n), F32),
        grid=(depth, n // tn),
        in_specs=[
            pl.BlockSpec((rows, d), lambda l, j: (0, 0)),
            pl.BlockSpec((None, d, tn), lambda l, j: (l, 0, j)),
            pl.BlockSpec((None, 1, tn), lambda l, j: (l, 0, j)),
        ],
        out_specs=pl.BlockSpec((None, rows, tn), lambda l, j: (l, 0, j)),
        compiler_params=_params(2, 2),
        name="adaln_modulation",
    )(c_all, w_ada, b_ada.reshape(depth, 1, n))


def _mod_in(layer, which, d, n_grid):
    if n_grid == 2:
        imap = lambda b, i: (layer, b, which, 0, 0)
    else:
        imap = lambda b, i, j: (layer, b, which, 0, 0)
    return pl.BlockSpec((None, None, None, 1, d), imap)


def _row_in(layer, d, n_grid):
    if n_grid == 2:
        imap = lambda b, i: (layer, 0, 0)
    else:
        imap = lambda b, i, j: (layer, 0, 0)
    return pl.BlockSpec((None, 1, d), imap)


def _residue_perms():
    mats = []
    for _, dil in ATTN_WINDOWS[1:]:
        ub = PERM_BLOCK // dil
        p = np.zeros((PERM_BLOCK, PERM_BLOCK), np.float32)
        nat = np.arange(PERM_BLOCK)
        p[(nat % dil) * ub + nat // dil, nat] = 1.0
        mats.append(p)
    return np.stack(mats)


def _norm_proj_kernel(xa_ref, xb_ref, g_ref, sh_ref, sc_ref, shn_ref, scn_ref, w_ref, p_ref, o_ref,
                      h0_ref, h1_ref, hp_ref, rs_ref, ab_ref, *, tm, steps_per_group):
    i, j = pl.program_id(1), pl.program_id(2)
    tile = pl.program_id(0) * pl.num_programs(1) + i
    half = tm // 2
    spg = steps_per_group
    gain = g_ref[...]
    rows_per_step = -(-half // (spg * NORM_CHUNK)) * NORM_CHUNK

    @pl.when((tile == 0) & (j == 0))
    def _():
        _ada_norm_rows(xa_ref, gain, sh_ref[...], sc_ref[...], h0_ref.at[0:half], rs_ref, ab_ref, half)
        _ada_norm_rows(xb_ref, gain, sh_ref[...], sc_ref[...], h0_ref.at[half:tm], rs_ref, ab_ref, half)

    def norm_ahead(x_ref, h_rows, step):
        row0 = jnp.minimum(step * rows_per_step, half - rows_per_step)
        _ada_norm_chunks(x_ref, gain, shn_ref[...], scn_ref[...], h_rows, row0, rows_per_step)

    def project(lhs_ref):
        o_ref[...] = jnp.dot(lhs_ref[...], w_ref[...], preferred_element_type=F32).astype(o_ref.dtype)

    def column_step(h_cur, h_next):
        @pl.when((j >= spg) & (j % spg == 0))
        def _():
            for blk in range(tm // PERM_BLOCK):
                rows = slice(blk * PERM_BLOCK, (blk + 1) * PERM_BLOCK)
                hp_ref[rows, :] = jnp.dot(p_ref[...], h_cur[rows, :],
                                          preferred_element_type=F32).astype(hp_ref.dtype)

        @pl.when(j < spg)
        def _():
            project(h_cur)

        @pl.when((j >= spg) & (j < 2 * spg))
        def _():
            norm_ahead(xa_ref, h_next.at[0:half], j - spg)
            project(hp_ref)

        @pl.when(j >= 2 * spg)
        def _():
            norm_ahead(xb_ref, h_next.at[half:tm], j - 2 * spg)
            project(hp_ref)

    @pl.when(tile % 2 == 0)
    def _():
        column_step(h0_ref, h1_ref)

    @pl.when(tile % 2 == 1)
    def _():
        column_step(h1_ref, h0_ref)


def _norm_proj(x, mod5, gains, w, perms, layer, w_layer, tm, tn):
    b, s, d = x.shape
    n = w.shape[-1]
    n_i = s // tm
    spg = (n // len(ATTN_WINDOWS)) // tn
    assert n // tn == len(ATTN_WINDOWS) * spg
    kern = functools.partial(_norm_proj_kernel, tm=tm, steps_per_group=spg)

    def tile_ahead(bb, i, ahead):
        t1 = jnp.minimum(bb * n_i + i + ahead, b * n_i - 1)
        return t1 // n_i, t1 % n_i

    def x_half(which, first_step):
        def imap(bb, i, j):
            b1, i1 = tile_ahead(bb, i, (j >= first_step).astype(jnp.int32))
            return b1, 2 * i1 + which, 0
        return pl.BlockSpec((None, tm // 2, d), imap)

    mod_next = lambda which: pl.BlockSpec((None, None, None, 1, d),
                                          lambda bb, i, j: (layer, tile_ahead(bb, i, 1)[0], which, 0, 0))
    return pl.pallas_call(
        kern,
        out_shape=jax.ShapeDtypeStruct((b, s, n), BF16),
        grid=(b, n_i, n // tn),
        in_specs=[
            x_half(0, spg),
            x_half(1, 2 * spg),
            _row_in(layer, d, 3),
            _mod_in(layer, 0, d, 3),
            _mod_in(layer, 1, d, 3),
            mod_next(0),
            mod_next(1),
            pl.BlockSpec((None, d, tn), lambda bb, i, j: (w_layer, 0, j)),
            pl.BlockSpec((None, PERM_BLOCK, PERM_BLOCK),
                         lambda bb, i, j: (jnp.maximum(j // spg, 1) - 1, 0, 0)),
        ],
        out_specs=pl.BlockSpec((None, tm, tn), lambda bb, i, j: (bb, i, j)),
        scratch_shapes=[pltpu.VMEM((tm, d), BF16)] * 3 + _norm_scratch(tm // 2, d),
        compiler_params=_params(3, 0),
        name="norm_qkv_proj",
    )(x, x, gains, mod5, mod5, mod5, mod5, w, perms)


def _attn_kernel(q_ref, kp_ref, km_ref, kn_ref, vp_ref, vm_ref, vn_ref, pt_ref, o_ref, lse_ref,
                 kc_ref, vc_ref, op_ref, lp_ref, bias_ref=None, *, nblk, ub, n_u, dilation, slopes):
    i = pl.program_id(1)
    tq = nblk * ub
    width = q_ref.shape[-1]
    sub = 2 * HALF
    span = sub + 2 * HALF
    row = lax.broadcasted_iota(jnp.int32, (sub, span), 0)
    col = lax.broadcasted_iota(jnp.int32, (sub, span), 1)
    adu = jnp.abs(col - HALF - row)
    band = adu <= HALF
    dist = (adu * dilation).astype(F32)
    lane = lax.broadcasted_iota(jnp.int32, (sub, LANES), 1)
    scale = HEAD_DIM ** -0.5
    blocks_per_sub = sub // ub

    def key_valid(sb):
        u_key = i * tq + (sb * sub - HALF) + col
        return band & (u_key >= 0) & (u_key < n_u)

    def one_class(r):
        def gather(dst, prev, main, nxt):
            dst[0:HALF, :] = prev[:, r].reshape(HALF, width)
            dst[HALF:HALF + tq, :] = main[:, r].reshape(tq, width)
            dst[HALF + tq:, :] = nxt[:, r].reshape(HALF, width)

        gather(kc_ref, kp_ref, km_ref, kn_ref)
        gather(vc_ref, vp_ref, vm_ref, vn_ref)
        for sb in range(tq // sub):
            if bias_ref is None:
                valid = key_valid(sb)
            lse_tile = jnp.zeros((sub, LANES), F32)
            rows = slice(sb * sub, (sb + 1) * sub)
            for h in range(HEADS_PER_GROUP):
                cs = slice(h * HEAD_DIM, (h + 1) * HEAD_DIM)
                q = q_ref[sb * blocks_per_sub:(sb + 1) * blocks_per_sub, r, :, cs].reshape(sub, HEAD_DIM)
                k = kc_ref[sb * sub:sb * sub + span, cs]
                v = vc_ref[sb * sub:sb * sub + span, cs]
                s = lax.dot_general(q, k, (((1,), (1,)), ((), ())), preferred_element_type=F32) * scale
                if bias_ref is None:
                    s = jnp.where(valid, s - slopes[h] * dist, MASK_VALUE)
                else:
                    s = s + bias_ref[sb, h]
                m = jnp.max(s, axis=-1, keepdims=True)
                p = jnp.exp(s - m)
                l = jnp.sum(p, axis=-1, keepdims=True)
                o = jnp.dot(p.astype(BF16), v, preferred_element_type=F32) * (1.0 / l)
                if dilation == 1:
                    o_ref[rows, cs] = o.astype(o_ref.dtype)
                else:
                    op_ref[r, rows, cs] = o.astype(op_ref.dtype)
                lse_tile = jnp.where(lane == h, m + jnp.log(l), lse_tile)
            if dilation == 1:
                lse_ref[rows, :] = lse_tile
            else:
                lp_ref[r, rows, :] = lse_tile

    if dilation == 1:
        one_class(0)
        return

    for sb in range(tq // sub):
        valid = key_valid(sb)
        for h in range(HEADS_PER_GROUP):
            bias_ref[sb, h] = jnp.where(valid, -slopes[h] * dist, MASK_VALUE)

    def class_step(r, carry):
        one_class(r)
        return carry

    lax.fori_loop(0, dilation, class_step, 0, unroll=min(dilation, 8))
    pt = pt_ref[...]
    for blk in range(nblk):
        urows = slice(blk * ub, (blk + 1) * ub)
        nat = slice(blk * PERM_BLOCK, (blk + 1) * PERM_BLOCK)
        ob = jnp.concatenate([op_ref[rr, urows, :] for rr in range(dilation)], axis=0)
        o_ref[nat, :] = jnp.dot(pt, ob, preferred_element_type=F32).astype(o_ref.dtype)
        lb = jnp.concatenate([lp_ref[rr, urows, :] for rr in range(dilation)], axis=0)
        hi = lb.astype(BF16)
        rest = lb - hi.astype(F32)
        mid = rest.astype(BF16)
        lo = (rest - mid.astype(F32)).astype(BF16)
        lse_ref[nat, :] = (jnp.dot(pt, hi, preferred_element_type=F32)
                           + jnp.dot(pt, mid, preferred_element_type=F32)
                           + jnp.dot(pt, lo, preferred_element_type=F32))


def _group_attention(qkv, group, perms_t):
    b, s, width = qkv.shape
    _, dilation = ATTN_WINDOWS[group]
    n_u = s // dilation
    ub = HALF if dilation == 1 else PERM_BLOCK // dilation
    rb = ub * dilation
    tq = min({1: 512, 4: 256, 16: 128}[dilation], n_u)
    nblk = tq // ub
    hb = HALF // ub
    n_slabs = width // GROUP_WIDTH
    view = qkv.reshape(b, s // rb, dilation, ub, width)
    n_halo = (s // rb) // hb
    slopes = tuple(float(np.exp2(np.float32(-8.0 * (group * HEADS_PER_GROUP + h + 1) / N_ATTN_HEADS)))
                   for h in range(HEADS_PER_GROUP))

    def main(t):
        return pl.BlockSpec((None, nblk, dilation, ub, GROUP_WIDTH),
                            lambda bb, i: (bb, i, 0, 0, group * 3 + t))

    def halo_p(t):
        return pl.BlockSpec((None, hb, dilation, ub, GROUP_WIDTH),
                            lambda bb, i: (bb, jnp.maximum(i * (nblk // hb) - 1, 0), 0, 0, group * 3 + t))

    def halo_n(t):
        return pl.BlockSpec((None, hb, dilation, ub, GROUP_WIDTH),
                            lambda bb, i: (bb, jnp.minimum((i + 1) * (nblk // hb), n_halo - 1), 0, 0,
                                           group * 3 + t))

    assert nblk % hb == 0 and width == n_slabs * GROUP_WIDTH
    pt = perms_t[max(group - 1, 0)]
    kern = functools.partial(_attn_kernel, nblk=nblk, ub=ub, n_u=n_u, dilation=dilation, slopes=slopes)
    scratch = [pltpu.VMEM((tq + 2 * HALF, GROUP_WIDTH), BF16), pltpu.VMEM((tq + 2 * HALF, GROUP_WIDTH), BF16),
               pltpu.VMEM((dilation, tq, GROUP_WIDTH), BF16), pltpu.VMEM((dilation, tq, LANES), F32)]
    if dilation > 1:
        scratch.append(pltpu.VMEM((tq // (2 * HALF), HEADS_PER_GROUP, 2 * HALF, 4 * HALF), F32))
    rows = nblk * rb
    return pl.pallas_call(
        kern,
        out_shape=(jax.ShapeDtypeStruct((b, s, GROUP_WIDTH), BF16),
                   jax.ShapeDtypeStruct((b, s, LANES), F32)),
        grid=(b, s // rows),
        in_specs=[main(0), halo_p(1), main(1), halo_n(1), halo_p(2), main(2), halo_n(2),
                  _const_spec(pt.shape, 2)],
        out_specs=(pl.BlockSpec((None, rows, GROUP_WIDTH), lambda bb, i: (bb, i, 0)),
                   pl.BlockSpec((None, rows, LANES), lambda bb, i: (bb, i, 0))),
        scratch_shapes=scratch,
        compiler_params=_params(2, 2),
        name=f"dilated_attention_g{group}",
    )(view, view, view, view, view, view, view, pt)


def _merge_proj_kernel(o0_ref, o1_ref, o2_ref, l0_ref, l1_ref, l2_ref, x_ref, gate_ref, w_ref,
                       out_ref, mix_ref):
    l0, l1, l2 = l0_ref[...], l1_ref[...], l2_ref[...]
    m = jnp.maximum(jnp.maximum(l0, l1), l2)
    e0, e1, e2 = jnp.exp(l0 - m), jnp.exp(l1 - m), jnp.exp(l2 - m)
    inv = 1.0 / (e0 + e1 + e2)
    for g, (o_ref, e) in enumerate(((o0_ref, e0), (o1_ref, e1), (o2_ref, e2))):
        alpha = e * inv
        for h in range(HEADS_PER_GROUP):
            src = slice(h * HEAD_DIM, (h + 1) * HEAD_DIM)
            dst = slice((g * HEADS_PER_GROUP + h) * HEAD_DIM, (g * HEADS_PER_GROUP + h + 1) * HEAD_DIM)
            mix_ref[:, dst] = (alpha[:, h:h + 1] * o_ref[:, src].astype(F32)).astype(mix_ref.dtype)
    y = jnp.dot(mix_ref[...], w_ref[...], preferred_element_type=F32)
    out_ref[...] = x_ref[...] + gate_ref[...] * y


def _merge_proj(outs, lses, x, mod5, w_o, layer, w_layer, tm):
    b, s, d = x.shape
    width = w_o.shape[1]
    tok = lambda c: pl.BlockSpec((None, tm, c), lambda bb, i: (bb, i, 0))
    return pl.pallas_call(
        _merge_proj_kernel,
        out_shape=jax.ShapeDtypeStruct((b, s, d), F32),
        grid=(b, s // tm),
        in_specs=[tok(GROUP_WIDTH)] * 3 + [tok(LANES)] * 3 + [
            tok(d),
            _mod_in(layer, 2, d, 2),
            pl.BlockSpec((None, width, d), lambda bb, i: (w_layer, 0, 0)),
        ],
        out_specs=tok(d),
        scratch_shapes=[pltpu.VMEM((tm, width), BF16)],
        compiler_params=_params(2, 2),
        name="attn_merge_out_proj",
    )(*outs, *lses, x, mod5, w_o)


K1_GROUP = 8
FUSED_STAGE1_ROWS = 256
STAGE2_CHUNK = 1024


def _dft_tables(s):
    n2 = FFT_N2
    n1 = s // n2
    c = FOURIER_GROUP_DIM
    ang_c = 2.0 * np.pi * np.outer(np.arange(c), np.arange(c)) / c
    cs_chan = np.concatenate([np.cos(ang_c), np.sin(ang_c)], axis=1) / np.sqrt(c)
    a1 = 2.0 * np.pi * np.outer(np.arange(n1), np.arange(n1)) / n1
    c1, s1 = np.cos(a1), np.sin(a1)
    w1 = np.kron(np.block([[c1, s1], [-s1, c1]]), np.eye(ROW_TILE))
    s2 = (np.arange(n2 // ROW_TILE)[:, None, None] * ROW_TILE + np.arange(ROW_TILE)[None, None, :])
    th = 2.0 * np.pi * np.arange(n1)[None, :, None] * s2 / s
    th = th.reshape(n2 // ROW_TILE, n1 * ROW_TILE, 1)
    a2 = 2.0 * np.pi * np.outer(np.arange(n2), np.arange(n2)) / n2
    cs2 = np.concatenate([np.cos(a2), np.sin(a2)], axis=1)
    return (jnp.asarray(cs_chan, BF16), jnp.asarray(w1, BF16),
            jnp.asarray(np.cos(th), F32), jnp.asarray(np.sin(th), F32), jnp.asarray(cs2, BF16))


def _chan_dft_kernel(x_ref, g_ref, sh_ref, sc_ref, cs_ref, zr_ref, zi_ref, h_ref, rs_ref, ab_ref, *, tm):
    _ada_norm_rows(x_ref, g_ref[...], sh_ref[...], sc_ref[...], h_ref, rs_ref, ab_ref, tm)
    c = FOURIER_GROUP_DIM
    for g in range(x_ref.shape[-1] // c):
        cols = slice(g * c, (g + 1) * c)
        r = jnp.dot(h_ref[:, cols], cs_ref[...], preferred_element_type=F32)
        zr_ref[:, cols] = r[:, :c].astype(zr_ref.dtype)
        zi_ref[:, cols] = (-r[:, c:]).astype(zi_ref.dtype)


def _chan_dft(x, mod5, gains, cs_chan, layer, tm):
    b, s, d = x.shape
    tok = pl.BlockSpec((None, tm, d), lambda bb, i: (bb, i, 0))
    kern = functools.partial(_chan_dft_kernel, tm=tm)
    return pl.pallas_call(
        kern,
        out_shape=(jax.ShapeDtypeStruct((b, s, d), BF16),) * 2,
        grid=(b, s // tm),
        in_specs=[tok, _row_in(layer, d, 2), _mod_in(layer, 0, d, 2), _mod_in(layer, 1, d, 2),
                  _const_spec(cs_chan.shape, 2)],
        out_specs=(tok, tok),
        scratch_shapes=[pltpu.VMEM((tm, d), BF16)] + _norm_scratch(tm, d),
        compiler_params=_params(2, 2),
        name="fourier_channel_dft",
    )(x, gains, mod5, mod5, cs_chan)


def _pos_dft1_kernel(zr_ref, zi_ref, w_ref, tc_ref, ts_ref, ur_ref, ui_ref, *, rows):
    tc = zr_ref.shape[-1]
    z = jnp.concatenate([zr_ref[...].reshape(rows, tc), zi_ref[...].reshape(rows, tc)], axis=0)
    t = jnp.dot(w_ref[...], z, preferred_element_type=F32)
    tr, ti = t[:rows], t[rows:]
    c, sn = tc_ref[...], ts_ref[...]
    ur_ref[...] = (tr * c + ti * sn).astype(ur_ref.dtype).reshape(ur_ref.shape)
    ui_ref[...] = (ti * c - tr * sn).astype(ui_ref.dtype).reshape(ui_ref.shape)


def _pos_dft1(zr, zi, w1, tw_c, tw_s):
    b, s, d = zr.shape
    n2 = FFT_N2
    n1 = s // n2
    rows = n1 * ROW_TILE
    tc = min(d, (1024 * 1024) // rows)
    view = lambda a: a.reshape(b, n1, n2 // ROW_TILE, ROW_TILE, d)
    blk = pl.BlockSpec((None, n1, None, ROW_TILE, tc), lambda bb, j, c: (bb, 0, j, 0, c))
    tw = pl.BlockSpec((None, rows, 1), lambda bb, j, c: (j, 0, 0))
    kern = functools.partial(_pos_dft1_kernel, rows=rows)
    ur, ui = pl.pallas_call(
        kern,
        out_shape=(jax.ShapeDtypeStruct((b, n1, n2 // ROW_TILE, ROW_TILE, d), BF16),) * 2,
        grid=(b, n2 // ROW_TILE, d // tc),
        in_specs=[blk, blk, _const_spec(w1.shape, 3), tw, tw],
        out_specs=(blk, blk),
        compiler_params=_params(3, 3),
        name="fourier_pos_dft_stage1",
    )(view(zr), view(zi), w1, tw_c, tw_s)
    return ur.reshape(b, s, d), ui.reshape(b, s, d)


def _fourier_short_kernel(xa_ref, xb_ref, g_ref, sh_ref, sc_ref, gate_ref, cs_ref, w1_ref, tc_ref, ts_ref,
                          cs2_ref, wf_ref, bf_ref, out_ref, ur_ref, ui_ref, yb_ref, *, n1, n_q, inv_norm):
    t = pl.program_id(1)
    n2 = cs2_ref.shape[0]
    d = yb_ref.shape[-1]
    rows = n1 * ROW_TILE
    n_tiles = n2 // ROW_TILE

    @pl.when(t < n_tiles)
    def _():
        def stage1(h_ref, z_ref, rs_ref, ab_ref):
            groups = NORM_CHUNK // ROW_TILE
            load = lambda c, cols: xa_ref[pl.ds(pl.multiple_of(c * groups, groups), groups), :, cols].reshape(
                NORM_CHUNK, LANES)
            _ada_norm_rows(xa_ref, g_ref[...], sh_ref[...], sc_ref[...], h_ref, rs_ref, ab_ref, rows, load)
            c = FOURIER_GROUP_DIM
            for g in range(d // c):
                cols = slice(g * c, (g + 1) * c)
                r = jnp.dot(h_ref[:, cols], cs_ref[...], preferred_element_type=F32)
                z_ref[0:rows, cols] = r[:, :c].astype(z_ref.dtype)
                z_ref[rows:, cols] = (-r[:, c:]).astype(z_ref.dtype)
            tt = jnp.dot(w1_ref[...], z_ref[...], preferred_element_type=F32)
            tr, ti = tt[:rows], tt[rows:]
            cw, sw = tc_ref[...], ts_ref[...]
            s2 = pl.ds(pl.multiple_of(t * ROW_TILE, ROW_TILE), ROW_TILE)
            ur_ref[:, s2, :] = (tr * cw + ti * sw).astype(ur_ref.dtype).reshape(n1, ROW_TILE, d)
            ui_ref[:, s2, :] = (ti * cw - tr * sw).astype(ui_ref.dtype).reshape(n1, ROW_TILE, d)

        pl.run_scoped(stage1, pltpu.VMEM((rows, d), BF16), pltpu.VMEM((2 * rows, d), BF16),
                      *_norm_scratch(rows, d))

    @pl.when(t >= n_tiles)
    def _():
        tb = t - n_tiles
        a, q = tb // n_q, tb % n_q

        @pl.when(q == 0)
        def _():
            def interleave(ys_ref):
                chunk_slabs = ys_ref.shape[0]
                chunk = chunk_slabs * LANES
                for cc in range(d // chunk):
                    cols = slice(cc * chunk, (cc + 1) * chunk)
                    for k in range(K1_GROUP):
                        k1 = a * K1_GROUP + k
                        u = jnp.concatenate([ur_ref[k1, :, cols], ui_ref[k1, :, cols]], axis=0)
                        y = jnp.dot(cs2_ref[...], u, preferred_element_type=F32) * inv_norm
                        for sl in range(chunk_slabs):
                            ys_ref[sl, pl.ds(k, n2, stride=K1_GROUP), :] = y[:, sl * LANES:(sl + 1) * LANES]
                    for sl in range(chunk_slabs):
                        c0 = cc * chunk + sl * LANES
                        yb_ref[:, c0:c0 + LANES] = ys_ref[sl].astype(yb_ref.dtype)

            pl.run_scoped(interleave, pltpu.VMEM((STAGE2_CHUNK // LANES, K1_GROUP * n2, LANES), F32))

        step_rows = xb_ref.shape[0] * K1_GROUP
        r0 = pl.multiple_of(q * step_rows, step_rows)
        proj = jnp.dot(yb_ref[pl.ds(r0, step_rows), :], wf_ref[...], preferred_element_type=F32) + bf_ref[...]
        out = xb_ref[...].reshape(proj.shape) + gate_ref[...] * proj
        out_ref[...] = out.reshape(out_ref.shape)


def _fourier_short(x, mod5, gains, tables, w_f, b_f, layer, w_layer):
    cs_chan, w1, tw_c, tw_s, cs2 = tables
    b, s, d = x.shape
    n2 = FFT_N2
    n1 = s // n2
    n_tiles = n2 // ROW_TILE
    n_a = n1 // K1_GROUP
    n_q = 4
    rows = n1 * ROW_TILE
    xa_view = x.reshape(b, n1, n_tiles, ROW_TILE, d)
    xb_view = x.reshape(b, n2, n_a, K1_GROUP, d)
    tile = lambda t: jnp.minimum(t, n_tiles - 1)
    late = lambda t: jnp.maximum(t - n_tiles, 0)
    xa_blk = pl.BlockSpec((None, n1, None, ROW_TILE, d), lambda bb, t: (bb, 0, tile(t), 0, 0))
    xb_blk = pl.BlockSpec((None, n2 // n_q, None, K1_GROUP, d),
                          lambda bb, t: (bb, late(t) % n_q, late(t) // n_q, 0, 0))
    tw = pl.BlockSpec((None, rows, 1), lambda bb, t: (tile(t), 0, 0))
    kern = functools.partial(_fourier_short_kernel, n1=n1, n_q=n_q, inv_norm=float(1.0 / np.sqrt(s)))
    out = pl.pallas_call(
        kern,
        out_shape=jax.ShapeDtypeStruct(xb_view.shape, F32),
        grid=(b, n_tiles + n_a * n_q),
        in_specs=[xa_blk, xb_blk, _row_in(layer, d, 2), _mod_in(layer, 0, d, 2), _mod_in(layer, 1, d, 2),
                  _mod_in(layer, 2, d, 2), _const_spec(cs_chan.shape, 2), _const_spec(w1.shape, 2), tw, tw,
                  _const_spec(cs2.shape, 2),
                  pl.BlockSpec((None, d, d), lambda bb, t: (w_layer, 0, 0), pipeline_mode=pl.Buffered(1)),
                  _row_in(w_layer, d, 2)],
        out_specs=xb_blk,
        scratch_shapes=[pltpu.VMEM((n1, n2, d), BF16), pltpu.VMEM((n1, n2, d), BF16),
                        pltpu.VMEM((K1_GROUP * n2, d), BF16)],
        compiler_params=_params(2, 1),
        name="fourier_mixer_short",
    )(xa_view, xb_view, gains, mod5, mod5, mod5, cs_chan, w1, tw_c, tw_s, cs2, w_f, b_f)
    return out.reshape(b, s, d)


def _pos_dft2_proj_kernel(ur_ref, ui_ref, cs2_ref, w_ref, b_ref, x_ref, gate_ref, out_ref, ys_ref, yb_ref, *,
                          inv_norm):
    n2 = cs2_ref.shape[0]
    d = ur_ref.shape[-1]
    chunk_slabs = ys_ref.shape[0]
    chunk = chunk_slabs * LANES

    @pl.when(pl.program_id(2) == 0)
    def _():
        for cc in range(d // chunk):
            for k in range(K1_GROUP):
                rows = slice(k * n2, (k + 1) * n2)
                cols = slice(cc * chunk, (cc + 1) * chunk)
                u = jnp.concatenate([ur_ref[rows, cols], ui_ref[rows, cols]], axis=0)
                y = jnp.dot(cs2_ref[...], u, preferred_element_type=F32) * inv_norm
                for sl in range(chunk_slabs):
                    ys_ref[sl, pl.ds(k, n2, stride=K1_GROUP), :] = y[:, sl * LANES:(sl + 1) * LANES]
            for sl in range(chunk_slabs):
                c0 = cc * chunk + sl * LANES
                yb_ref[:, c0:c0 + LANES] = ys_ref[sl].astype(yb_ref.dtype)

    step_rows = x_ref.shape[0] * K1_GROUP
    r0 = pl.multiple_of(pl.program_id(2) * step_rows, step_rows)
    proj = jnp.dot(yb_ref[pl.ds(r0, step_rows), :], w_ref[...], preferred_element_type=F32) + b_ref[...]
    out = x_ref[...].reshape(proj.shape) + gate_ref[...] * proj
    out_ref[...] = out.reshape(out_ref.shape)


def _pos_dft2_proj(ur, ui, cs2, x, mod5, w_f, b_f, layer, w_layer):
    b, s, d = x.shape
    n2 = FFT_N2
    n1 = s // n2
    n_q = 2
    rows = K1_GROUP * n2
    n_a = n1 // K1_GROUP

    def u_blk(first_step):
        def imap(bb, a, q):
            nxt = jnp.minimum(bb * n_a + a + (q >= first_step).astype(jnp.int32), b * n_a - 1)
            return nxt // n_a, nxt % n_a, 0
        return pl.BlockSpec((None, rows, d), imap)

    x_view = x.reshape(b, n2, n_a, K1_GROUP, d)
    x_blk = pl.BlockSpec((None, n2 // n_q, None, K1_GROUP, d), lambda bb, a, q: (bb, q, a, 0, 0))
    kern = functools.partial(_pos_dft2_proj_kernel, inv_norm=float(1.0 / np.sqrt(s)))
    out = pl.pallas_call(
        kern,
        out_shape=jax.ShapeDtypeStruct(x_view.shape, F32),
        grid=(b, n_a, n_q),
        in_specs=[u_blk(1), u_blk(1), _const_spec(cs2.shape, 3),
                  pl.BlockSpec((None, d, d), lambda bb, a, q: (w_layer, 0, 0), pipeline_mode=pl.Buffered(1)),
                  _row_in(w_layer, d, 3), x_blk, _mod_in(layer, 2, d, 3)],
        out_specs=x_blk,
        scratch_shapes=[pltpu.VMEM((STAGE2_CHUNK // LANES, rows, LANES), F32), pltpu.VMEM((rows, d), BF16)],
        compiler_params=_params(3, 2),
        name="fourier_pos_dft_stage2_proj",
    )(ur, ui, cs2, w_f, b_f, x_view, mod5)
    return out.reshape(b, s, d)


def _mlp_kernel(x_ref, g_ref, sh_ref, sc_ref, gate_ref, w1_ref, b1_ref, w2_ref, b2_ref, fg_ref,
                out_ref, h_ref, acc_ref, rs_ref, ab_ref, *, tm, final_norm):
    j = pl.program_id(2)

    @pl.when(j == 0)
    def _():
        _ada_norm_rows(x_ref, g_ref[...], sh_ref[...], sc_ref[...], h_ref, rs_ref, ab_ref, tm, zero_ref=acc_ref)

    u = jnp.dot(h_ref[...], w1_ref[...], preferred_element_type=F32) + b1_ref[...]
    u = jnp.maximum(u, 0.0)
    acc_ref[...] += jnp.dot((u * u).astype(BF16), w2_ref[...], preferred_element_type=F32)

    @pl.when(j == pl.num_programs(2) - 1)
    def _():
        y = x_ref[...] + gate_ref[...] * (acc_ref[...] + b2_ref[...])
        if final_norm:
            ms = jnp.mean(y * y, axis=-1, keepdims=True)
            y = y * lax.rsqrt(ms + RMS_EPS) * fg_ref[...]
        out_ref[...] = y


def _mlp(x, mod5, gains, w1, b1, w2, b2, final_g, layer, tm, tf, final_norm):
    b, s, d = x.shape
    dff = w1.shape[-1]
    kern = functools.partial(_mlp_kernel, tm=tm, final_norm=final_norm)
    tok = pl.BlockSpec((None, tm, d), lambda bb, i, j: (bb, i, 0))
    return pl.pallas_call(
        kern,
        out_shape=jax.ShapeDtypeStruct((b, s, d), F32),
        grid=(b, s // tm, dff // tf),
        in_specs=[
            tok,
            _row_in(layer, d, 3),
            _mod_in(layer, 3, d, 3),
            _mod_in(layer, 4, d, 3),
            _mod_in(layer, 5, d, 3),
            pl.BlockSpec((None, d, tf), lambda bb, i, j: (layer, 0, j)),
            pl.BlockSpec((None, 1, tf), lambda bb, i, j: (layer, 0, j)),
            pl.BlockSpec((None, tf, d), lambda bb, i, j: (layer, j, 0)),
            _row_in(layer, d, 3),
            pl.BlockSpec((1, d), lambda bb, i, j: (0, 0)),
        ],
        out_specs=tok,
        scratch_shapes=[pltpu.VMEM((tm, d), BF16), pltpu.VMEM((tm, d), F32)] + _norm_scratch(tm, d),
        compiler_params=_params(3, 2),
        name="sqrelu_mlp",
    )(x, gains, mod5, mod5, mod5, w1, b1, w2, b2, final_g)


def _trunk(x, mod5, p):
    depth = p["w1"].shape[0]
    s = x.shape[1]
    cs_chan, w1c, tw_c, tw_s, cs2 = _dft_tables(s)
    for i in range(depth):
        sub = i // 2
        if i % 2 == 0:
            qkv = _norm_proj(x, mod5, p["norm1_g"], p["w_qkv"], p["perms"], i, sub, tm=1024, tn=768)
            outs, lses = zip(*[_group_attention(qkv, g, p["perms_t"]) for g in range(len(ATTN_WINDOWS))])
            x = _merge_proj(outs, lses, x, mod5, p["w_o"], i, sub, tm=512)
        else:
            if (s // FFT_N2) * ROW_TILE <= FUSED_STAGE1_ROWS:
                x = _fourier_short(x, mod5, p["norm1_g"], (cs_chan, w1c, tw_c, tw_s, cs2), p["w_f"], p["b_f"],
                                   i, sub)
            else:
                zr, zi = _chan_dft(x, mod5, p["norm1_g"], cs_chan, i, tm=512)
                ur, ui = _pos_dft1(zr, zi, w1c, tw_c, tw_s)
                x = _pos_dft2_proj(ur, ui, cs2, x, mod5, p["w_f"], p["b_f"], i, sub)
        x = _mlp(x, mod5, p["norm2_g"], p["w1"], p["b1"], p["w2"], p["b2"], p["final_g"], i,
                 tm=512, tf=1024, final_norm=(i == depth - 1))
    return x


def kernel(x_prompt, x_sample, c_prompt, c_sample, w_ada, b_ada, norm1_g, norm2_g, w_qkv, w_o, w_f, b_f,
           w1, b1, w2, b2, final_g):
    depth, d, _ = w_ada.shape
    n_p, n_s = c_prompt.shape[0], c_sample.shape[0]
    rows = -(-(n_p + n_s) // ROW_TILE) * ROW_TILE
    c_all = jnp.concatenate([c_prompt, c_sample, jnp.zeros((rows - n_p - n_s, d), F32)], axis=0)
    mod = _modulation(c_all, w_ada, b_ada)
    mod_p = mod[:, :n_p].reshape(depth, n_p, N_MOD, 1, d)
    mod_s = mod[:, n_p:n_p + n_s].reshape(depth, n_s, N_MOD, 1, d)

    row3 = lambda a: a.reshape(a.shape[0], 1, a.shape[-1])
    perms = _residue_perms()
    params = {
        "norm1_g": row3(norm1_g), "norm2_g": row3(norm2_g),
        "w_qkv": w_qkv.astype(BF16), "w_o": w_o.astype(BF16),
        "w_f": w_f.astype(BF16), "b_f": row3(b_f),
        "w1": w1.astype(BF16), "b1": row3(b1), "w2": w2.astype(BF16), "b2": row3(b2),
        "final_g": final_g.reshape(1, d),
        "perms": jnp.asarray(perms, BF16),
        "perms_t": jnp.asarray(np.transpose(perms, (0, 2, 1)), BF16),
    }
    return (_trunk(x_prompt, mod_p, params), _trunk(x_sample, mod_s, params))
```

```python
import functools

import numpy as np
import jax
import jax.numpy as jnp
from jax import lax
from jax.experimental import pallas as pl
from jax.experimental.pallas import tpu as pltpu

F32 = jnp.float32
BF16 = jnp.bfloat16

N_MOD = 6
RMS_EPS = 1e-6
MASK_VALUE = -1e30
ATTN_WINDOWS = ((128, 1), (512, 4), (2048, 16))
HEADS_PER_GROUP = 6
HEAD_DIM = 128
N_ATTN_HEADS = len(ATTN_WINDOWS) * HEADS_PER_GROUP
GROUP_WIDTH = HEADS_PER_GROUP * HEAD_DIM
FOURIER_GROUP_DIM = 256
HALF = 64
FFT_N2 = 128
LANES = 128
ROW_TILE = 16
PERM_BLOCK = 256
VMEM_LIMIT = 56 * 1024 * 1024


def _params(n_axes, n_parallel):
    sem = ("parallel",) * n_parallel + ("arbitrary",) * (n_axes - n_parallel)
    return pltpu.CompilerParams(dimension_semantics=sem, vmem_limit_bytes=VMEM_LIMIT)


def _const_spec(shape, n_grid):
    zeros = (0,) * len(shape)
    imap = (lambda a, b: zeros) if n_grid == 2 else (lambda a, b, c: zeros)
    return pl.BlockSpec(shape, imap, pipeline_mode=pl.Buffered(1))


NORM_CHUNK = 64


def _norm_scratch(tm, d):
    return [pltpu.VMEM((tm, LANES), F32), pltpu.VMEM((2, d), F32)]


def _ada_norm_rows(x_ref, gain, shift, scale, h_ref, rs_ref, ab_ref, rows, load=None, zero_ref=None):
    d = h_ref.shape[-1]
    slabs = d // LANES
    if load is None:
        load = lambda c, cols: x_ref[pl.ds(pl.multiple_of(c * NORM_CHUNK, NORM_CHUNK), NORM_CHUNK), cols]
    ab_ref[0:1, :] = gain * (1.0 + scale)
    ab_ref[1:2, :] = shift

    def stats(c, carry):
        r0 = pl.multiple_of(c * NORM_CHUNK, NORM_CHUNK)
        acc = jnp.zeros((NORM_CHUNK, LANES), F32)
        for t in range(slabs):
            xt = load(c, slice(t * LANES, (t + 1) * LANES))
            acc = acc + xt * xt
        ms = jnp.sum(acc, axis=-1, keepdims=True) * (1.0 / d)
        rs_ref[pl.ds(r0, NORM_CHUNK), :] = jnp.broadcast_to(lax.rsqrt(ms + RMS_EPS), (NORM_CHUNK, LANES))
        return carry

    lax.fori_loop(0, rows // NORM_CHUNK, stats, 0, unroll=4)

    def apply(c, carry):
        r0 = pl.multiple_of(c * NORM_CHUNK, NORM_CHUNK)
        rs = rs_ref[pl.ds(r0, NORM_CHUNK), :]
        for t in range(slabs):
            cols = slice(t * LANES, (t + 1) * LANES)
            h = load(c, cols) * rs * ab_ref[0:1, cols] + ab_ref[1:2, cols]
            h_ref[pl.ds(r0, NORM_CHUNK), cols] = h.astype(h_ref.dtype)
            if zero_ref is not None:
                zero_ref[pl.ds(r0, NORM_CHUNK), cols] = jnp.zeros((NORM_CHUNK, LANES), zero_ref.dtype)
        return carry

    lax.fori_loop(0, rows // NORM_CHUNK, apply, 0)


def _ada_norm_chunks(x_ref, gain, shift, scale, h_ref, row0, rows):
    d = h_ref.shape[-1]
    slabs = d // LANES
    a = gain * (1.0 + scale)
    for c in range(rows // NORM_CHUNK):
        r0 = pl.multiple_of(row0 + c * NORM_CHUNK, NORM_CHUNK)
        acc = jnp.zeros((NORM_CHUNK, LANES), F32)
        for t in range(slabs):
            xt = x_ref[pl.ds(r0, NORM_CHUNK), t * LANES:(t + 1) * LANES]
            acc = acc + xt * xt
        ms = jnp.sum(acc, axis=-1, keepdims=True) * (1.0 / d)
        rs = jnp.broadcast_to(lax.rsqrt(ms + RMS_EPS), (NORM_CHUNK, LANES))
        for t in range(slabs):
            cols = slice(t * LANES, (t + 1) * LANES)
            h = x_ref[pl.ds(r0, NORM_CHUNK), cols] * rs * a[:, cols] + shift[:, cols]
            h_ref[pl.ds(r0, NORM_CHUNK), cols] = h.astype(h_ref.dtype)


def _mod_kernel(c_ref, w_ref, b_ref, o_ref):
    c = c_ref[...]
    act = (c * jax.nn.sigmoid(c)).astype(BF16)
    w = w_ref[...].astype(BF16)
    o_ref[...] = jnp.dot(act, w, preferred_element_type=F32) + b_ref[...]


def _modulation(c_all, w_ada, b_ada):
    depth, d, n = w_ada.shape
    rows = c_all.shape[0]
    tn = 1024
    return pl.pallas_call(
        _mod_kernel,
        out_shape=jax.ShapeDtypeStruct((depth, rows, n), F32),
        grid=(depth, n // tn),
        in_specs=[
            pl.BlockSpec((rows, d), lambda l, j: (0, 0)),
            pl.BlockSpec((None, d, tn), lambda l, j: (l, 0, j)),
            pl.BlockSpec((None, 1, tn), lambda l, j: (l, 0, j)),
        ],
        out_specs=pl.BlockSpec((None, rows, tn), lambda l, j: (l, 0, j)),
        compiler_params=_params(2, 2),
        name="adaln_modulation",
    )(c_all, w_ada, b_ada.reshape(depth, 1, n))


def _mod_in(layer, which, d, n_grid):
    if n_grid == 2:
        imap = lambda b, i: (layer, b, which, 0, 0)
    else:
        imap = lambda b, i, j: (layer, b, which, 0, 0)
    return pl.BlockSpec((None, None, None, 1, d), imap)


def _row_in(layer, d, n_grid):
    if n_grid == 2:
        imap = lambda b, i: (layer, 0, 0)
    else:
        imap = lambda b, i, j: (layer, 0, 0)
    return pl.BlockSpec((None, 1, d), imap)


def _residue_perms():
    mats = []
    for _, dil in ATTN_WINDOWS[1:]:
        ub = PERM_BLOCK // dil
        p = np.zeros((PERM_BLOCK, PERM_BLOCK), np.float32)
        nat = np.arange(PERM_BLOCK)
        p[(nat % dil) * ub + nat // dil, nat] = 1.0
        mats.append(p)
    return np.stack(mats)


def _norm_proj_kernel(xa_ref, xb_ref, g_ref, sh_ref, sc_ref, shn_ref, scn_ref, w_ref, p_ref, o_ref,
                      h0_ref, h1_ref, hp_ref, rs_ref, ab_ref, *, tm, steps_per_group):
    i, j = pl.program_id(1), pl.program_id(2)
    tile = pl.program_id(0) * pl.num_programs(1) + i
    half = tm // 2
    spg = steps_per_group
    gain = g_ref[...]
    rows_per_step = -(-half // (spg * NORM_CHUNK)) * NORM_CHUNK

    @pl.when((tile == 0) & (j == 0))
    def _():
        _ada_norm_rows(xa_ref, gain, sh_ref[...], sc_ref[...], h0_ref.at[0:half], rs_ref, ab_ref, half)
        _ada_norm_rows(xb_ref, gain, sh_ref[...], sc_ref[...], h0_ref.at[half:tm], rs_ref, ab_ref, half)

    def norm_ahead(x_ref, h_rows, step):
        row0 = jnp.minimum(step * rows_per_step, half - rows_per_step)
        _ada_norm_chunks(x_ref, gain, shn_ref[...], scn_ref[...], h_rows, row0, rows_per_step)

    def project(lhs_ref):
        o_ref[...] = jnp.dot(lhs_ref[...], w_ref[...], preferred_element_type=F32).astype(o_ref.dtype)

    def column_step(h_cur, h_next):
        @pl.when((j >= spg) & (j % spg == 0))
        def _():
            for blk in range(tm // PERM_BLOCK):
                rows = slice(blk * PERM_BLOCK, (blk + 1) * PERM_BLOCK)
                hp_ref[rows, :] = jnp.dot(p_ref[...], h_cur[rows, :],
                                          preferred_element_type=F32).astype(hp_ref.dtype)

        @pl.when(j < spg)
        def _():
            project(h_cur)

        @pl.when((j >= spg) & (j < 2 * spg))
        def _():
            norm_ahead(xa_ref, h_next.at[0:half], j - spg)
            project(hp_ref)

        @pl.when(j >= 2 * spg)
        def _():
            norm_ahead(xb_ref, h_next.at[half:tm], j - 2 * spg)
            project(hp_ref)

    @pl.when(tile % 2 == 0)
    def _():
        column_step(h0_ref, h1_ref)

    @pl.when(tile % 2 == 1)
    def _():
        column_step(h1_ref, h0_ref)


def _norm_proj(x, mod5, gains, w, perms, layer, w_layer, tm, tn):
    b, s, d = x.shape
    n = w.shape[-1]
    n_i = s // tm
    spg = (n // len(ATTN_WINDOWS)) // tn
    assert n // tn == len(ATTN_WINDOWS) * spg
    kern = functools.partial(_norm_proj_kernel, tm=tm, steps_per_group=spg)

    def tile_ahead(bb, i, ahead):
        t1 = jnp.minimum(bb * n_i + i + ahead, b * n_i - 1)
        return t1 // n_i, t1 % n_i

    def x_half(which, first_step):
        def imap(bb, i, j):
            b1, i1 = tile_ahead(bb, i, (j >= first_step).astype(jnp.int32))
            return b1, 2 * i1 + which, 0
        return pl.BlockSpec((None, tm // 2, d), imap)

    mod_next = lambda which: pl.BlockSpec((None, None, None, 1, d),
                                          lambda bb, i, j: (layer, tile_ahead(bb, i, 1)[0], which, 0, 0))
    return pl.pallas_call(
        kern,
        out_shape=jax.ShapeDtypeStruct((b, s, n), BF16),
        grid=(b, n_i, n // tn),
        in_specs=[
            x_half(0, spg),
            x_half(1, 2 * spg),
            _row_in(layer, d, 3),
            _mod_in(layer, 0, d, 3),
            _mod_in(layer, 1, d, 3),
            mod_next(0),
            mod_next(1),
            pl.BlockSpec((None, d, tn), lambda bb, i, j: (w_layer, 0, j)),
            pl.BlockSpec((None, PERM_BLOCK, PERM_BLOCK),
                         lambda bb, i, j: (jnp.maximum(j // spg, 1) - 1, 0, 0)),
        ],
        out_specs=pl.BlockSpec((None, tm, tn), lambda bb, i, j: (bb, i, j)),
        scratch_shapes=[pltpu.VMEM((tm, d), BF16)] * 3 + _norm_scratch(tm // 2, d),
        compiler_params=_params(3, 0),
        name="norm_qkv_proj",
    )(x, x, gains, mod5, mod5, mod5, mod5, w, perms)


def _attn_kernel(q_ref, kp_ref, km_ref, kn_ref, vp_ref, vm_ref, vn_ref, pt_ref, o_ref, lse_ref,
                 kc_ref, vc_ref, op_ref, lp_ref, bias_ref=None, *, nblk, ub, n_u, dilation, slopes):
    i = pl.program_id(1)
    tq = nblk * ub
    width = q_ref.shape[-1]
    sub = 2 * HALF
    span = sub + 2 * HALF
    row = lax.broadcasted_iota(jnp.int32, (sub, span), 0)
    col = lax.broadcasted_iota(jnp.int32, (sub, span), 1)
    adu = jnp.abs(col - HALF - row)
    band = adu <= HALF
    dist = (adu * dilation).astype(F32)
    lane = lax.broadcasted_iota(jnp.int32, (sub, LANES), 1)
    scale = HEAD_DIM ** -0.5
    blocks_per_sub = sub // ub

    def key_valid(sb):
        u_key = i * tq + (sb * sub - HALF) + col
        return band & (u_key >= 0) & (u_key < n_u)

    def one_class(r):
        def gather(dst, prev, main, nxt):
            dst[0:HALF, :] = prev[:, r].reshape(HALF, width)
            dst[HALF:HALF + tq, :] = main[:, r].reshape(tq, width)
            dst[HALF + tq:, :] = nxt[:, r].reshape(HALF, width)

        gather(kc_ref, kp_ref, km_ref, kn_ref)
        gather(vc_ref, vp_ref, vm_ref, vn_ref)
        for sb in range(tq // sub):
            if bias_ref is None:
                valid = key_valid(sb)
            lse_tile = jnp.zeros((sub, LANES), F32)
            rows = slice(sb * sub, (sb + 1) * sub)
            for h in range(HEADS_PER_GROUP):
                cs = slice(h * HEAD_DIM, (h + 1) * HEAD_DIM)
                q = q_ref[sb * blocks_per_sub:(sb + 1) * blocks_per_sub, r, :, cs].reshape(sub, HEAD_DIM)
                k = kc_ref[sb * sub:sb * sub + span, cs]
                v = vc_ref[sb * sub:sb * sub + span, cs]
                s = lax.dot_general(q, k, (((1,), (1,)), ((), ())), preferred_element_type=F32) * scale
                if bias_ref is None:
                    s = jnp.where(valid, s - slopes[h] * dist, MASK_VALUE)
                else:
                    s = s + bias_ref[sb, h]
                m = jnp.max(s, axis=-1, keepdims=True)
                p = jnp.exp(s - m)
                l = jnp.sum(p, axis=-1, keepdims=True)
                o = jnp.dot(p.astype(BF16), v, preferred_element_type=F32) * (1.0 / l)
                if dilation == 1:
                    o_ref[rows, cs] = o.astype(o_ref.dtype)
                else:
                    op_ref[r, rows, cs] = o.astype(op_ref.dtype)
                lse_tile = jnp.where(lane == h, m + jnp.log(l), lse_tile)
            if dilation == 1:
                lse_ref[rows, :] = lse_tile
            else:
                lp_ref[r, rows, :] = lse_tile

    if dilation == 1:
        one_class(0)
        return

    for sb in range(tq // sub):
        valid = key_valid(sb)
        for h in range(HEADS_PER_GROUP):
            bias_ref[sb, h] = jnp.where(valid, -slopes[h] * dist, MASK_VALUE)

    def class_step(r, carry):
        one_class(r)
        return carry

    lax.fori_loop(0, dilation, class_step, 0, unroll=min(dilation, 8))
    pt = pt_ref[...]
    for blk in range(nblk):
        urows = slice(blk * ub, (blk + 1) * ub)
        nat = slice(blk * PERM_BLOCK, (blk + 1) * PERM_BLOCK)
        ob = jnp.concatenate([op_ref[rr, urows, :] for rr in range(dilation)], axis=0)
        o_ref[nat, :] = jnp.dot(pt, ob, preferred_element_type=F32).astype(o_ref.dtype)
        lb = jnp.concatenate([lp_ref[rr, urows, :] for rr in range(dilation)], axis=0)
        hi = lb.astype(BF16)
        rest = lb - hi.astype(F32)
        mid = rest.astype(BF16)
        lo = (rest - mid.astype(F32)).astype(BF16)
        lse_ref[nat, :] = (jnp.dot(pt, hi, preferred_element_type=F32)
                           + jnp.dot(pt, mid, preferred_element_type=F32)
                           + jnp.dot(pt, lo, preferred_element_type=F32))


def _group_attention(qkv, group, perms_t):
    b, s, width = qkv.shape
    _, dilation = ATTN_WINDOWS[group]
    n_u = s // dilation
    ub = HALF if dilation == 1 else PERM_BLOCK // dilation
    rb = ub * dilation
    tq = min({1: 512, 4: 256, 16: 128}[dilation], n_u)
    nblk = tq // ub
    hb = HALF // ub
    n_slabs = width // GROUP_WIDTH
    view = qkv.reshape(b, s // rb, dilation, ub, width)
    n_halo = (s // rb) // hb
    slopes = tuple(float(np.exp2(np.float32(-8.0 * (group * HEADS_PER_GROUP + h + 1) / N_ATTN_HEADS)))
                   for h in range(HEADS_PER_GROUP))

    def main(t):
        return pl.BlockSpec((None, nblk, dilation, ub, GROUP_WIDTH),
                            lambda bb, i: (bb, i, 0, 0, group * 3 + t))

    def halo_p(t):
        return pl.BlockSpec((None, hb, dilation, ub, GROUP_WIDTH),
                            lambda bb, i: (bb, jnp.maximum(i * (nblk // hb) - 1, 0), 0, 0, group * 3 + t))

    def halo_n(t):
        return pl.BlockSpec((None, hb, dilation, ub, GROUP_WIDTH),
                            lambda bb, i: (bb, jnp.minimum((i + 1) * (nblk // hb), n_halo - 1), 0, 0,
                                           group * 3 + t))

    assert nblk % hb == 0 and width == n_slabs * GROUP_WIDTH
    pt = perms_t[max(group - 1, 0)]
    kern = functools.partial(_attn_kernel, nblk=nblk, ub=ub, n_u=n_u, dilation=dilation, slopes=slopes)
    scratch = [pltpu.VMEM((tq + 2 * HALF, GROUP_WIDTH), BF16), pltpu.VMEM((tq + 2 * HALF, GROUP_WIDTH), BF16),
               pltpu.VMEM((dilation, tq, GROUP_WIDTH), BF16), pltpu.VMEM((dilation, tq, LANES), F32)]
    if dilation > 1:
        scratch.append(pltpu.VMEM((tq // (2 * HALF), HEADS_PER_GROUP, 2 * HALF, 4 * HALF), F32))
    rows = nblk * rb
    return pl.pallas_call(
        kern,
        out_shape=(jax.ShapeDtypeStruct((b, s, GROUP_WIDTH), BF16),
                   jax.ShapeDtypeStruct((b, s, LANES), F32)),
        grid=(b, s // rows),
        in_specs=[main(0), halo_p(1), main(1), halo_n(1), halo_p(2), main(2), halo_n(2),
                  _const_spec(pt.shape, 2)],
        out_specs=(pl.BlockSpec((None, rows, GROUP_WIDTH), lambda bb, i: (bb, i, 0)),
                   pl.BlockSpec((None, rows, LANES), lambda bb, i: (bb, i, 0))),
        scratch_shapes=scratch,
        compiler_params=_params(2, 2),
        name=f"dilated_attention_g{group}",
    )(view, view, view, view, view, view, view, pt)


def _merge_proj_kernel(o0_ref, o1_ref, o2_ref, l0_ref, l1_ref, l2_ref, x_ref, gate_ref, w_ref,
                       out_ref, mix_ref):
    l0, l1, l2 = l0_ref[...], l1_ref[...], l2_ref[...]
    m = jnp.maximum(jnp.maximum(l0, l1), l2)
    e0, e1, e2 = jnp.exp(l0 - m), jnp.exp(l1 - m), jnp.exp(l2 - m)
    inv = 1.0 / (e0 + e1 + e2)
    for g, (o_ref, e) in enumerate(((o0_ref, e0), (o1_ref, e1), (o2_ref, e2))):
        alpha = e * inv
        for h in range(HEADS_PER_GROUP):
            src = slice(h * HEAD_DIM, (h + 1) * HEAD_DIM)
            dst = slice((g * HEADS_PER_GROUP + h) * HEAD_DIM, (g * HEADS_PER_GROUP + h + 1) * HEAD_DIM)
            mix_ref[:, dst] = (alpha[:, h:h + 1] * o_ref[:, src].astype(F32)).astype(mix_ref.dtype)
    y = jnp.dot(mix_ref[...], w_ref[...], preferred_element_type=F32)
    out_ref[...] = x_ref[...] + gate_ref[...] * y


def _merge_proj(outs, lses, x, mod5, w_o, layer, w_layer, tm):
    b, s, d = x.shape
    width = w_o.shape[1]
    tok = lambda c: pl.BlockSpec((None, tm, c), lambda bb, i: (bb, i, 0))
    return pl.pallas_call(
        _merge_proj_kernel,
        out_shape=jax.ShapeDtypeStruct((b, s, d), F32),
        grid=(b, s // tm),
        in_specs=[tok(GROUP_WIDTH)] * 3 + [tok(LANES)] * 3 + [
            tok(d),
            _mod_in(layer, 2, d, 2),
            pl.BlockSpec((None, width, d), lambda bb, i: (w_layer, 0, 0)),
        ],
        out_specs=tok(d),
        scratch_shapes=[pltpu.VMEM((tm, width), BF16)],
        compiler_params=_params(2, 2),
        name="attn_merge_out_proj",
    )(*outs, *lses, x, mod5, w_o)


K1_GROUP = 8
FUSED_STAGE1_ROWS = 256
STAGE2_CHUNK = 1024


def _dft_tables(s):
    n2 = FFT_N2
    n1 = s // n2
    c = FOURIER_GROUP_DIM
    ang_c = 2.0 * np.pi * np.outer(np.arange(c), np.arange(c)) / c
    cs_chan = np.concatenate([np.cos(ang_c), np.sin(ang_c)], axis=1) / np.sqrt(c)
    a1 = 2.0 * np.pi * np.outer(np.arange(n1), np.arange(n1)) / n1
    c1, s1 = np.cos(a1), np.sin(a1)
    w1 = np.kron(np.block([[c1, s1], [-s1, c1]]), np.eye(ROW_TILE))
    s2 = (np.arange(n2 // ROW_TILE)[:, None, None] * ROW_TILE + np.arange(ROW_TILE)[None, None, :])
    th = 2.0 * np.pi * np.arange(n1)[None, :, None] * s2 / s
    th = th.reshape(n2 // ROW_TILE, n1 * ROW_TILE, 1)
    a2 = 2.0 * np.pi * np.outer(np.arange(n2), np.arange(n2)) / n2
    cs2 = np.concatenate([np.cos(a2), np.sin(a2)], axis=1)
    return (jnp.asarray(cs_chan, BF16), jnp.asarray(w1, BF16),
            jnp.asarray(np.cos(th), F32), jnp.asarray(np.sin(th), F32), jnp.asarray(cs2, BF16))


def _chan_dft_kernel(x_ref, g_ref, sh_ref, sc_ref, cs_ref, zr_ref, zi_ref, h_ref, rs_ref, ab_ref, *, tm):
    _ada_norm_rows(x_ref, g_ref[...], sh_ref[...], sc_ref[...], h_ref, rs_ref, ab_ref, tm)
    c = FOURIER_GROUP_DIM
    for g in range(x_ref.shape[-1] // c):
        cols = slice(g * c, (g + 1) * c)
        r = jnp.dot(h_ref[:, cols], cs_ref[...], preferred_element_type=F32)
        zr_ref[:, cols] = r[:, :c].astype(zr_ref.dtype)
        zi_ref[:, cols] = (-r[:, c:]).astype(zi_ref.dtype)


def _chan_dft(x, mod5, gains, cs_chan, layer, tm):
    b, s, d = x.shape
    tok = pl.BlockSpec((None, tm, d), lambda bb, i: (bb, i, 0))
    kern = functools.partial(_chan_dft_kernel, tm=tm)
    return pl.pallas_call(
        kern,
        out_shape=(jax.ShapeDtypeStruct((b, s, d), BF16),) * 2,
        grid=(b, s // tm),
        in_specs=[tok, _row_in(layer, d, 2), _mod_in(layer, 0, d, 2), _mod_in(layer, 1, d, 2),
                  _const_spec(cs_chan.shape, 2)],
        out_specs=(tok, tok),
        scratch_shapes=[pltpu.VMEM((tm, d), BF16)] + _norm_scratch(tm, d),
        compiler_params=_params(2, 2),
        name="fourier_channel_dft",
    )(x, gains, mod5, mod5, cs_chan)


def _pos_dft1_kernel(zr_ref, zi_ref, w_ref, tc_ref, ts_ref, ur_ref, ui_ref, *, rows):
    tc = zr_ref.shape[-1]
    z = jnp.concatenate([zr_ref[...].reshape(rows, tc), zi_ref[...].reshape(rows, tc)], axis=0)
    t = jnp.dot(w_ref[...], z, preferred_element_type=F32)
    tr, ti = t[:rows], t[rows:]
    c, sn = tc_ref[...], ts_ref[...]
    ur_ref[...] = (tr * c + ti * sn).astype(ur_ref.dtype).reshape(ur_ref.shape)
    ui_ref[...] = (ti * c - tr * sn).astype(ui_ref.dtype).reshape(ui_ref.shape)


def _pos_dft1(zr, zi, w1, tw_c, tw_s):
    b, s, d = zr.shape
    n2 = FFT_N2
    n1 = s // n2
    rows = n1 * ROW_TILE
    tc = min(d, (1024 * 1024) // rows)
    view = lambda a: a.reshape(b, n1, n2 // ROW_TILE, ROW_TILE, d)
    blk = pl.BlockSpec((None, n1, None, ROW_TILE, tc), lambda bb, j, c: (bb, 0, j, 0, c))
    tw = pl.BlockSpec((None, rows, 1), lambda bb, j, c: (j, 0, 0))
    kern = functools.partial(_pos_dft1_kernel, rows=rows)
    ur, ui = pl.pallas_call(
        kern,
        out_shape=(jax.ShapeDtypeStruct((b, n1, n2 // ROW_TILE, ROW_TILE, d), BF16),) * 2,
        grid=(b, n2 // ROW_TILE, d // tc),
        in_specs=[blk, blk, _const_spec(w1.shape, 3), tw, tw],
        out_specs=(blk, blk),
        compiler_params=_params(3, 3),
        name="fourier_pos_dft_stage1",
    )(view(zr), view(zi), w1, tw_c, tw_s)
    return ur.reshape(b, s, d), ui.reshape(b, s, d)


def _fourier_short_kernel(xa_ref, xb_ref, g_ref, sh_ref, sc_ref, gate_ref, cs_ref, w1_ref, tc_ref, ts_ref,
                          cs2_ref, wf_ref, bf_ref, out_ref, ur_ref, ui_ref, yb_ref, *, n1, n_q, inv_norm):
    t = pl.program_id(1)
    n2 = cs2_ref.shape[0]
    d = yb_ref.shape[-1]
    rows = n1 * ROW_TILE
    n_tiles = n2 // ROW_TILE

    @pl.when(t < n_tiles)
    def _():
        def stage1(h_ref, z_ref, rs_ref, ab_ref):
            groups = NORM_CHUNK // ROW_TILE
            load = lambda c, cols: xa_ref[pl.ds(pl.multiple_of(c * groups, groups), groups), :, cols].reshape(
                NORM_CHUNK, LANES)
            _ada_norm_rows(xa_ref, g_ref[...], sh_ref[...], sc_ref[...], h_ref, rs_ref, ab_ref, rows, load)
            c = FOURIER_GROUP_DIM
            for g in range(d // c):
                cols = slice(g * c, (g + 1) * c)
                r = jnp.dot(h_ref[:, cols], cs_ref[...], preferred_element_type=F32)
                z_ref[0:rows, cols] = r[:, :c].astype(z_ref.dtype)
                z_ref[rows:, cols] = (-r[:, c:]).astype(z_ref.dtype)
            tt = jnp.dot(w1_ref[...], z_ref[...], preferred_element_type=F32)
            tr, ti = tt[:rows], tt[rows:]
            cw, sw = tc_ref[...], ts_ref[...]
            s2 = pl.ds(pl.multiple_of(t * ROW_TILE, ROW_TILE), ROW_TILE)
            ur_ref[:, s2, :] = (tr * cw + ti * sw).astype(ur_ref.dtype).reshape(n1, ROW_TILE, d)
            ui_ref[:, s2, :] = (ti * cw - tr * sw).astype(ui_ref.dtype).reshape(n1, ROW_TILE, d)

        pl.run_scoped(stage1, pltpu.VMEM((rows, d), BF16), pltpu.VMEM((2 * rows, d), BF16),
                      *_norm_scratch(rows, d))

    @pl.when(t >= n_tiles)
    def _():
        tb = t - n_tiles
        a, q = tb // n_q, tb % n_q

        @pl.when(q == 0)
        def _():
            def interleave(ys_ref):
                chunk_slabs = ys_ref.shape[0]
                chunk = chunk_slabs * LANES
                for cc in range(d // chunk):
                    cols = slice(cc * chunk, (cc + 1) * chunk)
                    for k in range(K1_GROUP):
                        k1 = a * K1_GROUP + k
                        u = jnp.concatenate([ur_ref[k1, :, cols], ui_ref[k1, :, cols]], axis=0)
                        y = jnp.dot(cs2_ref[...], u, preferred_element_type=F32) * inv_norm
                        for sl in range(chunk_slabs):
                            ys_ref[sl, pl.ds(k, n2, stride=K1_GROUP), :] = y[:, sl * LANES:(sl + 1) * LANES]
                    for sl in range(chunk_slabs):
                        c0 = cc * chunk + sl * LANES
                        yb_ref[:, c0:c0 + LANES] = ys_ref[sl].astype(yb_ref.dtype)

            pl.run_scoped(interleave, pltpu.VMEM((STAGE2_CHUNK // LANES, K1_GROUP * n2, LANES), F32))

        step_rows = xb_ref.shape[0] * K1_GROUP
        r0 = pl.multiple_of(q * step_rows, step_rows)
        proj = jnp.dot(yb_ref[pl.ds(r0, step_rows), :], wf_ref[...], preferred_element_type=F32) + bf_ref[...]
        out = xb_ref[...].reshape(proj.shape) + gate_ref[...] * proj
        out_ref[...] = out.reshape(out_ref.shape)


def _fourier_short(x, mod5, gains, tables, w_f, b_f, layer, w_layer):
    cs_chan, w1, tw_c, tw_s, cs2 = tables
    b, s, d = x.shape
    n2 = FFT_N2
    n1 = s // n2
    n_tiles = n2 // ROW_TILE
    n_a = n1 // K1_GROUP
    n_q = 4
    rows = n1 * ROW_TILE
    xa_view = x.reshape(b, n1, n_tiles, ROW_TILE, d)
    xb_view = x.reshape(b, n2, n_a, K1_GROUP, d)
    tile = lambda t: jnp.minimum(t, n_tiles - 1)
    late = lambda t: jnp.maximum(t - n_tiles, 0)
    xa_blk = pl.BlockSpec((None, n1, None, ROW_TILE, d), lambda bb, t: (bb, 0, tile(t), 0, 0))
    xb_blk = pl.BlockSpec((None, n2 // n_q, None, K1_GROUP, d),
                          lambda bb, t: (bb, late(t) % n_q, late(t) // n_q, 0, 0))
    tw = pl.BlockSpec((None, rows, 1), lambda bb, t: (tile(t), 0, 0))
    kern = functools.partial(_fourier_short_kernel, n1=n1, n_q=n_q, inv_norm=float(1.0 / np.sqrt(s)))
    out = pl.pallas_call(
        kern,
        out_shape=jax.ShapeDtypeStruct(xb_view.shape, F32),
        grid=(b, n_tiles + n_a * n_q),
        in_specs=[xa_blk, xb_blk, _row_in(layer, d, 2), _mod_in(layer, 0, d, 2), _mod_in(layer, 1, d, 2),
                  _mod_in(layer, 2, d, 2), _const_spec(cs_chan.shape, 2), _const_spec(w1.shape, 2), tw, tw,
                  _const_spec(cs2.shape, 2),
                  pl.BlockSpec((None, d, d), lambda bb, t: (w_layer, 0, 0), pipeline_mode=pl.Buffered(1)),
                  _row_in(w_layer, d, 2)],
        out_specs=xb_blk,
        scratch_shapes=[pltpu.VMEM((n1, n2, d), BF16), pltpu.VMEM((n1, n2, d), BF16),
                        pltpu.VMEM((K1_GROUP * n2, d), BF16)],
        compiler_params=_params(2, 1),
        name="fourier_mixer_short",
    )(xa_view, xb_view, gains, mod5, mod5, mod5, cs_chan, w1, tw_c, tw_s, cs2, w_f, b_f)
    return out.reshape(b, s, d)


def _pos_dft2_proj_kernel(ur_ref, ui_ref, cs2_ref, w_ref, b_ref, x_ref, gate_ref, out_ref, ys_ref, yb_ref, *,
                          inv_norm):
    n2 = cs2_ref.shape[0]
    d = ur_ref.shape[-1]
    chunk_slabs = ys_ref.shape[0]
    chunk = chunk_slabs * LANES

    @pl.when(pl.program_id(2) == 0)
    def _():
        for cc in range(d // chunk):
            for k in range(K1_GROUP):
                rows = slice(k * n2, (k + 1) * n2)
                cols = slice(cc * chunk, (cc + 1) * chunk)
                u = jnp.concatenate([ur_ref[rows, cols], ui_ref[rows, cols]], axis=0)
                y = jnp.dot(cs2_ref[...], u, preferred_element_type=F32) * inv_norm
                for sl in range(chunk_slabs):
                    ys_ref[sl, pl.ds(k, n2, stride=K1_GROUP), :] = y[:, sl * LANES:(sl + 1) * LANES]
            for sl in range(chunk_slabs):
                c0 = cc * chunk + sl * LANES
                yb_ref[:, c0:c0 + LANES] = ys_ref[sl].astype(yb_ref.dtype)

    step_rows = x_ref.shape[0] * K1_GROUP
    r0 = pl.multiple_of(pl.program_id(2) * step_rows, step_rows)
    proj = jnp.dot(yb_ref[pl.ds(r0, step_rows), :], w_ref[...], preferred_element_type=F32) + b_ref[...]
    out = x_ref[...].reshape(proj.shape) + gate_ref[...] * proj
    out_ref[...] = out.reshape(out_ref.shape)


def _pos_dft2_proj(ur, ui, cs2, x, mod5, w_f, b_f, layer, w_layer):
    b, s, d = x.shape
    n2 = FFT_N2
    n1 = s // n2
    n_q = 2
    rows = K1_GROUP * n2
    n_a = n1 // K1_GROUP

    def u_blk(first_step):
        def imap(bb, a, q):
            nxt = jnp.minimum(bb * n_a + a + (q >= first_step).astype(jnp.int32), b * n_a - 1)
            return nxt // n_a, nxt % n_a, 0
        return pl.BlockSpec((None, rows, d), imap)

    x_view = x.reshape(b, n2, n_a, K1_GROUP, d)
    x_blk = pl.BlockSpec((None, n2 // n_q, None, K1_GROUP, d), lambda bb, a, q: (bb, q, a, 0, 0))
    kern = functools.partial(_pos_dft2_proj_kernel, inv_norm=float(1.0 / np.sqrt(s)))
    out = pl.pallas_call(
        kern,
        out_shape=jax.ShapeDtypeStruct(x_view.shape, F32),
        grid=(b, n_a, n_q),
        in_specs=[u_blk(1), u_blk(1), _const_spec(cs2.shape, 3),
                  pl.BlockSpec((None, d, d), lambda bb, a, q: (w_layer, 0, 0), pipeline_mode=pl.Buffered(1)),
                  _row_in(w_layer, d, 3), x_blk, _mod_in(layer, 2, d, 3)],
        out_specs=x_blk,
        scratch_shapes=[pltpu.VMEM((STAGE2_CHUNK // LANES, rows, LANES), F32), pltpu.VMEM((rows, d), BF16)],
        compiler_params=_params(3, 2),
        name="fourier_pos_dft_stage2_proj",
    )(ur, ui, cs2, w_f, b_f, x_view, mod5)
    return out.reshape(b, s, d)


def _mlp_kernel(xn_ref, xr_ref, g_ref, sh_ref, sc_ref, gate_ref, w1_ref, b1_ref, w2_ref, b2_ref, fg_ref,
                out_ref, h_ref, acc_ref, rs_ref, ab_ref, *, tm, final_norm):
    j = pl.program_id(2)

    @pl.when(j == 0)
    def _():
        _ada_norm_rows(xn_ref, g_ref[...], sh_ref[...], sc_ref[...], h_ref, rs_ref, ab_ref, tm, zero_ref=acc_ref)

    u = jnp.dot(h_ref[...], w1_ref[...], preferred_element_type=F32) + b1_ref[...]
    u = jnp.maximum(u, 0.0)
    acc_ref[...] += jnp.dot((u * u).astype(BF16), w2_ref[...], preferred_element_type=F32)

    @pl.when(j == pl.num_programs(2) - 1)
    def _():
        y = xr_ref[...] + gate_ref[...] * (acc_ref[...] + b2_ref[...])
        if final_norm:
            ms = jnp.mean(y * y, axis=-1, keepdims=True)
            y = y * lax.rsqrt(ms + RMS_EPS) * fg_ref[...]
        out_ref[...] = y


def _mlp(x, mod5, gains, w1, b1, w2, b2, final_g, layer, tm, tf, final_norm):
    b, s, d = x.shape
    dff = w1.shape[-1]
    kern = functools.partial(_mlp_kernel, tm=tm, final_norm=final_norm)
    n_i, n_j = s // tm, dff // tf
    tok = pl.BlockSpec((None, tm, d), lambda bb, i, j: (bb, i, 0))

    def shifted(shift_of_j):
        def imap(bb, i, j):
            t = jnp.clip(bb * n_i + i + shift_of_j(j), 0, b * n_i - 1)
            return t // n_i, t % n_i, 0
        return pl.BlockSpec((None, tm, d), imap)

    x_norm = shifted(lambda j: (j >= n_j // 2).astype(jnp.int32))
    x_res = shifted(lambda j: -(j < 1).astype(jnp.int32))
    return pl.pallas_call(
        kern,
        out_shape=jax.ShapeDtypeStruct((b, s, d), F32),
        grid=(b, n_i, n_j),
        in_specs=[
            x_norm,
            x_res,
            _row_in(layer, d, 3),
            _mod_in(layer, 3, d, 3),
            _mod_in(layer, 4, d, 3),
            _mod_in(layer, 5, d, 3),
            pl.BlockSpec((None, d, tf), lambda bb, i, j: (layer, 0, j)),
            pl.BlockSpec((None, 1, tf), lambda bb, i, j: (layer, 0, j)),
            pl.BlockSpec((None, tf, d), lambda bb, i, j: (layer, j, 0)),
            _row_in(layer, d, 3),
            pl.BlockSpec((1, d), lambda bb, i, j: (0, 0)),
        ],
        out_specs=tok,
        scratch_shapes=[pltpu.VMEM((tm, d), BF16), pltpu.VMEM((tm, d), F32)] + _norm_scratch(tm, d),
        compiler_params=_params(3, 0),
        name="sqrelu_mlp",
    )(x, x, gains, mod5, mod5, mod5, w1, b1, w2, b2, final_g)


def _trunk(x, mod5, p):
    depth = p["w1"].shape[0]
    s = x.shape[1]
    cs_chan, w1c, tw_c, tw_s, cs2 = _dft_tables(s)
    for i in range(depth):
        sub = i // 2
        if i % 2 == 0:
            qkv = _norm_proj(x, mod5, p["norm1_g"], p["w_qkv"], p["perms"], i, sub, tm=1024, tn=768)
            outs, lses = zip(*[_group_attention(qkv, g, p["perms_t"]) for g in range(len(ATTN_WINDOWS))])
            x = _merge_proj(outs, lses, x, mod5, p["w_o"], i, sub, tm=512)
        else:
            if (s // FFT_N2) * ROW_TILE <= FUSED_STAGE1_ROWS:
                x = _fourier_short(x, mod5, p["norm1_g"], (cs_chan, w1c, tw_c, tw_s, cs2), p["w_f"], p["b_f"],
                                   i, sub)
            else:
                zr, zi = _chan_dft(x, mod5, p["norm1_g"], cs_chan, i, tm=512)
                ur, ui = _pos_dft1(zr, zi, w1c, tw_c, tw_s)
                x = _pos_dft2_proj(ur, ui, cs2, x, mod5, p["w_f"], p["b_f"], i, sub)
        x = _mlp(x, mod5, p["norm2_g"], p["w1"], p["b1"], p["w2"], p["b2"], p["final_g"], i,
                 tm=512, tf=1024, final_norm=(i == depth - 1))
    return x


def kernel(x_prompt, x_sample, c_prompt, c_sample, w_ada, b_ada, norm1_g, norm2_g, w_qkv, w_o, w_f, b_f,
           w1, b1, w2, b2, final_g):
    depth, d, _ = w_ada.shape
    n_p, n_s = c_prompt.shape[0], c_sample.shape[0]
    rows = -(-(n_p + n_s) // ROW_TILE) * ROW_TILE
    c_all = jnp.concatenate([c_prompt, c_sample, jnp.zeros((rows - n_p - n_s, d), F32)], axis=0)
    mod = _modulation(c_all, w_ada, b_ada)
    mod_p = mod[:, :n_p].reshape(depth, n_p, N_MOD, 1, d)
    mod_s = mod[:, n_p:n_p + n_s].reshape(depth, n_s, N_MOD, 1, d)

    row3 = lambda a: a.reshape(a.shape[0], 1, a.shape[-1])
    perms = _residue_perms()
    params = {
        "norm1_g": row3(norm1_g), "norm2_g": row3(norm2_g),
        "w_qkv": w_qkv.astype(BF16), "w_o": w_o.astype(BF16),
        "w_f": w_f.astype(BF16), "b_f": row3(b_f),
        "w1": w1.astype(BF16), "b1": row3(b1), "w2": w2.astype(BF16), "b2": row3(b2),
        "final_g": final_g.reshape(1, d),
        "perms": jnp.asarray(perms, BF16),
        "perms_t": jnp.asarray(np.transpose(perms, (0, 2, 1)), BF16),
    }
    return (_trunk(x_prompt, mod_p, params), _trunk(x_sample, mod_s, params))
```
